```python
import math
import jax, jax.numpy as jnp
from jax import lax
import numpy as np

D_MODEL = 2048
BATCH = 4
SEQ = 2048
DEPTH = 1

D_MIX = D_MODEL
D_ATTN = D_MIX // 2
D_SSM = D_MIX - D_ATTN
ATTN_HEAD_DIM = 64
V_HEAD_DIM = 2 * ATTN_HEAD_DIM
N_ATTN_HEADS = D_ATTN // V_HEAD_DIM
SSM_GROUP = 16
N_SSM_GROUPS = D_SSM // SSM_GROUP
SSM_STATE = 64
D_IN = 3 * D_ATTN + D_SSM
D_FF = 5632
N_BUCKETS = 32
MAX_DISTANCE = 128
Q_BLOCK = 128
RMS_EPS = 1e-6
DT_MIN = 1e-3
DT_MAX = 1e-1
NEG_INF = -1e30

kernel_name = "hybrid_macaron_diffattn_s5_block"


def rms_norm(x, g):
    xf = x.astype(jnp.float32)
    y = xf * lax.rsqrt(jnp.mean(xf * xf, axis=-1, keepdims=True) + RMS_EPS) * g.astype(jnp.float32)
    return y.astype(x.dtype)


def swiglu(x, w_gate, w_up, w_down):
    return (jax.nn.silu(x @ w_gate) * (x @ w_up)) @ w_down


def t5_bucket(n):
    max_exact = N_BUCKETS // 2
    nf = jnp.maximum(n, 1).astype(jnp.float32)
    large = max_exact + (jnp.log(nf / max_exact) / math.log(MAX_DISTANCE / max_exact)
                         * (N_BUCKETS - max_exact)).astype(jnp.int32)
    large = jnp.minimum(large, N_BUCKETS - 1)
    return jnp.where(n < max_exact, n, large)


def diff_attention(q, k, v, lam, rel_bias):
    L = q.shape[1]
    scale = ATTN_HEAD_DIM ** -0.5
    outs = []
    for i in range(L // Q_BLOCK):
        q0 = i * Q_BLOCK
        kl = q0 + Q_BLOCK
        qb = q[:, q0:kl]
        kb = k[:, :kl]
        vb = v[:, :kl].astype(jnp.float32)
        logits = jnp.einsum("bqmhd,bkmhd->bmhqk", qb, kb).astype(jnp.float32) * scale
        dist = (q0 + jnp.arange(Q_BLOCK))[:, None] - jnp.arange(kl)[None, :]
        bias = rel_bias[t5_bucket(jnp.maximum(dist, 0))].astype(jnp.float32)
        logits = logits + jnp.transpose(bias, (2, 0, 1))[None, None]
        logits = jnp.where(dist >= 0, logits, NEG_INF)
        p = jax.nn.softmax(logits, axis=-1)
        w = p[:, 0] - lam * p[:, 1]
        outs.append(jnp.einsum("bhqk,bkhe->bqhe", w, vb))
    return jnp.concatenate(outs, axis=1)


def s5_ssm(u, a_re, a_im, b_re, b_im, c_re, c_im, d_skip, log_dt):
    Bsz, L, _ = u.shape
    f32 = jnp.float32
    uf = u.astype(f32).reshape(Bsz, L, N_SSM_GROUPS, SSM_GROUP)
    ar = a_re.astype(f32)
    ai = a_im.astype(f32)
    dt = jnp.exp(log_dt.astype(f32))[:, None]
    decay = jnp.exp(dt * ar)
    ab_re = decay * jnp.cos(dt * ai)
    ab_im = decay * jnp.sin(dt * ai)
    den = ar * ar + ai * ai
    nr = ab_re - 1.0
    ni = ab_im
    coef_re = (nr * ar + ni * ai) / den
    coef_im = (ni * ar - nr * ai) / den
    br = b_re.astype(f32)
    bi = b_im.astype(f32)
    bb_re = coef_re[..., None] * br - coef_im[..., None] * bi
    bb_im = coef_re[..., None] * bi + coef_im[..., None] * br
    bu_re = jnp.einsum("blgp,gnp->blgn", uf, bb_re)
    bu_im = jnp.einsum("blgp,gnp->blgn", uf, bb_im)
    a_seq_re = jnp.broadcast_to(ab_re[None, None], (1, L, N_SSM_GROUPS, SSM_STATE))
    a_seq_im = jnp.broadcast_to(ab_im[None, None], (1, L, N_SSM_GROUPS, SSM_STATE))

    def combine(e1, e2):
        a1r, a1i, b1r, b1i = e1
        a2r, a2i, b2r, b2i = e2
        return (a2r * a1r - a2i * a1i,
                a2r * a1i + a2i * a1r,
                a2r * b1r - a2i * b1i + b2r,
                a2r * b1i + a2i * b1r + b2i)

    _, _, xr, xi = lax.associative_scan(combine, (a_seq_re, a_seq_im, bu_re, bu_im), axis=1)
    y = (jnp.einsum("gpn,blgn->blgp", c_re.astype(f32), xr)
         - jnp.einsum("gpn,blgn->blgp", c_im.astype(f32), xi)
         + d_skip.astype(f32) * uf)
    return y.reshape(Bsz, L, D_SSM)


def setup_inputs(seed: int = 0) -> dict:
    key = jax.random.key(seed)
    ks = jax.random.split(key, 32)
    f32 = jnp.float32
    nrm = lambda k, shape, s: jax.random.normal(k, shape, f32) * s
    gain = lambda k: 1.0 + 0.02 * jax.random.normal(k, (DEPTH, D_MODEL), f32)
    a_im = jnp.broadcast_to((math.pi * jnp.arange(SSM_STATE, dtype=f32))[None, None],
                            (DEPTH, N_SSM_GROUPS, SSM_STATE))
    return {
        "x": jax.random.normal(ks[0], (BATCH, SEQ, D_MODEL), f32),
        "ffn1_pre_g": gain(ks[1]),
        "ffn1_w_gate": nrm(ks[2], (DEPTH, D_MODEL, D_FF), D_MODEL ** -0.5),
        "ffn1_w_up": nrm(ks[3], (DEPTH, D_MODEL, D_FF), D_MODEL ** -0.5),
        "ffn1_w_down": nrm(ks[4], (DEPTH, D_FF, D_MODEL), D_FF ** -0.5),
        "ffn1_post_g": gain(ks[5]),
        "mix_pre_g": gain(ks[6]),
        "w_in": nrm(ks[7], (DEPTH, D_MODEL, D_IN), D_MODEL ** -0.5),
        "lambda_q1": nrm(ks[8], (DEPTH, ATTN_HEAD_DIM), 0.1),
        "lambda_k1": nrm(ks[9], (DEPTH, ATTN_HEAD_DIM), 0.1),
        "lambda_q2": nrm(ks[10], (DEPTH, ATTN_HEAD_DIM), 0.1),
        "lambda_k2": nrm(ks[11], (DEPTH, ATTN_HEAD_DIM), 0.1),
        "subln_g": 1.0 + 0.02 * jax.random.normal(ks[12], (DEPTH, V_HEAD_DIM), f32),
        "rel_bias": nrm(ks[13], (N_BUCKETS, N_ATTN_HEADS), 0.5),
        "ssm_a_re": -0.5 + 0.01 * jax.random.normal(ks[14], (DEPTH, N_SSM_GROUPS, SSM_STATE), f32),
        "ssm_a_im": a_im,
        "ssm_b_re": nrm(ks[15], (DEPTH, N_SSM_GROUPS, SSM_STATE, SSM_GROUP), (2.0 * SSM_GROUP) ** -0.5),
        "ssm_b_im": nrm(ks[16], (DEPTH, N_SSM_GROUPS, SSM_STATE, SSM_GROUP), (2.0 * SSM_GROUP) ** -0.5),
        "ssm_c_re": nrm(ks[17], (DEPTH, N_SSM_GROUPS, SSM_GROUP, SSM_STATE), SSM_STATE ** -0.5),
        "ssm_c_im": nrm(ks[18], (DEPTH, N_SSM_GROUPS, SSM_GROUP, SSM_STATE), SSM_STATE ** -0.5),
        "ssm_d": jax.random.normal(ks[19], (DEPTH, N_SSM_GROUPS, SSM_GROUP), f32),
        "ssm_log_dt": jax.random.uniform(ks[20], (DEPTH, N_SSM_GROUPS), f32,
                                         math.log(DT_MIN), math.log(DT_MAX)),
        "w_glu": nrm(ks[21], (DEPTH, D_SSM, D_SSM), D_SSM ** -0.5),
        "b_glu": nrm(ks[22], (DEPTH, D_SSM), 0.01),
        "w_out": nrm(ks[23], (DEPTH, D_MIX, D_MODEL), D_MIX ** -0.5),
        "mix_post_g": gain(ks[24]),
        "ffn2_pre_g": gain(ks[25]),
        "ffn2_w_gate": nrm(ks[26], (DEPTH, D_MODEL, D_FF), D_MODEL ** -0.5),
        "ffn2_w_up": nrm(ks[27], (DEPTH, D_MODEL, D_FF), D_MODEL ** -0.5),
        "ffn2_w_down": nrm(ks[28], (DEPTH, D_FF, D_MODEL), D_FF ** -0.5),
        "ffn2_post_g": gain(ks[29]),
    }


def reference(x, ffn1_pre_g, ffn1_w_gate, ffn1_w_up, ffn1_w_down, ffn1_post_g,
              mix_pre_g, w_in, lambda_q1, lambda_k1, lambda_q2, lambda_k2, subln_g,
              rel_bias, ssm_a_re, ssm_a_im, ssm_b_re, ssm_b_im, ssm_c_re, ssm_c_im,
              ssm_d, ssm_log_dt, w_glu, b_glu, w_out, mix_post_g,
              ffn2_pre_g, ffn2_w_gate, ffn2_w_up, ffn2_w_down, ffn2_post_g):
    Bsz, L, _ = x.shape
    for l in range(DEPTH):
        h = swiglu(rms_norm(x, ffn1_pre_g[l]), ffn1_w_gate[l], ffn1_w_up[l], ffn1_w_down[l])
        x = x + 0.5 * rms_norm(h, ffn1_post_g[l])

        h = rms_norm(x, mix_pre_g[l])
        proj = h @ w_in[l]
        q = proj[..., :D_ATTN].reshape(Bsz, L, 2, N_ATTN_HEADS, ATTN_HEAD_DIM)
        k = proj[..., D_ATTN:2 * D_ATTN].reshape(Bsz, L, 2, N_ATTN_HEADS, ATTN_HEAD_DIM)
        v = proj[..., 2 * D_ATTN:3 * D_ATTN].reshape(Bsz, L, N_ATTN_HEADS, V_HEAD_DIM)
        u = proj[..., 3 * D_ATTN:]

        lambda_init = 0.8 - 0.6 * math.exp(-0.3 * l)
        lam = (jnp.exp(jnp.sum(lambda_q1[l].astype(jnp.float32) * lambda_k1[l].astype(jnp.float32)))
               - jnp.exp(jnp.sum(lambda_q2[l].astype(jnp.float32) * lambda_k2[l].astype(jnp.float32)))
               + lambda_init)
        a = diff_attention(q, k, v, lam, rel_bias)
        a = rms_norm(a, subln_g[l]) * (1.0 - lambda_init)
        a = a.reshape(Bsz, L, D_ATTN).astype(x.dtype)

        y = s5_ssm(u, ssm_a_re[l], ssm_a_im[l], ssm_b_re[l], ssm_b_im[l],
                   ssm_c_re[l], ssm_c_im[l], ssm_d[l], ssm_log_dt[l])
        z = jax.nn.gelu(y).astype(x.dtype)
        s = z * jax.nn.sigmoid(z @ w_glu[l] + b_glu[l])

        m = jnp.concatenate([a, s], axis=-1) @ w_out[l]
        x = x + rms_norm(m, mix_post_g[l])

        h = swiglu(rms_norm(x, ffn2_pre_g[l]), ffn2_w_gate[l], ffn2_w_up[l], ffn2_w_down[l])
        x = x + 0.5 * rms_norm(h, ffn2_post_g[l])
    return x
```

```python
import functools
import math

import jax
import jax.numpy as jnp
from jax import lax
from jax.experimental import pallas as pl
from jax.experimental.pallas import tpu as pltpu

V7X_LANES = 128
V7X_SUBLANES = 8
V7X_VMEM_LIMIT_BYTES = 56 * 1024 * 1024

RMS_EPS = 1e-6
NEG_INF = -1e30
N_BUCKETS = 32
MAX_DISTANCE = 128
ATTN_HEAD_DIM = 64
V_HEAD_DIM = 128
SSM_GROUP = 16
SSM_STATE = 64
LAMBDA_INIT = 0.8 - 0.6 * math.exp(-0.3 * 0)

F32 = jnp.float32
BF16 = jnp.bfloat16


def _params(semantics):
    return pltpu.CompilerParams(dimension_semantics=semantics,
                                vmem_limit_bytes=V7X_VMEM_LIMIT_BYTES)


def _rms_scale(v):
    return lax.rsqrt(jnp.mean(v * v, axis=-1, keepdims=True) + RMS_EPS)


def _ffn_kernel(x_ref, pre_g_ref, wg_ref, wu_ref, wd_ref, post_g_ref, o_ref, xn_ref, acc_ref):
    k = pl.program_id(1)

    @pl.when(k == 0)
    def _():
        xf = x_ref[...]
        xn_ref[...] = (xf * _rms_scale(xf) * pre_g_ref[...]).astype(BF16)
        acc_ref[...] = jnp.zeros_like(acc_ref)

    xn = xn_ref[...]
    gate = jnp.dot(xn, wg_ref[...], preferred_element_type=F32)
    up = jnp.dot(xn, wu_ref[...], preferred_element_type=F32)
    hidden = (gate * jax.nn.sigmoid(gate) * up).astype(BF16)
    acc_ref[...] += jnp.dot(hidden, wd_ref[...], preferred_element_type=F32)

    @pl.when(k == pl.num_programs(1) - 1)
    def _():
        h = acc_ref[...]
        o_ref[...] = x_ref[...] + 0.5 * (h * _rms_scale(h) * post_g_ref[...])


def _ffn(x, pre_g, wg, wu, wd, post_g, *, tm, tf):
    t, d = x.shape
    f = wg.shape[1]
    return pl.pallas_call(
        _ffn_kernel,
        grid=(t // tm, f // tf),
        in_specs=[
            pl.BlockSpec((tm, d), lambda i, k: (i, 0)),
            pl.BlockSpec((1, d), lambda i, k: (0, 0)),
            pl.BlockSpec((d, tf), lambda i, k: (0, k)),
            pl.BlockSpec((d, tf), lambda i, k: (0, k)),
            pl.BlockSpec((tf, d), lambda i, k: (k, 0)),
            pl.BlockSpec((1, d), lambda i, k: (0, 0)),
        ],
        out_specs=pl.BlockSpec((tm, d), lambda i, k: (i, 0)),
        out_shape=jax.ShapeDtypeStruct((t, d), F32),
        scratch_shapes=[pltpu.VMEM((tm, d), BF16), pltpu.VMEM((tm, d), F32)],
        compiler_params=_params(("parallel", "arbitrary")),
        name="ffn",
    )(x, pre_g, wg, wu, wd, post_g)


def _in_proj_kernel(x_ref, g_ref, w_ref, qkv_ref, u_ref, hn_ref, *, q_scale):
    n = pl.program_id(1)

    @pl.when(n == 0)
    def _():
        xf = x_ref[...]
        hn_ref[...] = (xf * _rms_scale(xf) * g_ref[...]).astype(BF16)

    acc = jnp.dot(hn_ref[...], w_ref[...], preferred_element_type=F32)

    @pl.when(n == 0)
    def _():
        qkv_ref[...] = (acc * q_scale).astype(BF16)

    @pl.when(jnp.logical_and(n > 0, n < 3))
    def _():
        qkv_ref[...] = acc.astype(BF16)

    @pl.when(n == 3)
    def _():
        u_ref[...] = acc


def _in_proj(x, g, w, *, tm, q_scale):
    t, d = x.shape
    tn = w.shape[1] // 4
    return pl.pallas_call(
        functools.partial(_in_proj_kernel, q_scale=q_scale),
        grid=(t // tm, 4),
        in_specs=[
            pl.BlockSpec((tm, d), lambda i, n: (i, 0)),
            pl.BlockSpec((1, d), lambda i, n: (0, 0)),
            pl.BlockSpec((d, tn), lambda i, n: (0, n)),
        ],
        out_specs=[
            pl.BlockSpec((tm, tn), lambda i, n: (i, jnp.minimum(n, 2))),
            pl.BlockSpec((tm, tn), lambda i, n: (i, 0)),
        ],
        out_shape=[jax.ShapeDtypeStruct((t, 3 * tn), BF16),
                   jax.ShapeDtypeStruct((t, tn), F32)],
        scratch_shapes=[pltpu.VMEM((tm, d), BF16)],
        compiler_params=_params(("parallel", "arbitrary")),
        name="in_proj",
    )(x, g, w)


def _bias_band_kernel(rb_ref, o_ref, *, tq):
    h = pl.program_id(0)
    row = lax.broadcasted_iota(jnp.int32, (tq, 2 * tq), 0)
    col = lax.broadcasted_iota(jnp.int32, (tq, 2 * tq), 1)
    n = jnp.maximum(tq + row - col, 0)
    max_exact = N_BUCKETS // 2
    nf = jnp.maximum(n, 1).astype(F32)
    large = max_exact + (jnp.log(nf / max_exact) / math.log(MAX_DISTANCE / max_exact)
                         * (N_BUCKETS - max_exact)).astype(jnp.int32)
    large = jnp.minimum(large, N_BUCKETS - 1)
    bucket = jnp.where(n < max_exact, n, large)
    val = jnp.zeros((tq, 2 * tq), F32)
    for b in range(N_BUCKETS):
        val = jnp.where(bucket == b, rb_ref[b, h], val)
    o_ref[0] = val


def _bias_band(rel_bias, *, tq):
    n_heads = rel_bias.shape[1]
    return pl.pallas_call(
        functools.partial(_bias_band_kernel, tq=tq),
        grid=(n_heads,),
        in_specs=[pl.BlockSpec(memory_space=pltpu.SMEM)],
        out_specs=pl.BlockSpec((1, tq, 2 * tq), lambda h: (h, 0, 0)),
        out_shape=jax.ShapeDtypeStruct((n_heads, tq, 2 * tq), F32),
        compiler_params=_params(("arbitrary",)),
        name="bias_band",
    )(rel_bias)


def _attn_kernel(rb_ref, lq1_ref, lk1_ref, lq2_ref, lk2_ref, q1_ref, q2_ref, k1_ref, k2_ref,
                 v_ref, band_ref, g_ref, o_ref, *, tq):
    h = pl.program_id(1)
    i = pl.program_id(2)
    lane = lax.broadcasted_iota(jnp.int32, (tq, 2 * ATTN_HEAD_DIM), 1)
    mine = (lane >= ATTN_HEAD_DIM) == (lax.rem(h, 2) == 1)
    q1 = jnp.where(mine, q1_ref[0].astype(F32), 0.0).astype(BF16)
    q2 = jnp.where(mine, q2_ref[0].astype(F32), 0.0).astype(BF16)

    def scores(q, k_ref, start):
        kc = k_ref[0, pl.ds(start, tq), :]
        return lax.dot_general(q, kc, (((1,), (1,)), ((), ())), preferred_element_type=F32)

    def update(s, vc, state):
        m, l, acc = state
        m_new = jnp.maximum(m, jnp.max(s, axis=-1, keepdims=True))
        alpha = jnp.exp(m - m_new)
        p = jnp.exp(s - m_new)
        l_new = alpha * l + jnp.sum(p, axis=-1, keepdims=True)
        acc_new = alpha * acc + jnp.dot(p.astype(BF16), vc, preferred_element_type=F32)
        return m_new, l_new, acc_new

    def chunk(start, bias, mask, state1, state2):
        vc = v_ref[0, pl.ds(start, tq), :]
        s1 = scores(q1, k1_ref, start) + bias
        s2 = scores(q2, k2_ref, start) + bias
        if mask is not None:
            s1 = jnp.where(mask, s1, NEG_INF)
            s2 = jnp.where(mask, s2, NEG_INF)
        return update(s1, vc, state1), update(s2, vc, state2)

    init = (jnp.full((tq, 1), NEG_INF, F32), jnp.zeros((tq, 1), F32),
            jnp.zeros((tq, V_HEAD_DIM), F32))

    far_bias = rb_ref[N_BUCKETS - 1, h]

    def far_body(j, carry):
        s1, s2 = carry
        return chunk(pl.multiple_of(j * tq, tq), far_bias, None, s1, s2)

    state1, state2 = lax.fori_loop(0, jnp.maximum(i - 1, 0), far_body, (init, init))

    prev_start = pl.multiple_of(jnp.maximum(i - 1, 0) * tq, tq)
    n1, n2 = chunk(prev_start, band_ref[0, :, :tq], None, state1, state2)
    has_prev = i > 0
    state1 = jax.tree.map(lambda a, b: jnp.where(has_prev, a, b), n1, state1)
    state2 = jax.tree.map(lambda a, b: jnp.where(has_prev, a, b), n2, state2)

    row = lax.broadcasted_iota(jnp.int32, (tq, tq), 0)
    col = lax.broadcasted_iota(jnp.int32, (tq, tq), 1)
    state1, state2 = chunk(pl.multiple_of(i * tq, tq), band_ref[0, :, tq:], col <= row,
                           state1, state2)

    lam = (jnp.exp(jnp.sum(lq1_ref[...] * lk1_ref[...], axis=-1, keepdims=True))
           - jnp.exp(jnp.sum(lq2_ref[...] * lk2_ref[...], axis=-1, keepdims=True)) + LAMBDA_INIT)
    _, l1, acc1 = state1
    _, l2, acc2 = state2
    a = acc1 / l1 - lam * (acc2 / l2)
    a = a * _rms_scale(a) * g_ref[...] * (1.0 - LAMBDA_INIT)
    o_ref[0] = a.astype(o_ref.dtype)


def _attention(qkv, rel_bias, band, lq1, lk1, lq2, lk2, subln_g, *, batch, seq, n_heads, tq):
    d_attn = n_heads * V_HEAD_DIM
    pair = 2 * ATTN_HEAD_DIM
    n_pairs = d_attn // 2 // pair
    qkv3 = qkv.reshape(batch, seq, 3 * d_attn)
    small = pl.BlockSpec((1, ATTN_HEAD_DIM), lambda b, h, i: (0, 0))
    return pl.pallas_call(
        functools.partial(_attn_kernel, tq=tq),
        grid=(batch, n_heads, seq // tq),
        in_specs=[
            pl.BlockSpec(memory_space=pltpu.SMEM),
            small, small, small, small,
            pl.BlockSpec((1, tq, pair), lambda b, h, i: (b, i, h // 2)),
            pl.BlockSpec((1, tq, pair), lambda b, h, i: (b, i, n_pairs + h // 2)),
            pl.BlockSpec((1, seq, pair), lambda b, h, i: (b, 0, 2 * n_pairs + h // 2)),
            pl.BlockSpec((1, seq, pair), lambda b, h, i: (b, 0, 3 * n_pairs + h // 2)),
            pl.BlockSpec((1, seq, V_HEAD_DIM), lambda b, h, i: (b, 0, 4 * n_pairs + h)),
            pl.BlockSpec((1, tq, 2 * tq), lambda b, h, i: (h, 0, 0)),
            pl.BlockSpec((1, V_HEAD_DIM), lambda b, h, i: (0, 0)),
        ],
        out_specs=pl.BlockSpec((1, tq, V_HEAD_DIM), lambda b, h, i: (b, i, h)),
        out_shape=jax.ShapeDtypeStruct((batch, seq, d_attn), BF16),
        compiler_params=_params(("parallel", "parallel", "arbitrary")),
        name="diff_attention",
    )(rel_bias, lq1, lk1, lq2, lk2, qkv3, qkv3, qkv3, qkv3, qkv3, band, subln_g)


def _ssm_params_kernel(a_re_ref, a_im_ref, log_dt_ref, b_re_ref, b_im_ref,
                       ab_re_ref, ab_im_ref, bb_re_ref, bb_im_ref):
    ar = a_re_ref[...]
    ai = a_im_ref[...]
    dt = jnp.exp(log_dt_ref[...])
    decay = jnp.exp(dt * ar)
    ab_re = decay * jnp.cos(dt * ai)
    ab_im = decay * jnp.sin(dt * ai)
    den = ar * ar + ai * ai
    nr = ab_re - 1.0
    ni = ab_im
    coef_re = (nr * ar + ni * ai) / den
    coef_im = (ni * ar - nr * ai) / den
    br = b_re_ref[...]
    bi = b_im_ref[...]
    ab_re_ref[...] = ab_re
    ab_im_ref[...] = ab_im
    bb_re_ref[...] = coef_re * br - coef_im * bi
    bb_im_ref[...] = coef_re * bi + coef_im * br


def _ssm_params(a_re, a_im, log_dt, b_re_t, b_im_t):
    g, _, n = a_re.shape
    p = b_re_t.shape[1]
    return pl.pallas_call(
        _ssm_params_kernel,
        out_shape=[jax.ShapeDtypeStruct((g, 1, n), F32), jax.ShapeDtypeStruct((g, 1, n), F32),
                   jax.ShapeDtypeStruct((g, p, n), F32), jax.ShapeDtypeStruct((g, p, n), F32)],
        name="ssm_params",
    )(a_re, a_im, log_dt, b_re_t, b_im_t)


SSM_COL_CHUNK = 512
SSM_BLOCKS = 4


def _ssm_input_drive(u_ref, lhs_ref, bmat_re_ref, bmat_im_ref, bu_re_ref, bu_im_ref, *, tb):
    d_ssm = lhs_ref.shape[1]
    for t in range(tb):
        lhs_ref[t * V7X_SUBLANES:(t + 1) * V7X_SUBLANES, :] = u_ref[0, :, t * d_ssm:(t + 1) * d_ssm]
    kin = d_ssm // SSM_BLOCKS
    kst = bu_re_ref.shape[1] // SSM_BLOCKS
    for j in range(SSM_BLOCKS):
        uj = lhs_ref[:, j * kin:(j + 1) * kin].astype(BF16)
        bu_re_ref[:, j * kst:(j + 1) * kst] = jnp.dot(uj, bmat_re_ref[j], preferred_element_type=F32)
        bu_im_ref[:, j * kst:(j + 1) * kst] = jnp.dot(uj, bmat_im_ref[j], preferred_element_type=F32)


def _ssm_recurrence(ab_re_ref, ab_im_ref, st_re_ref, st_im_ref, bu_re_ref, bu_im_ref, *, tb, store):
    n_cols = st_re_ref.shape[1]
    for c in range(n_cols // SSM_COL_CHUNK):
        cs = slice(c * SSM_COL_CHUNK, (c + 1) * SSM_COL_CHUNK)
        ar = jnp.broadcast_to(ab_re_ref[:, cs], (V7X_SUBLANES, SSM_COL_CHUNK))
        ai = jnp.broadcast_to(ab_im_ref[:, cs], (V7X_SUBLANES, SSM_COL_CHUNK))

        def step(t, carry, cs=cs, ar=ar, ai=ai):
            xr, xi = carry
            rows = pl.ds(pl.multiple_of(t * V7X_SUBLANES, V7X_SUBLANES), V7X_SUBLANES)
            nxr = ar * xr - ai * xi + bu_re_ref[rows, cs]
            nxi = ar * xi + ai * xr + bu_im_ref[rows, cs]
            if store:
                bu_re_ref[rows, cs] = nxr
                bu_im_ref[rows, cs] = nxi
            return nxr, nxi

        xr, xi = lax.fori_loop(0, tb, step, (st_re_ref[:, cs], st_im_ref[:, cs]), unroll=4)
        st_re_ref[:, cs] = xr
        st_im_ref[:, cs] = xi


def _ssm_ends_kernel(u_ref, bmat_re_ref, bmat_im_ref, ab_re_ref, ab_im_ref, end_re_ref, end_im_ref,
                     lhs_ref, bu_re_ref, bu_im_ref, st_re_ref, st_im_ref, *, tb):
    tblk = pl.program_id(1)

    @pl.when(tblk == 0)
    def _():
        st_re_ref[...] = jnp.zeros_like(st_re_ref)
        st_im_ref[...] = jnp.zeros_like(st_im_ref)

    _ssm_input_drive(u_ref, lhs_ref, bmat_re_ref, bmat_im_ref, bu_re_ref, bu_im_ref, tb=tb)
    _ssm_recurrence(ab_re_ref, ab_im_ref, st_re_ref, st_im_ref, bu_re_ref, bu_im_ref, tb=tb, store=False)

    @pl.when(tblk == pl.num_programs(1) - 1)
    def _():
        end_re_ref[0] = st_re_ref[...]
        end_im_ref[0] = st_im_ref[...]


def _ssm_main_kernel(u_ref, end_re_ref, end_im_ref, bmat_re_ref, bmat_im_ref, ab_re_ref, ab_im_ref,
                     cmat_re_ref, cmat_im_ref, d_ref, wglu_ref, bglu_ref, o_ref,
                     lhs_ref, bu_re_ref, bu_im_ref, st_re_ref, st_im_ref, *, tb, seg_len):
    tblk = pl.program_id(1)
    n_seg = st_re_ref.shape[0]

    @pl.when(tblk == 0)
    def _():
        pr = ab_re_ref[...]
        pi = ab_im_ref[...]
        for _ in range(int(math.log2(seg_len))):
            pr, pi = pr * pr - pi * pi, 2.0 * pr * pi
        er = jnp.zeros_like(pr)
        ei = jnp.zeros_like(pi)
        st_re_ref[0:1, :] = er
        st_im_ref[0:1, :] = ei
        for j in range(1, n_seg):
            lr = end_re_ref[0, j - 1:j, :]
            li = end_im_ref[0, j - 1:j, :]
            er, ei = lr + pr * er - pi * ei, li + pr * ei + pi * er
            st_re_ref[j:j + 1, :] = er
            st_im_ref[j:j + 1, :] = ei

    _ssm_input_drive(u_ref, lhs_ref, bmat_re_ref, bmat_im_ref, bu_re_ref, bu_im_ref, tb=tb)
    _ssm_recurrence(ab_re_ref, ab_im_ref, st_re_ref, st_im_ref, bu_re_ref, bu_im_ref, tb=tb, store=True)

    d_ssm = lhs_ref.shape[1]
    kin = d_ssm // SSM_BLOCKS
    kst = bu_re_ref.shape[1] // SSM_BLOCKS
    ys = []
    for j in range(SSM_BLOCKS):
        xr = bu_re_ref[:, j * kst:(j + 1) * kst].astype(BF16)
        xi = bu_im_ref[:, j * kst:(j + 1) * kst].astype(BF16)
        ys.append(jnp.dot(xr, cmat_re_ref[j], preferred_element_type=F32)
                  - jnp.dot(xi, cmat_im_ref[j], preferred_element_type=F32))
    y = jnp.concatenate(ys, axis=-1) + d_ref[...] * lhs_ref[...]
    z = jax.nn.gelu(y)
    gate = jnp.dot(z.astype(BF16), wglu_ref[...], preferred_element_type=F32) + bglu_ref[...]
    s = z * jax.nn.sigmoid(gate)
    for t in range(tb):
        o_ref[0, :, t * d_ssm:(t + 1) * d_ssm] = s[t * V7X_SUBLANES:(t + 1) * V7X_SUBLANES, :]


def _ssm(u_seg, bmat_re, bmat_im, ab_re, ab_im, cmat_re, cmat_im, d_skip, w_glu, b_glu, *, d_ssm, tb):
    batch, n_seg, flat = u_seg.shape
    seg_len = flat // d_ssm
    n_cols = ab_re.shape[1]
    rows = tb * n_seg
    grid = (batch, seg_len // tb)
    u_spec = pl.BlockSpec((1, n_seg, tb * d_ssm), lambda b, t: (b, 0, t))
    end_spec = pl.BlockSpec((1, n_seg, n_cols), lambda b, t: (b, 0, 0))

    def whole(a):
        return pl.BlockSpec(a.shape, lambda b, t: (0,) * a.ndim)

    scratch = [pltpu.VMEM((rows, d_ssm), F32), pltpu.VMEM((rows, n_cols), F32),
               pltpu.VMEM((rows, n_cols), F32), pltpu.VMEM((n_seg, n_cols), F32),
               pltpu.VMEM((n_seg, n_cols), F32)]
    end_re, end_im = pl.pallas_call(
        functools.partial(_ssm_ends_kernel, tb=tb),
        grid=grid,
        in_specs=[u_spec, whole(bmat_re), whole(bmat_im), whole(ab_re), whole(ab_im)],
        out_specs=[end_spec, end_spec],
        out_shape=[jax.ShapeDtypeStruct((batch, n_seg, n_cols), F32)] * 2,
        scratch_shapes=scratch,
        compiler_params=_params(("parallel", "arbitrary")),
        name="ssm_ends",
    )(u_seg, bmat_re, bmat_im, ab_re, ab_im)
    return pl.pallas_call(
        functools.partial(_ssm_main_kernel, tb=tb, seg_len=seg_len),
        grid=grid,
        in_specs=[u_spec, end_spec, end_spec, whole(bmat_re), whole(bmat_im), whole(ab_re), whole(ab_im),
                  whole(cmat_re), whole(cmat_im), whole(d_skip), whole(w_glu), whole(b_glu)],
        out_specs=u_spec,
        out_shape=jax.ShapeDtypeStruct(u_seg.shape, F32),
        scratch_shapes=scratch,
        compiler_params=_params(("parallel", "arbitrary")),
        name="ssm_main",
    )(u_seg, end_re, end_im, bmat_re, bmat_im, ab_re, ab_im, cmat_re, cmat_im, d_skip, w_glu, b_glu)


def _out_proj_kernel(a_ref, s_ref, wa_ref, ws_ref, x_ref, g_ref, o_ref):
    m = (jnp.dot(a_ref[...], wa_ref[...], preferred_element_type=F32)
         + jnp.dot(s_ref[...].astype(BF16), ws_ref[...], preferred_element_type=F32))
    o_ref[...] = x_ref[...] + m * _rms_scale(m) * g_ref[...]


def _out_proj(a, s, wa, ws, x, g, *, tm):
    t, d = x.shape
    da = a.shape[1]
    ds = s.shape[1]
    return pl.pallas_call(
        _out_proj_kernel,
        grid=(t // tm,),
        in_specs=[
            pl.BlockSpec((tm, da), lambda i: (i, 0)),
            pl.BlockSpec((tm, ds), lambda i: (i, 0)),
            pl.BlockSpec((da, d), lambda i: (0, 0)),
            pl.BlockSpec((ds, d), lambda i: (0, 0)),
            pl.BlockSpec((tm, d), lambda i: (i, 0)),
            pl.BlockSpec((1, d), lambda i: (0, 0)),
        ],
        out_specs=pl.BlockSpec((tm, d), lambda i: (i, 0)),
        out_shape=jax.ShapeDtypeStruct((t, d), F32),
        compiler_params=_params(("parallel",)),
        name="out_proj",
    )(a, s, wa, ws, x, g)


def _block_diag(w, n_blocks, spec):
    g = w.shape[0]
    gl = g // n_blocks
    eye = jnp.eye(gl, dtype=w.dtype)
    w4 = w.reshape(n_blocks, gl, w.shape[1], w.shape[2])
    out = jnp.einsum(spec, w4, eye)
    return out.reshape(n_blocks, out.shape[1] * out.shape[2], out.shape[3] * out.shape[4]).astype(BF16)


def kernel(x, ffn1_pre_g, ffn1_w_gate, ffn1_w_up, ffn1_w_down, ffn1_post_g, mix_pre_g, w_in, lambda_q1, lambda_k1, lambda_q2, lambda_k2, subln_g, rel_bias, ssm_a_re, ssm_a_im, ssm_b_re, ssm_b_im, ssm_c_re, ssm_c_im, ssm_d, ssm_log_dt, w_glu, b_glu, w_out, mix_post_g, ffn2_pre_g, ffn2_w_gate, ffn2_w_up, ffn2_w_down, ffn2_post_g):
    batch, seq, d_model = x.shape
    depth = ffn1_pre_g.shape[0]
    assert depth == 1, "LAMBDA_INIT is specialised to a single layer"
    n_heads = rel_bias.shape[1]
    d_attn = n_heads * V_HEAD_DIM
    n_groups, n_state = ssm_a_re.shape[1:]
    d_ssm = n_groups * SSM_GROUP
    n_seg = V7X_SUBLANES
    seg_len = seq // n_seg
    tokens = batch * seq
    tm, tf, tq, tb = 512, 512, 256, 32

    xt = x.reshape(tokens, d_model)
    row = lambda v: v.reshape(1, -1)
    l = 0

    x1 = _ffn(xt, row(ffn1_pre_g[l]), ffn1_w_gate[l].astype(BF16), ffn1_w_up[l].astype(BF16),
              ffn1_w_down[l].astype(BF16), row(ffn1_post_g[l]), tm=tm, tf=tf)

    qkv, u = _in_proj(x1, row(mix_pre_g[l]), w_in[l].astype(BF16), tm=tm,
                      q_scale=ATTN_HEAD_DIM ** -0.5)

    band = _bias_band(rel_bias, tq=tq)
    a = _attention(qkv, rel_bias, band, row(lambda_q1[l]), row(lambda_k1[l]), row(lambda_q2[l]),
                   row(lambda_k2[l]), row(subln_g[l]), batch=batch, seq=seq, n_heads=n_heads, tq=tq)

    ab_re, ab_im, bb_re, bb_im = _ssm_params(
        ssm_a_re[l].reshape(n_groups, 1, n_state), ssm_a_im[l].reshape(n_groups, 1, n_state),
        ssm_log_dt[l].reshape(n_groups, 1, 1),
        jnp.swapaxes(ssm_b_re[l], 1, 2), jnp.swapaxes(ssm_b_im[l], 1, 2))
    bmat_re = _block_diag(bb_re, SSM_BLOCKS, "jgpn,gh->jgphn")
    bmat_im = _block_diag(bb_im, SSM_BLOCKS, "jgpn,gh->jgphn")
    cmat_re = _block_diag(ssm_c_re[l], SSM_BLOCKS, "jgpn,gh->jhngp")
    cmat_im = _block_diag(ssm_c_im[l], SSM_BLOCKS, "jgpn,gh->jhngp")
    u_seg = u.reshape(batch, n_seg, seg_len * d_ssm)
    s_seg = _ssm(u_seg, bmat_re, bmat_im, ab_re.reshape(1, -1), ab_im.reshape(1, -1), cmat_re, cmat_im,
                 row(ssm_d[l]), w_glu[l].astype(BF16), row(b_glu[l]), d_ssm=d_ssm, tb=tb)
    s = s_seg.reshape(tokens, d_ssm)

    wo = w_out[l].astype(BF16)
    x2 = _out_proj(a.reshape(tokens, d_attn), s, wo[:d_attn], wo[d_attn:], x1, row(mix_post_g[l]), tm=tm)

    x3 = _ffn(x2, row(ffn2_pre_g[l]), ffn2_w_gate[l].astype(BF16), ffn2_w_up[l].astype(BF16),
              ffn2_w_down[l].astype(BF16), row(ffn2_post_g[l]), tm=tm, tf=tf)
    return x3.reshape(batch, seq, d_model)
```

```python
import functools
import math

import jax
import jax.numpy as jnp
from jax import lax
from jax.experimental import pallas as pl
from jax.experimental.pallas import tpu as pltpu

V7X_LANES = 128
V7X_SUBLANES = 8
V7X_VMEM_LIMIT_BYTES = 56 * 1024 * 1024

RMS_EPS = 1e-6
NEG_INF = -1e30
N_BUCKETS = 32
MAX_DISTANCE = 128
ATTN_HEAD_DIM = 64
V_HEAD_DIM = 128
SSM_GROUP = 16
SSM_STATE = 64
LAMBDA_INIT = 0.8 - 0.6 * math.exp(-0.3 * 0)

F32 = jnp.float32
BF16 = jnp.bfloat16


def _params(semantics):
    return pltpu.CompilerParams(dimension_semantics=semantics,
                                vmem_limit_bytes=V7X_VMEM_LIMIT_BYTES)


def _rms_scale(v):
    return lax.rsqrt(jnp.mean(v * v, axis=-1, keepdims=True) + RMS_EPS)


def _ffn_kernel(x_ref, pre_g_ref, wg_ref, wu_ref, wd_ref, post_g_ref, o_ref, xn_ref, acc_ref):
    k = pl.program_id(1)

    @pl.when(k == 0)
    def _():
        xf = x_ref[...]
        xn_ref[...] = (xf * _rms_scale(xf) * pre_g_ref[...]).astype(BF16)
        acc_ref[...] = jnp.zeros_like(acc_ref)

    xn = xn_ref[...]
    gate = jnp.dot(xn, wg_ref[...], preferred_element_type=F32)
    up = jnp.dot(xn, wu_ref[...], preferred_element_type=F32)
    hidden = (gate * jax.nn.sigmoid(gate) * up).astype(BF16)
    acc_ref[...] += jnp.dot(hidden, wd_ref[...], preferred_element_type=F32)

    @pl.when(k == pl.num_programs(1) - 1)
    def _():
        h = acc_ref[...]
        o_ref[...] = x_ref[...] + 0.5 * (h * _rms_scale(h) * post_g_ref[...])


def _ffn(x, pre_g, wg, wu, wd, post_g, *, tm, tf):
    t, d = x.shape
    f = wg.shape[1]
    return pl.pallas_call(
        _ffn_kernel,
        grid=(t // tm, f // tf),
        in_specs=[
            pl.BlockSpec((tm, d), lambda i, k: (i, 0)),
            pl.BlockSpec((1, d), lambda i, k: (0, 0)),
            pl.BlockSpec((d, tf), lambda i, k: (0, k)),
            pl.BlockSpec((d, tf), lambda i, k: (0, k)),
            pl.BlockSpec((tf, d), lambda i, k: (k, 0)),
            pl.BlockSpec((1, d), lambda i, k: (0, 0)),
        ],
        out_specs=pl.BlockSpec((tm, d), lambda i, k: (i, 0)),
        out_shape=jax.ShapeDtypeStruct((t, d), F32),
        scratch_shapes=[pltpu.VMEM((tm, d), BF16), pltpu.VMEM((tm, d), F32)],
        compiler_params=_params(("parallel", "arbitrary")),
        name="ffn",
    )(x, pre_g, wg, wu, wd, post_g)


def _in_proj_kernel(x_ref, g_ref, w_ref, qkv_ref, u_ref, hn_ref, *, q_scale):
    n = pl.program_id(1)

    @pl.when(n == 0)
    def _():
        xf = x_ref[...]
        hn_ref[...] = (xf * _rms_scale(xf) * g_ref[...]).astype(BF16)

    acc = jnp.dot(hn_ref[...], w_ref[...], preferred_element_type=F32)

    @pl.when(n == 0)
    def _():
        qkv_ref[...] = (acc * q_scale).astype(BF16)

    @pl.when(jnp.logical_and(n > 0, n < 3))
    def _():
        qkv_ref[...] = acc.astype(BF16)

    @pl.when(n == 3)
    def _():
        u_ref[...] = acc


def _in_proj(x, g, w, *, tm, q_scale):
    t, d = x.shape
    tn = w.shape[1] // 4
    return pl.pallas_call(
        functools.partial(_in_proj_kernel, q_scale=q_scale),
        grid=(t // tm, 4),
        in_specs=[
            pl.BlockSpec((tm, d), lambda i, n: (i, 0)),
            pl.BlockSpec((1, d), lambda i, n: (0, 0)),
            pl.BlockSpec((d, tn), lambda i, n: (0, n)),
        ],
        out_specs=[
            pl.BlockSpec((tm, tn), lambda i, n: (i, jnp.minimum(n, 2))),
            pl.BlockSpec((tm, tn), lambda i, n: (i, 0)),
        ],
        out_shape=[jax.ShapeDtypeStruct((t, 3 * tn), BF16),
                   jax.ShapeDtypeStruct((t, tn), F32)],
        scratch_shapes=[pltpu.VMEM((tm, d), BF16)],
        compiler_params=_params(("parallel", "arbitrary")),
        name="in_proj",
    )(x, g, w)


def _bias_band_kernel(rb_ref, o_ref, *, tq):
    h = pl.program_id(0)
    key = lax.broadcasted_iota(jnp.int32, (3 * tq, tq), 0)
    qry = lax.broadcasted_iota(jnp.int32, (3 * tq, tq), 1)
    n = jnp.maximum(2 * tq + qry - key, 0)
    max_exact = N_BUCKETS // 2
    nf = jnp.maximum(n, 1).astype(F32)
    large = max_exact + (jnp.log(nf / max_exact) / math.log(MAX_DISTANCE / max_exact)
                         * (N_BUCKETS - max_exact)).astype(jnp.int32)
    large = jnp.minimum(large, N_BUCKETS - 1)
    bucket = jnp.where(n < max_exact, n, large)
    val = jnp.zeros((3 * tq, tq), F32)
    for b in range(N_BUCKETS):
        val = jnp.where(bucket == b, rb_ref[b, h], val)
    o_ref[0] = val


def _bias_band(rel_bias, *, tq):
    assert tq >= MAX_DISTANCE
    n_heads = rel_bias.shape[1]
    return pl.pallas_call(
        functools.partial(_bias_band_kernel, tq=tq),
        grid=(n_heads,),
        in_specs=[pl.BlockSpec(memory_space=pltpu.SMEM)],
        out_specs=pl.BlockSpec((1, 3 * tq, tq), lambda h: (h, 0, 0)),
        out_shape=jax.ShapeDtypeStruct((n_heads, 3 * tq, tq), F32),
        compiler_params=_params(("arbitrary",)),
        name="bias_band",
    )(rel_bias)


N_STREAMS = 4


def _attn_kernel(lq1_ref, lk1_ref, lq2_ref, lk2_ref, q1_ref, q2_ref, k1_ref, k2_ref,
                 v_ref, band_ref, g_ref, o_ref, vt_ref, qm_ref, s_ref, m_ref, l_ref, acc_ref, *, tq):
    i = pl.program_id(2)
    n_chunks = v_ref.shape[1] // tq
    q_refs = (q1_ref, q2_ref)
    k_refs = (k1_ref, k2_ref)
    streams = [(hh, mi) for hh in range(2) for mi in range(2)]

    @pl.when(i == 0)
    def _():
        for c in range(n_chunks):
            vt_ref[c] = v_ref[0, c * tq:(c + 1) * tq, :].astype(F32).T.astype(BF16)

    lane = lax.broadcasted_iota(jnp.int32, (tq, 2 * ATTN_HEAD_DIM), 1)
    for si, (hh, mi) in enumerate(streams):
        qm_ref[si] = jnp.where((lane >= ATTN_HEAD_DIM) == (hh == 1),
                               q_refs[mi][0].astype(F32), 0.0).astype(BF16)
    m_ref[...] = jnp.full(m_ref.shape, NEG_INF, F32)
    l_ref[...] = jnp.zeros(l_ref.shape, F32)
    acc_ref[...] = jnp.zeros(acc_ref.shape, F32)

    def score_phase(j, buf, diagonal):
        rows = pl.ds(pl.multiple_of(j * tq, tq), tq)
        if diagonal:
            band_rows = pl.ds(2 * tq, tq)
            key = lax.broadcasted_iota(jnp.int32, (tq, tq), 0)
            qry = lax.broadcasted_iota(jnp.int32, (tq, tq), 1)
            mask = key <= qry
        else:
            band_rows = pl.ds(pl.multiple_of(jnp.where(j == i - 1, tq, 0), tq), tq)
        for si, (hh, mi) in enumerate(streams):
            s = lax.dot_general(k_refs[mi][0, rows, :], qm_ref[si], (((1,), (1,)), ((), ())),
                                preferred_element_type=F32) + band_ref[hh, band_rows, :]
            if diagonal:
                s = jnp.where(mask, s, NEG_INF)
            s_ref[buf, si] = s

    def value_phase(j, buf):
        for si, (hh, mi) in enumerate(streams):
            s = s_ref[buf, si]
            m_old = m_ref[si]
            m_new = jnp.maximum(m_old, jnp.max(s, axis=0, keepdims=True))
            alpha = jnp.exp(m_old - m_new)
            p = jnp.exp(s - m_new)
            l_ref[si] = alpha * l_ref[si] + jnp.sum(p, axis=0, keepdims=True)
            vt = vt_ref[j, hh * V_HEAD_DIM:(hh + 1) * V_HEAD_DIM, :]
            acc_ref[si] = alpha * acc_ref[si] + jnp.dot(vt, p.astype(BF16), preferred_element_type=F32)
            m_ref[si] = m_new

    @pl.when(i == 0)
    def _():
        score_phase(0, 0, True)

    @pl.when(i > 0)
    def _():
        score_phase(0, 0, False)

    def step(j, parity, diagonal):
        score_phase(j + 1, 1 - parity, diagonal)
        value_phase(j, parity)

    n = i - 1

    def pair_body(jj, carry):
        step(2 * jj, 0, False)
        step(2 * jj + 1, 1, False)
        return carry

    lax.fori_loop(0, jnp.maximum(n, 0) // 2, pair_body, 0)

    @pl.when(jnp.logical_and(i > 0, lax.rem(n, 2) == 1))
    def _():
        step(n - 1, 0, False)
        step(n, 1, True)

    @pl.when(jnp.logical_and(i > 0, lax.rem(n, 2) == 0))
    def _():
        step(n, 0, True)

    value_phase(i, lax.rem(i, 2))

    lam = (jnp.exp(jnp.sum(lq1_ref[...] * lk1_ref[...], axis=-1, keepdims=True))
           - jnp.exp(jnp.sum(lq2_ref[...] * lk2_ref[...], axis=-1, keepdims=True)) + LAMBDA_INIT)
    for hh in range(2):
        s1, s2 = 2 * hh, 2 * hh + 1
        a = acc_ref[s1] / l_ref[s1] - lam * (acc_ref[s2] / l_ref[s2])
        scale = lax.rsqrt(jnp.mean(a * a, axis=0, keepdims=True) + RMS_EPS)
        a = a * scale * g_ref[...] * (1.0 - LAMBDA_INIT)
        o_ref[0, :, hh * V_HEAD_DIM:(hh + 1) * V_HEAD_DIM] = a.T.astype(o_ref.dtype)


def _attention(qkv, band, lq1, lk1, lq2, lk2, subln_g, *, batch, seq, n_heads, tq):
    d_attn = n_heads * V_HEAD_DIM
    pair = 2 * ATTN_HEAD_DIM
    vpair = 2 * V_HEAD_DIM
    n_pairs = d_attn // 2 // pair
    qkv3 = qkv.reshape(batch, seq, 3 * d_attn)
    small = pl.BlockSpec((1, ATTN_HEAD_DIM), lambda b, p, i: (0, 0))
    return pl.pallas_call(
        functools.partial(_attn_kernel, tq=tq),
        grid=(batch, n_pairs, seq // tq),
        in_specs=[
            small, small, small, small,
            pl.BlockSpec((1, tq, pair), lambda b, p, i: (b, i, p)),
            pl.BlockSpec((1, tq, pair), lambda b, p, i: (b, i, n_pairs + p)),
            pl.BlockSpec((1, seq, pair), lambda b, p, i: (b, 0, 2 * n_pairs + p)),
            pl.BlockSpec((1, seq, pair), lambda b, p, i: (b, 0, 3 * n_pairs + p)),
            pl.BlockSpec((1, seq, vpair), lambda b, p, i: (b, 0, 2 * n_pairs + p)),
            pl.BlockSpec((2, 3 * tq, tq), lambda b, p, i: (p, 0, 0)),
            pl.BlockSpec((V_HEAD_DIM, 1), lambda b, p, i: (0, 0)),
        ],
        out_specs=pl.BlockSpec((1, tq, vpair), lambda b, p, i: (b, i, p)),
        out_shape=jax.ShapeDtypeStruct((batch, seq, d_attn), BF16),
        scratch_shapes=[pltpu.VMEM((seq // tq, vpair, tq), BF16),
                        pltpu.VMEM((N_STREAMS, tq, pair), BF16),
                        pltpu.VMEM((2, N_STREAMS, tq, tq), F32),
                        pltpu.VMEM((N_STREAMS, 1, tq), F32), pltpu.VMEM((N_STREAMS, 1, tq), F32),
                        pltpu.VMEM((N_STREAMS, V_HEAD_DIM, tq), F32)],
        compiler_params=_params(("parallel", "parallel", "arbitrary")),
        name="diff_attention",
    )(lq1, lk1, lq2, lk2, qkv3, qkv3, qkv3, qkv3, qkv3, band, subln_g)


def _ssm_params_kernel(a_re_ref, a_im_ref, log_dt_ref, b_re_ref, b_im_ref,
                       ab_re_ref, ab_im_ref, bb_re_ref, bb_im_ref):
    ar = a_re_ref[...]
    ai = a_im_ref[...]
    dt = jnp.exp(log_dt_ref[...])
    decay = jnp.exp(dt * ar)
    ab_re = decay * jnp.cos(dt * ai)
    ab_im = decay * jnp.sin(dt * ai)
    den = ar * ar + ai * ai
    nr = ab_re - 1.0
    ni = ab_im
    coef_re = (nr * ar + ni * ai) / den
    coef_im = (ni * ar - nr * ai) / den
    br = b_re_ref[...]
    bi = b_im_ref[...]
    ab_re_ref[...] = ab_re
    ab_im_ref[...] = ab_im
    bb_re_ref[...] = coef_re * br - coef_im * bi
    bb_im_ref[...] = coef_re * bi + coef_im * br


def _ssm_params(a_re, a_im, log_dt, b_re_t, b_im_t):
    g, _, n = a_re.shape
    p = b_re_t.shape[1]
    return pl.pallas_call(
        _ssm_params_kernel,
        out_shape=[jax.ShapeDtypeStruct((g, 1, n), F32), jax.ShapeDtypeStruct((g, 1, n), F32),
                   jax.ShapeDtypeStruct((g, p, n), F32), jax.ShapeDtypeStruct((g, p, n), F32)],
        name="ssm_params",
    )(a_re, a_im, log_dt, b_re_t, b_im_t)


SSM_COL_CHUNK = 512
SSM_BLOCKS = 4


def _ssm_input_drive(u_ref, lhs_ref, bmat_re_ref, bmat_im_ref, bu_re_ref, bu_im_ref, *, tb):
    d_ssm = lhs_ref.shape[1]
    for t in range(tb):
        lhs_ref[t * V7X_SUBLANES:(t + 1) * V7X_SUBLANES, :] = u_ref[0, :, t * d_ssm:(t + 1) * d_ssm]
    kin = d_ssm // SSM_BLOCKS
    kst = bu_re_ref.shape[1] // SSM_BLOCKS
    for j in range(SSM_BLOCKS):
        uj = lhs_ref[:, j * kin:(j + 1) * kin].astype(BF16)
        bu_re_ref[:, j * kst:(j + 1) * kst] = jnp.dot(uj, bmat_re_ref[j], preferred_element_type=F32)
        bu_im_ref[:, j * kst:(j + 1) * kst] = jnp.dot(uj, bmat_im_ref[j], preferred_element_type=F32)


def _ssm_recurrence(ab_re_ref, ab_im_ref, st_re_ref, st_im_ref, bu_re_ref, bu_im_ref, *, tb, store):
    n_cols = st_re_ref.shape[1]
    for c in range(n_cols // SSM_COL_CHUNK):
        cs = slice(c * SSM_COL_CHUNK, (c + 1) * SSM_COL_CHUNK)
        ar = jnp.broadcast_to(ab_re_ref[:, cs], (V7X_SUBLANES, SSM_COL_CHUNK))
        ai = jnp.broadcast_to(ab_im_ref[:, cs], (V7X_SUBLANES, SSM_COL_CHUNK))

        def step(t, carry, cs=cs, ar=ar, ai=ai):
            xr, xi = carry
            rows = pl.ds(pl.multiple_of(t * V7X_SUBLANES, V7X_SUBLANES), V7X_SUBLANES)
            nxr = ar * xr - ai * xi + bu_re_ref[rows, cs]
            nxi = ar * xi + ai * xr + bu_im_ref[rows, cs]
            if store:
                bu_re_ref[rows, cs] = nxr
                bu_im_ref[rows, cs] = nxi
            return nxr, nxi

        xr, xi = lax.fori_loop(0, tb, step, (st_re_ref[:, cs], st_im_ref[:, cs]), unroll=4)
        st_re_ref[:, cs] = xr
        st_im_ref[:, cs] = xi


def _ssm_ends_kernel(u_ref, bmat_re_ref, bmat_im_ref, ab_re_ref, ab_im_ref, end_re_ref, end_im_ref,
                     lhs_ref, bu_re_ref, bu_im_ref, st_re_ref, st_im_ref, *, tb):
    tblk = pl.program_id(1)

    @pl.when(tblk == 0)
    def _():
        st_re_ref[...] = jnp.zeros_like(st_re_ref)
        st_im_ref[...] = jnp.zeros_like(st_im_ref)

    _ssm_input_drive(u_ref, lhs_ref, bmat_re_ref, bmat_im_ref, bu_re_ref, bu_im_ref, tb=tb)
    _ssm_recurrence(ab_re_ref, ab_im_ref, st_re_ref, st_im_ref, bu_re_ref, bu_im_ref, tb=tb, store=False)

    @pl.when(tblk == pl.num_programs(1) - 1)
    def _():
        end_re_ref[0] = st_re_ref[...]
        end_im_ref[0] = st_im_ref[...]


def _ssm_main_kernel(u_ref, end_re_ref, end_im_ref, bmat_re_ref, bmat_im_ref, ab_re_ref, ab_im_ref,
                     cmat_re_ref, cmat_im_ref, d_ref, wglu_ref, bglu_ref, o_ref,
                     lhs_ref, bu_re_ref, bu_im_ref, st_re_ref, st_im_ref, *, tb, seg_len):
    tblk = pl.program_id(1)
    n_seg = st_re_ref.shape[0]

    @pl.when(tblk == 0)
    def _():
        pr = ab_re_ref[...]
        pi = ab_im_ref[...]
        for _ in range(int(math.log2(seg_len))):
            pr, pi = pr * pr - pi * pi, 2.0 * pr * pi
        er = jnp.zeros_like(pr)
        ei = jnp.zeros_like(pi)
        st_re_ref[0:1, :] = er
        st_im_ref[0:1, :] = ei
        for j in range(1, n_seg):
            lr = end_re_ref[0, j - 1:j, :]
            li = end_im_ref[0, j - 1:j, :]
            er, ei = lr + pr * er - pi * ei, li + pr * ei + pi * er
            st_re_ref[j:j + 1, :] = er
            st_im_ref[j:j + 1, :] = ei

    _ssm_input_drive(u_ref, lhs_ref, bmat_re_ref, bmat_im_ref, bu_re_ref, bu_im_ref, tb=tb)
    _ssm_recurrence(ab_re_ref, ab_im_ref, st_re_ref, st_im_ref, bu_re_ref, bu_im_ref, tb=tb, store=True)

    d_ssm = lhs_ref.shape[1]
    kin = d_ssm // SSM_BLOCKS
    kst = bu_re_ref.shape[1] // SSM_BLOCKS
    ys = []
    for j in range(SSM_BLOCKS):
        xr = bu_re_ref[:, j * kst:(j + 1) * kst].astype(BF16)
        xi = bu_im_ref[:, j * kst:(j + 1) * kst].astype(BF16)
        ys.append(jnp.dot(xr, cmat_re_ref[j], preferred_element_type=F32)
                  - jnp.dot(xi, cmat_im_ref[j], preferred_element_type=F32))
    y = jnp.concatenate(ys, axis=-1) + d_ref[...] * lhs_ref[...]
    z = jax.nn.gelu(y)
    gate = jnp.dot(z.astype(BF16), wglu_ref[...], preferred_element_type=F32) + bglu_ref[...]
    s = z * jax.nn.sigmoid(gate)
    for t in range(tb):
        o_ref[0, :, t * d_ssm:(t + 1) * d_ssm] = s[t * V7X_SUBLANES:(t + 1) * V7X_SUBLANES, :]


def _ssm(u_seg, bmat_re, bmat_im, ab_re, ab_im, cmat_re, cmat_im, d_skip, w_glu, b_glu, *, d_ssm, tb):
    batch, n_seg, flat = u_seg.shape
    seg_len = flat // d_ssm
    n_cols = ab_re.shape[1]
    rows = tb * n_seg
    grid = (batch, seg_len // tb)
    u_spec = pl.BlockSpec((1, n_seg, tb * d_ssm), lambda b, t: (b, 0, t))
    end_spec = pl.BlockSpec((1, n_seg, n_cols), lambda b, t: (b, 0, 0))

    def whole(a):
        return pl.BlockSpec(a.shape, lambda b, t: (0,) * a.ndim)

    scratch = [pltpu.VMEM((rows, d_ssm), F32), pltpu.VMEM((rows, n_cols), F32),
               pltpu.VMEM((rows, n_cols), F32), pltpu.VMEM((n_seg, n_cols), F32),
               pltpu.VMEM((n_seg, n_cols), F32)]
    end_re, end_im = pl.pallas_call(
        functools.partial(_ssm_ends_kernel, tb=tb),
        grid=grid,
        in_specs=[u_spec, whole(bmat_re), whole(bmat_im), whole(ab_re), whole(ab_im)],
        out_specs=[end_spec, end_spec],
        out_shape=[jax.ShapeDtypeStruct((batch, n_seg, n_cols), F32)] * 2,
        scratch_shapes=scratch,
        compiler_params=_params(("parallel", "arbitrary")),
        name="ssm_ends",
    )(u_seg, bmat_re, bmat_im, ab_re, ab_im)
    return pl.pallas_call(
        functools.partial(_ssm_main_kernel, tb=tb, seg_len=seg_len),
        grid=grid,
        in_specs=[u_spec, end_spec, end_spec, whole(bmat_re), whole(bmat_im), whole(ab_re), whole(ab_im),
                  whole(cmat_re), whole(cmat_im), whole(d_skip), whole(w_glu), whole(b_glu)],
        out_specs=u_spec,
        out_shape=jax.ShapeDtypeStruct(u_seg.shape, F32),
        scratch_shapes=scratch,
        compiler_params=_params(("parallel", "arbitrary")),
        name="ssm_main",
    )(u_seg, end_re, end_im, bmat_re, bmat_im, ab_re, ab_im, cmat_re, cmat_im, d_skip, w_glu, b_glu)


def _out_proj_kernel(a_ref, s_ref, wa_ref, ws_ref, x_ref, g_ref, o_ref):
    m = (jnp.dot(a_ref[...], wa_ref[...], preferred_element_type=F32)
         + jnp.dot(s_ref[...].astype(BF16), ws_ref[...], preferred_element_type=F32))
    o_ref[...] = x_ref[...] + m * _rms_scale(m) * g_ref[...]


def _out_proj(a, s, wa, ws, x, g, *, tm):
    t, d = x.shape
    da = a.shape[1]
    ds = s.shape[1]
    return pl.pallas_call(
        _out_proj_kernel,
        grid=(t // tm,),
        in_specs=[
            pl.BlockSpec((tm, da), lambda i: (i, 0)),
            pl.BlockSpec((tm, ds), lambda i: (i, 0)),
            pl.BlockSpec((da, d), lambda i: (0, 0)),
            pl.BlockSpec((ds, d), lambda i: (0, 0)),
            pl.BlockSpec((tm, d), lambda i: (i, 0)),
            pl.BlockSpec((1, d), lambda i: (0, 0)),
        ],
        out_specs=pl.BlockSpec((tm, d), lambda i: (i, 0)),
        out_shape=jax.ShapeDtypeStruct((t, d), F32),
        compiler_params=_params(("parallel",)),
        name="out_proj",
    )(a, s, wa, ws, x, g)


def _block_diag(w, n_blocks, spec):
    g = w.shape[0]
    gl = g // n_blocks
    eye = jnp.eye(gl, dtype=w.dtype)
    w4 = w.reshape(n_blocks, gl, w.shape[1], w.shape[2])
    out = jnp.einsum(spec, w4, eye)
    return out.reshape(n_blocks, out.shape[1] * out.shape[2], out.shape[3] * out.shape[4]).astype(BF16)


def kernel(x, ffn1_pre_g, ffn1_w_gate, ffn1_w_up, ffn1_w_down, ffn1_post_g, mix_pre_g, w_in, lambda_q1, lambda_k1, lambda_q2, lambda_k2, subln_g, rel_bias, ssm_a_re, ssm_a_im, ssm_b_re, ssm_b_im, ssm_c_re, ssm_c_im, ssm_d, ssm_log_dt, w_glu, b_glu, w_out, mix_post_g, ffn2_pre_g, ffn2_w_gate, ffn2_w_up, ffn2_w_down, ffn2_post_g):
    batch, seq, d_model = x.shape
    depth = ffn1_pre_g.shape[0]
    assert depth == 1, "LAMBDA_INIT is specialised to a single layer"
    n_heads = rel_bias.shape[1]
    d_attn = n_heads * V_HEAD_DIM
    n_groups, n_state = ssm_a_re.shape[1:]
    d_ssm = n_groups * SSM_GROUP
    n_seg = V7X_SUBLANES
    seg_len = seq // n_seg
    tokens = batch * seq
    tm, tf, tq, tb = 512, 512, 256, 32

    xt = x.reshape(tokens, d_model)
    row = lambda v: v.reshape(1, -1)
    l = 0

    x1 = _ffn(xt, row(ffn1_pre_g[l]), ffn1_w_gate[l].astype(BF16), ffn1_w_up[l].astype(BF16),
              ffn1_w_down[l].astype(BF16), row(ffn1_post_g[l]), tm=tm, tf=tf)

    qkv, u = _in_proj(x1, row(mix_pre_g[l]), w_in[l].astype(BF16), tm=tm,
                      q_scale=ATTN_HEAD_DIM ** -0.5)

    band = _bias_band(rel_bias, tq=tq)
    a = _attention(qkv, band, row(lambda_q1[l]), row(lambda_k1[l]), row(lambda_q2[l]),
                   row(lambda_k2[l]), subln_g[l].reshape(-1, 1), batch=batch, seq=seq, n_heads=n_heads, tq=tq)

    ab_re, ab_im, bb_re, bb_im = _ssm_params(
        ssm_a_re[l].reshape(n_groups, 1, n_state), ssm_a_im[l].reshape(n_groups, 1, n_state),
        ssm_log_dt[l].reshape(n_groups, 1, 1),
        jnp.swapaxes(ssm_b_re[l], 1, 2), jnp.swapaxes(ssm_b_im[l], 1, 2))
    bmat_re = _block_diag(bb_re, SSM_BLOCKS, "jgpn,gh->jgphn")
    bmat_im = _block_diag(bb_im, SSM_BLOCKS, "jgpn,gh->jgphn")
    cmat_re = _block_diag(ssm_c_re[l], SSM_BLOCKS, "jgpn,gh->jhngp")
    cmat_im = _block_diag(ssm_c_im[l], SSM_BLOCKS, "jgpn,gh->jhngp")
    u_seg = u.reshape(batch, n_seg, seg_len * d_ssm)
    s_seg = _ssm(u_seg, bmat_re, bmat_im, ab_re.reshape(1, -1), ab_im.reshape(1, -1), cmat_re, cmat_im,
                 row(ssm_d[l]), w_glu[l].astype(BF16), row(b_glu[l]), d_ssm=d_ssm, tb=tb)
    s = s_seg.reshape(tokens, d_ssm)

    wo = w_out[l].astype(BF16)
    x2 = _out_proj(a.reshape(tokens, d_attn), s, wo[:d_attn], wo[d_attn:], x1, row(mix_post_g[l]), tm=tm)

    x3 = _ffn(x2, row(ffn2_pre_g[l]), ffn2_w_gate[l].astype(BF16), ffn2_w_up[l].astype(BF16),
              ffn2_w_down[l].astype(BF16), row(ffn2_post_g[l]), tm=tm, tf=tf)
    return x3.reshape(batch, seq, d_model)
```

```python
import functools
import math

import jax
import jax.numpy as jnp
from jax import lax
from jax.experimental import pallas as pl
from jax.experimental.pallas import tpu as pltpu

V7X_LANES = 128
V7X_SUBLANES = 8
V7X_VMEM_LIMIT_BYTES = 56 * 1024 * 1024

RMS_EPS = 1e-6
NEG_INF = -1e30
N_BUCKETS = 32
MAX_DISTANCE = 128
ATTN_HEAD_DIM = 64
V_HEAD_DIM = 128
SSM_GROUP = 16
SSM_STATE = 64
LAMBDA_INIT = 0.8 - 0.6 * math.exp(-0.3 * 0)

F32 = jnp.float32
BF16 = jnp.bfloat16


def _params(semantics):
    return pltpu.CompilerParams(dimension_semantics=semantics,
                                vmem_limit_bytes=V7X_VMEM_LIMIT_BYTES)


def _rms_scale(v):
    return lax.rsqrt(jnp.mean(v * v, axis=-1, keepdims=True) + RMS_EPS)


def _ffn_kernel(x_ref, pre_g_ref, wg_ref, wu_ref, wd_ref, post_g_ref, *rest, n_cast):
    cast_in = rest[:n_cast]
    o_ref = rest[n_cast]
    cast_out = rest[n_cast + 1:2 * n_cast + 1]
    xn_ref, acc_ref = rest[2 * n_cast + 1:]
    k = pl.program_id(1)

    @pl.when(k == 0)
    def _():
        xf = x_ref[...]
        xn_ref[...] = (xf * _rms_scale(xf) * pre_g_ref[...]).astype(BF16)
        acc_ref[...] = jnp.zeros_like(acc_ref)
        for src, dst in zip(cast_in, cast_out):
            dst[...] = src[...].astype(BF16)

    xn = xn_ref[...]
    gate = jnp.dot(xn, wg_ref[...], preferred_element_type=F32)
    up = jnp.dot(xn, wu_ref[...], preferred_element_type=F32)
    hidden = (gate * jax.nn.sigmoid(gate) * up).astype(BF16)
    acc_ref[...] += jnp.dot(hidden, wd_ref[...], preferred_element_type=F32)

    @pl.when(k == pl.num_programs(1) - 1)
    def _():
        h = acc_ref[...]
        o_ref[...] = x_ref[...] + 0.5 * (h * _rms_scale(h) * post_g_ref[...])


def _ffn(x, pre_g, wg, wu, wd, post_g, *, tm, tf, cast=()):
    t, d = x.shape
    f = wg.shape[1]
    n_tiles = t // tm
    cast_specs = [pl.BlockSpec((w.shape[0] // n_tiles, w.shape[1]), lambda i, k: (i, 0)) for w in cast]
    outs = pl.pallas_call(
        functools.partial(_ffn_kernel, n_cast=len(cast)),
        grid=(n_tiles, f // tf),
        in_specs=[
            pl.BlockSpec((tm, d), lambda i, k: (i, 0)),
            pl.BlockSpec((1, d), lambda i, k: (0, 0)),
            pl.BlockSpec((d, tf), lambda i, k: (0, k)),
            pl.BlockSpec((d, tf), lambda i, k: (0, k)),
            pl.BlockSpec((tf, d), lambda i, k: (k, 0)),
            pl.BlockSpec((1, d), lambda i, k: (0, 0)),
        ] + cast_specs,
        out_specs=[pl.BlockSpec((tm, d), lambda i, k: (i, 0))] + cast_specs,
        out_shape=[jax.ShapeDtypeStruct((t, d), F32)] + [jax.ShapeDtypeStruct(w.shape, BF16) for w in cast],
        scratch_shapes=[pltpu.VMEM((tm, d), BF16), pltpu.VMEM((tm, d), F32)],
        compiler_params=_params(("parallel", "arbitrary")),
        name="ffn",
    )(x, pre_g, wg, wu, wd, post_g, *cast)
    return outs


def _in_proj_kernel(x_ref, g_ref, w_ref, qkv_ref, u_ref, hn_ref, *, q_scale):
    n = pl.program_id(1)

    @pl.when(n == 0)
    def _():
        xf = x_ref[...]
        hn_ref[...] = (xf * _rms_scale(xf) * g_ref[...]).astype(BF16)

    acc = jnp.dot(hn_ref[...], w_ref[...], preferred_element_type=F32)

    @pl.when(n == 0)
    def _():
        qkv_ref[...] = (acc * q_scale).astype(BF16)

    @pl.when(jnp.logical_and(n > 0, n < 3))
    def _():
        qkv_ref[...] = acc.astype(BF16)

    @pl.when(n == 3)
    def _():
        u_ref[...] = acc


def _in_proj(x, g, w, *, tm, q_scale):
    t, d = x.shape
    tn = w.shape[1] // 4
    return pl.pallas_call(
        functools.partial(_in_proj_kernel, q_scale=q_scale),
        grid=(t // tm, 4),
        in_specs=[
            pl.BlockSpec((tm, d), lambda i, n: (i, 0)),
            pl.BlockSpec((1, d), lambda i, n: (0, 0)),
            pl.BlockSpec((d, tn), lambda i, n: (0, n)),
        ],
        out_specs=[
            pl.BlockSpec((tm, tn), lambda i, n: (i, jnp.minimum(n, 2))),
            pl.BlockSpec((tm, tn), lambda i, n: (i, 0)),
        ],
        out_shape=[jax.ShapeDtypeStruct((t, 3 * tn), BF16),
                   jax.ShapeDtypeStruct((t, tn), F32)],
        scratch_shapes=[pltpu.VMEM((tm, d), BF16)],
        compiler_params=_params(("parallel", "arbitrary")),
        name="in_proj",
    )(x, g, w)


def _bias_band_kernel(rb_ref, o_ref, *, tq):
    h = pl.program_id(0)
    key = lax.broadcasted_iota(jnp.int32, (3 * tq, tq), 0)
    qry = lax.broadcasted_iota(jnp.int32, (3 * tq, tq), 1)
    n = jnp.maximum(2 * tq + qry - key, 0)
    max_exact = N_BUCKETS // 2
    nf = jnp.maximum(n, 1).astype(F32)
    large = max_exact + (jnp.log(nf / max_exact) / math.log(MAX_DISTANCE / max_exact)
                         * (N_BUCKETS - max_exact)).astype(jnp.int32)
    large = jnp.minimum(large, N_BUCKETS - 1)
    bucket = jnp.where(n < max_exact, n, large)
    val = jnp.zeros((3 * tq, tq), F32)
    for b in range(N_BUCKETS):
        val = jnp.where(bucket == b, rb_ref[b, h], val)
    o_ref[0] = val


def _bias_band(rel_bias, *, tq):
    assert tq >= MAX_DISTANCE
    n_heads = rel_bias.shape[1]
    return pl.pallas_call(
        functools.partial(_bias_band_kernel, tq=tq),
        grid=(n_heads,),
        in_specs=[pl.BlockSpec(memory_space=pltpu.SMEM)],
        out_specs=pl.BlockSpec((1, 3 * tq, tq), lambda h: (h, 0, 0)),
        out_shape=jax.ShapeDtypeStruct((n_heads, 3 * tq, tq), F32),
        compiler_params=_params(("arbitrary",)),
        name="bias_band",
    )(rel_bias)


N_STREAMS = 4


def _attn_kernel(lq1_ref, lk1_ref, lq2_ref, lk2_ref, q1_ref, q2_ref, k1_ref, k2_ref,
                 v_ref, band_ref, g_ref, *rest, tq, cast_steps):
    n_cast = len(cast_steps)
    cast_in = rest[:n_cast]
    o_ref = rest[n_cast]
    cast_out = rest[n_cast + 1:2 * n_cast + 1]
    vt_ref, qm_ref, s_ref, m_ref, l_ref, acc_ref = rest[2 * n_cast + 1:]
    i = pl.program_id(2)
    step_id = (pl.program_id(0) * pl.num_programs(1) + pl.program_id(1)) * pl.num_programs(2) + i
    for src, dst, steps in zip(cast_in, cast_out, cast_steps):
        @pl.when(step_id < steps)
        def _(src=src, dst=dst):
            dst[...] = src[...].astype(BF16)

    n_chunks = v_ref.shape[1] // tq
    q_refs = (q1_ref, q2_ref)
    k_refs = (k1_ref, k2_ref)
    streams = [(hh, mi) for hh in range(2) for mi in range(2)]

    @pl.when(i == 0)
    def _():
        for c in range(n_chunks):
            vt_ref[c] = v_ref[0, c * tq:(c + 1) * tq, :].astype(F32).T.astype(BF16)

    lane = lax.broadcasted_iota(jnp.int32, (tq, 2 * ATTN_HEAD_DIM), 1)
    for si, (hh, mi) in enumerate(streams):
        qm_ref[si] = jnp.where((lane >= ATTN_HEAD_DIM) == (hh == 1),
                               q_refs[mi][0].astype(F32), 0.0).astype(BF16)
    m_ref[...] = jnp.full(m_ref.shape, NEG_INF, F32)
    l_ref[...] = jnp.zeros(l_ref.shape, F32)
    acc_ref[...] = jnp.zeros(acc_ref.shape, F32)

    def score_phase(j, buf, diagonal):
        rows = pl.ds(pl.multiple_of(j * tq, tq), tq)
        if diagonal:
            band_rows = pl.ds(2 * tq, tq)
            key = lax.broadcasted_iota(jnp.int32, (tq, tq), 0)
            qry = lax.broadcasted_iota(jnp.int32, (tq, tq), 1)
            mask = key <= qry
        else:
            band_rows = pl.ds(pl.multiple_of(jnp.where(j == i - 1, tq, 0), tq), tq)
        for si, (hh, mi) in enumerate(streams):
            s = lax.dot_general(k_refs[mi][0, rows, :], qm_ref[si], (((1,), (1,)), ((), ())),
                                preferred_element_type=F32) + band_ref[hh, band_rows, :]
            if diagonal:
                s = jnp.where(mask, s, NEG_INF)
            s_ref[buf, si] = s

    def value_phase(j, buf):
        for si, (hh, mi) in enumerate(streams):
            s = s_ref[buf, si]
            m_old = m_ref[si]
            m_new = jnp.maximum(m_old, jnp.max(s, axis=0, keepdims=True))
            alpha = jnp.exp(m_old - m_new)
            p = jnp.exp(s - m_new)
            l_ref[si] = alpha * l_ref[si] + jnp.sum(p, axis=0, keepdims=True)
            vt = vt_ref[j, hh * V_HEAD_DIM:(hh + 1) * V_HEAD_DIM, :]
            acc_ref[si] = alpha * acc_ref[si] + jnp.dot(vt, p.astype(BF16), preferred_element_type=F32)
            m_ref[si] = m_new

    @pl.when(i == 0)
    def _():
        score_phase(0, 0, True)

    @pl.when(i > 0)
    def _():
        score_phase(0, 0, False)

    def step(j, parity, diagonal):
        score_phase(j + 1, 1 - parity, diagonal)
        value_phase(j, parity)

    n = i - 1

    def pair_body(jj, carry):
        step(2 * jj, 0, False)
        step(2 * jj + 1, 1, False)
        return carry

    lax.fori_loop(0, jnp.maximum(n, 0) // 2, pair_body, 0)

    @pl.when(jnp.logical_and(i > 0, lax.rem(n, 2) == 1))
    def _():
        step(n - 1, 0, False)
        step(n, 1, True)

    @pl.when(jnp.logical_and(i > 0, lax.rem(n, 2) == 0))
    def _():
        step(n, 0, True)

    value_phase(i, lax.rem(i, 2))

    lam = (jnp.exp(jnp.sum(lq1_ref[...] * lk1_ref[...], axis=-1, keepdims=True))
           - jnp.exp(jnp.sum(lq2_ref[...] * lk2_ref[...], axis=-1, keepdims=True)) + LAMBDA_INIT)
    for hh in range(2):
        s1, s2 = 2 * hh, 2 * hh + 1
        a = acc_ref[s1] / l_ref[s1] - lam * (acc_ref[s2] / l_ref[s2])
        scale = lax.rsqrt(jnp.mean(a * a, axis=0, keepdims=True) + RMS_EPS)
        a = a * scale * g_ref[...] * (1.0 - LAMBDA_INIT)
        o_ref[0, :, hh * V_HEAD_DIM:(hh + 1) * V_HEAD_DIM] = a.T.astype(o_ref.dtype)


def _attention(qkv, band, lq1, lk1, lq2, lk2, subln_g, *, batch, seq, n_heads, tq, cast=()):
    d_attn = n_heads * V_HEAD_DIM
    pair = 2 * ATTN_HEAD_DIM
    vpair = 2 * V_HEAD_DIM
    n_pairs = d_attn // 2 // pair
    nq = seq // tq
    n_steps = batch * n_pairs * nq
    cast_steps = []
    for w in cast:
        steps = n_steps
        while w.shape[0] % (16 * steps):
            steps //= 2
        cast_steps.append(steps)

    def cast_spec(w, steps):
        return pl.BlockSpec((w.shape[0] // steps, w.shape[1]),
                            lambda b, p, i: (jnp.minimum((b * n_pairs + p) * nq + i, steps - 1), 0))

    cast_specs = [cast_spec(w, steps) for w, steps in zip(cast, cast_steps)]
    qkv3 = qkv.reshape(batch, seq, 3 * d_attn)
    small = pl.BlockSpec((1, ATTN_HEAD_DIM), lambda b, p, i: (0, 0))
    return pl.pallas_call(
        functools.partial(_attn_kernel, tq=tq, cast_steps=tuple(cast_steps)),
        grid=(batch, n_pairs, nq),
        in_specs=[
            small, small, small, small,
            pl.BlockSpec((1, tq, pair), lambda b, p, i: (b, i, p)),
            pl.BlockSpec((1, tq, pair), lambda b, p, i: (b, i, n_pairs + p)),
            pl.BlockSpec((1, seq, pair), lambda b, p, i: (b, 0, 2 * n_pairs + p)),
            pl.BlockSpec((1, seq, pair), lambda b, p, i: (b, 0, 3 * n_pairs + p)),
            pl.BlockSpec((1, seq, vpair), lambda b, p, i: (b, 0, 2 * n_pairs + p)),
            pl.BlockSpec((2, 3 * tq, tq), lambda b, p, i: (p, 0, 0)),
            pl.BlockSpec((V_HEAD_DIM, 1), lambda b, p, i: (0, 0)),
        ] + cast_specs,
        out_specs=[pl.BlockSpec((1, tq, vpair), lambda b, p, i: (b, i, p))] + cast_specs,
        out_shape=[jax.ShapeDtypeStruct((batch, seq, d_attn), BF16)]
                  + [jax.ShapeDtypeStruct(w.shape, BF16) for w in cast],
        scratch_shapes=[pltpu.VMEM((seq // tq, vpair, tq), BF16),
                        pltpu.VMEM((N_STREAMS, tq, pair), BF16),
                        pltpu.VMEM((2, N_STREAMS, tq, tq), F32),
                        pltpu.VMEM((N_STREAMS, 1, tq), F32), pltpu.VMEM((N_STREAMS, 1, tq), F32),
                        pltpu.VMEM((N_STREAMS, V_HEAD_DIM, tq), F32)],
        compiler_params=_params(("arbitrary", "arbitrary", "arbitrary")),
        name="diff_attention",
    )(lq1, lk1, lq2, lk2, qkv3, qkv3, qkv3, qkv3, qkv3, band, subln_g, *cast)


def _ssm_params_kernel(a_re_ref, a_im_ref, log_dt_ref, b_re_ref, b_im_ref,
                       ab_re_ref, ab_im_ref, bb_re_ref, bb_im_ref):
    ar = a_re_ref[...]
    ai = a_im_ref[...]
    dt = jnp.exp(log_dt_ref[...])
    decay = jnp.exp(dt * ar)
    ab_re = decay * jnp.cos(dt * ai)
    ab_im = decay * jnp.sin(dt * ai)
    den = ar * ar + ai * ai
    nr = ab_re - 1.0
    ni = ab_im
    coef_re = (nr * ar + ni * ai) / den
    coef_im = (ni * ar - nr * ai) / den
    br = b_re_ref[...]
    bi = b_im_ref[...]
    ab_re_ref[...] = ab_re
    ab_im_ref[...] = ab_im
    bb_re_ref[...] = coef_re * br - coef_im * bi
    bb_im_ref[...] = coef_re * bi + coef_im * br


def _ssm_params(a_re, a_im, log_dt, b_re_t, b_im_t):
    g, _, n = a_re.shape
    p = b_re_t.shape[1]
    return pl.pallas_call(
        _ssm_params_kernel,
        out_shape=[jax.ShapeDtypeStruct((g, 1, n), F32), jax.ShapeDtypeStruct((g, 1, n), F32),
                   jax.ShapeDtypeStruct((g, p, n), F32), jax.ShapeDtypeStruct((g, p, n), F32)],
        name="ssm_params",
    )(a_re, a_im, log_dt, b_re_t, b_im_t)


SSM_COL_CHUNK = 512
SSM_BLOCKS = 4


def _ssm_input_drive(u_ref, lhs_ref, bmat_re_ref, bmat_im_ref, bu_re_ref, bu_im_ref, *, tb):
    n_seg = u_ref.shape[1]
    n_slabs = lhs_ref.shape[0]
    for seg in range(n_seg):
        for c in range(n_slabs):
            lhs_ref[c, pl.ds(seg, tb, stride=n_seg), :] = u_ref[0, seg, :, c * V7X_LANES:(c + 1) * V7X_LANES]
    per_block = n_slabs // SSM_BLOCKS
    kst = bu_re_ref.shape[1] // SSM_BLOCKS
    for j in range(SSM_BLOCKS):
        uj = jnp.concatenate([lhs_ref[c] for c in range(j * per_block, (j + 1) * per_block)],
                             axis=-1).astype(BF16)
        bu_re_ref[:, j * kst:(j + 1) * kst] = jnp.dot(uj, bmat_re_ref[j], preferred_element_type=F32)
        bu_im_ref[:, j * kst:(j + 1) * kst] = jnp.dot(uj, bmat_im_ref[j], preferred_element_type=F32)


def _ssm_recurrence(ab_re_ref, ab_im_ref, st_re_ref, st_im_ref, bu_re_ref, bu_im_ref, *, tb, store):
    n_cols = st_re_ref.shape[1]
    for c in range(n_cols // SSM_COL_CHUNK):
        cs = slice(c * SSM_COL_CHUNK, (c + 1) * SSM_COL_CHUNK)
        ar = jnp.broadcast_to(ab_re_ref[:, cs], (V7X_SUBLANES, SSM_COL_CHUNK))
        ai = jnp.broadcast_to(ab_im_ref[:, cs], (V7X_SUBLANES, SSM_COL_CHUNK))

        def step(t, carry, cs=cs, ar=ar, ai=ai):
            xr, xi = carry
            rows = pl.ds(pl.multiple_of(t * V7X_SUBLANES, V7X_SUBLANES), V7X_SUBLANES)
            nxr = ar * xr - ai * xi + bu_re_ref[rows, cs]
            nxi = ar * xi + ai * xr + bu_im_ref[rows, cs]
            if store:
                bu_re_ref[rows, cs] = nxr
                bu_im_ref[rows, cs] = nxi
            return nxr, nxi

        xr, xi = lax.fori_loop(0, tb, step, (st_re_ref[:, cs], st_im_ref[:, cs]), unroll=4)
        st_re_ref[:, cs] = xr
        st_im_ref[:, cs] = xi


def _ssm_ends_kernel(u_ref, bmat_re_ref, bmat_im_ref, ab_re_ref, ab_im_ref, end_re_ref, end_im_ref,
                     lhs_ref, bu_re_ref, bu_im_ref, st_re_ref, st_im_ref, *, tb):
    tblk = pl.program_id(1)

    @pl.when(tblk == 0)
    def _():
        st_re_ref[...] = jnp.zeros_like(st_re_ref)
        st_im_ref[...] = jnp.zeros_like(st_im_ref)

    _ssm_input_drive(u_ref, lhs_ref, bmat_re_ref, bmat_im_ref, bu_re_ref, bu_im_ref, tb=tb)
    _ssm_recurrence(ab_re_ref, ab_im_ref, st_re_ref, st_im_ref, bu_re_ref, bu_im_ref, tb=tb, store=False)

    @pl.when(tblk == pl.num_programs(1) - 1)
    def _():
        end_re_ref[0] = st_re_ref[...]
        end_im_ref[0] = st_im_ref[...]


def _ssm_main_kernel(u_ref, end_re_ref, end_im_ref, bmat_re_ref, bmat_im_ref, ab_re_ref, ab_im_ref,
                     cmat_re_ref, cmat_im_ref, d_ref, wglu_ref, bglu_ref, o_ref,
                     lhs_ref, bu_re_ref, bu_im_ref, st_re_ref, st_im_ref, *, tb, seg_len):
    tblk = pl.program_id(1)
    n_seg = st_re_ref.shape[0]

    @pl.when(tblk == 0)
    def _():
        pr = ab_re_ref[...]
        pi = ab_im_ref[...]
        for _ in range(int(math.log2(seg_len))):
            pr, pi = pr * pr - pi * pi, 2.0 * pr * pi
        er = jnp.zeros_like(pr)
        ei = jnp.zeros_like(pi)
        st_re_ref[0:1, :] = er
        st_im_ref[0:1, :] = ei
        for j in range(1, n_seg):
            lr = end_re_ref[0, j - 1:j, :]
            li = end_im_ref[0, j - 1:j, :]
            er, ei = lr + pr * er - pi * ei, li + pr * ei + pi * er
            st_re_ref[j:j + 1, :] = er
            st_im_ref[j:j + 1, :] = ei

    _ssm_input_drive(u_ref, lhs_ref, bmat_re_ref, bmat_im_ref, bu_re_ref, bu_im_ref, tb=tb)
    _ssm_recurrence(ab_re_ref, ab_im_ref, st_re_ref, st_im_ref, bu_re_ref, bu_im_ref, tb=tb, store=True)

    n_slabs = lhs_ref.shape[0]
    kst = bu_re_ref.shape[1] // SSM_BLOCKS
    ys = []
    for j in range(SSM_BLOCKS):
        xr = bu_re_ref[:, j * kst:(j + 1) * kst].astype(BF16)
        xi = bu_im_ref[:, j * kst:(j + 1) * kst].astype(BF16)
        ys.append(jnp.dot(xr, cmat_re_ref[j], preferred_element_type=F32)
                  - jnp.dot(xi, cmat_im_ref[j], preferred_element_type=F32))
    u_rows = jnp.concatenate([lhs_ref[c] for c in range(n_slabs)], axis=-1)
    y = jnp.concatenate(ys, axis=-1) + d_ref[...] * u_rows
    z = jax.nn.gelu(y)
    gate = jnp.dot(z.astype(BF16), wglu_ref[...], preferred_element_type=F32) + bglu_ref[...]
    s = z * jax.nn.sigmoid(gate)
    for c in range(n_slabs):
        lhs_ref[c] = s[:, c * V7X_LANES:(c + 1) * V7X_LANES]
    for seg in range(n_seg):
        for c in range(n_slabs):
            o_ref[0, seg, :, c * V7X_LANES:(c + 1) * V7X_LANES] = lhs_ref[c, pl.ds(seg, tb, stride=n_seg), :]


def _ssm(u_seg, bmat_re, bmat_im, ab_re, ab_im, cmat_re, cmat_im, d_skip, w_glu, b_glu, *, tb):
    batch, n_seg, seg_len, d_ssm = u_seg.shape
    n_cols = ab_re.shape[1]
    rows = tb * n_seg
    grid = (batch, seg_len // tb)
    u_spec = pl.BlockSpec((1, n_seg, tb, d_ssm), lambda b, t: (b, 0, t, 0))
    end_spec = pl.BlockSpec((1, n_seg, n_cols), lambda b, t: (b, 0, 0))

    def whole(a):
        return pl.BlockSpec(a.shape, lambda b, t: (0,) * a.ndim)

    scratch = [pltpu.VMEM((d_ssm // V7X_LANES, rows, V7X_LANES), F32), pltpu.VMEM((rows, n_cols), F32),
               pltpu.VMEM((rows, n_cols), F32), pltpu.VMEM((n_seg, n_cols), F32),
               pltpu.VMEM((n_seg, n_cols), F32)]
    end_re, end_im = pl.pallas_call(
        functools.partial(_ssm_ends_kernel, tb=tb),
        grid=grid,
        in_specs=[u_spec, whole(bmat_re), whole(bmat_im), whole(ab_re), whole(ab_im)],
        out_specs=[end_spec, end_spec],
        out_shape=[jax.ShapeDtypeStruct((batch, n_seg, n_cols), F32)] * 2,
        scratch_shapes=scratch,
        compiler_params=_params(("parallel", "arbitrary")),
        name="ssm_ends",
    )(u_seg, bmat_re, bmat_im, ab_re, ab_im)
    return pl.pallas_call(
        functools.partial(_ssm_main_kernel, tb=tb, seg_len=seg_len),
        grid=grid,
        in_specs=[u_spec, end_spec, end_spec, whole(bmat_re), whole(bmat_im), whole(ab_re), whole(ab_im),
                  whole(cmat_re), whole(cmat_im), whole(d_skip), whole(w_glu), whole(b_glu)],
        out_specs=u_spec,
        out_shape=jax.ShapeDtypeStruct(u_seg.shape, F32),
        scratch_shapes=scratch,
        compiler_params=_params(("parallel", "arbitrary")),
        name="ssm_main",
    )(u_seg, end_re, end_im, bmat_re, bmat_im, ab_re, ab_im, cmat_re, cmat_im, d_skip, w_glu, b_glu)


def _out_proj_kernel(a_ref, s_ref, wa_ref, ws_ref, x_ref, g_ref, o_ref):
    m = (jnp.dot(a_ref[...], wa_ref[...], preferred_element_type=F32)
         + jnp.dot(s_ref[...].astype(BF16), ws_ref[...], preferred_element_type=F32))
    o_ref[...] = x_ref[...] + m * _rms_scale(m) * g_ref[...]


def _out_proj(a, s, w, x, g, *, tm):
    t, d = x.shape
    da = a.shape[1]
    ds = s.shape[1]
    assert da == ds and w.shape[0] == da + ds
    return pl.pallas_call(
        _out_proj_kernel,
        grid=(t // tm,),
        in_specs=[
            pl.BlockSpec((tm, da), lambda i: (i, 0)),
            pl.BlockSpec((tm, ds), lambda i: (i, 0)),
            pl.BlockSpec((da, d), lambda i: (0, 0)),
            pl.BlockSpec((ds, d), lambda i: (1, 0)),
            pl.BlockSpec((tm, d), lambda i: (i, 0)),
            pl.BlockSpec((1, d), lambda i: (0, 0)),
        ],
        out_specs=pl.BlockSpec((tm, d), lambda i: (i, 0)),
        out_shape=jax.ShapeDtypeStruct((t, d), F32),
        compiler_params=_params(("parallel",)),
        name="out_proj",
    )(a, s, w, w, x, g)


def _block_diag(w, n_blocks, spec):
    g = w.shape[0]
    gl = g // n_blocks
    eye = jnp.eye(gl, dtype=w.dtype)
    w4 = w.reshape(n_blocks, gl, w.shape[1], w.shape[2])
    out = jnp.einsum(spec, w4, eye)
    return out.reshape(n_blocks, out.shape[1] * out.shape[2], out.shape[3] * out.shape[4]).astype(BF16)


def kernel(x, ffn1_pre_g, ffn1_w_gate, ffn1_w_up, ffn1_w_down, ffn1_post_g, mix_pre_g, w_in, lambda_q1, lambda_k1, lambda_q2, lambda_k2, subln_g, rel_bias, ssm_a_re, ssm_a_im, ssm_b_re, ssm_b_im, ssm_c_re, ssm_c_im, ssm_d, ssm_log_dt, w_glu, b_glu, w_out, mix_post_g, ffn2_pre_g, ffn2_w_gate, ffn2_w_up, ffn2_w_down, ffn2_post_g):
    batch, seq, d_model = x.shape
    depth = ffn1_pre_g.shape[0]
    assert depth == 1, "LAMBDA_INIT is specialised to a single layer"
    n_heads = rel_bias.shape[1]
    d_attn = n_heads * V_HEAD_DIM
    n_groups, n_state = ssm_a_re.shape[1:]
    d_ssm = n_groups * SSM_GROUP
    n_seg = V7X_SUBLANES
    seg_len = seq // n_seg
    tokens = batch * seq
    tm, tf, tq, tb = 512, 512, 256, 32

    xt = x.reshape(tokens, d_model)
    row = lambda v: v.reshape(1, -1)
    l = 0

    x1, w_in_bf, w_glu_bf = _ffn(
        xt, row(ffn1_pre_g[l]), ffn1_w_gate[l].astype(BF16), ffn1_w_up[l].astype(BF16),
        ffn1_w_down[l].astype(BF16), row(ffn1_post_g[l]), tm=tm, tf=tf, cast=(w_in[l], w_glu[l]))

    qkv, u = _in_proj(x1, row(mix_pre_g[l]), w_in_bf, tm=tm, q_scale=ATTN_HEAD_DIM ** -0.5)

    band = _bias_band(rel_bias, tq=tq)
    a, wg2, wu2, wd2, wo = _attention(
        qkv, band, row(lambda_q1[l]), row(lambda_k1[l]), row(lambda_q2[l]), row(lambda_k2[l]),
        subln_g[l].reshape(-1, 1), batch=batch, seq=seq, n_heads=n_heads, tq=tq,
        cast=(ffn2_w_gate[l], ffn2_w_up[l], ffn2_w_down[l], w_out[l]))

    ab_re, ab_im, bb_re, bb_im = _ssm_params(
        ssm_a_re[l].reshape(n_groups, 1, n_state), ssm_a_im[l].reshape(n_groups, 1, n_state),
        ssm_log_dt[l].reshape(n_groups, 1, 1),
        jnp.swapaxes(ssm_b_re[l], 1, 2), jnp.swapaxes(ssm_b_im[l], 1, 2))
    bmat_re = _block_diag(bb_re, SSM_BLOCKS, "jgpn,gh->jgphn")
    bmat_im = _block_diag(bb_im, SSM_BLOCKS, "jgpn,gh->jgphn")
    cmat_re = _block_diag(ssm_c_re[l], SSM_BLOCKS, "jgpn,gh->jhngp")
    cmat_im = _block_diag(ssm_c_im[l], SSM_BLOCKS, "jgpn,gh->jhngp")
    u_seg = u.reshape(batch, n_seg, seg_len, d_ssm)
    s_seg = _ssm(u_seg, bmat_re, bmat_im, ab_re.reshape(1, -1), ab_im.reshape(1, -1), cmat_re, cmat_im,
                 row(ssm_d[l]), w_glu_bf, row(b_glu[l]), tb=tb)
    s = s_seg.reshape(tokens, d_ssm)

    x2 = _out_proj(a.reshape(tokens, d_attn), s, wo, x1, row(mix_post_g[l]), tm=tm)

    (x3,) = _ffn(x2, row(ffn2_pre_g[l]), wg2, wu2, wd2, row(ffn2_post_g[l]), tm=tm, tf=tf)
    return x3.reshape(batch, seq, d_model)
```

```python
import functools
import math

import jax
import jax.numpy as jnp
from jax import lax
from jax.experimental import pallas as pl
from jax.experimental.pallas import tpu as pltpu

V7X_LANES = 128
V7X_SUBLANES = 8
V7X_VMEM_LIMIT_BYTES = 56 * 1024 * 1024

RMS_EPS = 1e-6
NEG_INF = -1e30
N_BUCKETS = 32
MAX_DISTANCE = 128
ATTN_HEAD_DIM = 64
V_HEAD_DIM = 128
SSM_GROUP = 16
SSM_STATE = 64
LAMBDA_INIT = 0.8 - 0.6 * math.exp(-0.3 * 0)

F32 = jnp.float32
BF16 = jnp.bfloat16


def _params(semantics):
    return pltpu.CompilerParams(dimension_semantics=semantics,
                                vmem_limit_bytes=V7X_VMEM_LIMIT_BYTES)


def _rms_scale(v):
    return lax.rsqrt(jnp.mean(v * v, axis=-1, keepdims=True) + RMS_EPS)


ROW_CHUNK = 16


def _for_row_chunks(n_rows, body):
    def step(r, carry):
        body(pl.ds(pl.multiple_of(r * ROW_CHUNK, ROW_CHUNK), ROW_CHUNK))
        return carry
    lax.fori_loop(0, n_rows // ROW_CHUNK, step, 0, unroll=2)


def _rmsnorm_rows_bf16(x_ref, g_ref, dst_ref):
    def body(rows):
        xf = x_ref[rows, :]
        dst_ref[rows, :] = (xf * _rms_scale(xf) * g_ref[...]).astype(BF16)
    _for_row_chunks(x_ref.shape[0], body)


def _ffn_kernel(x_ref, pre_g_ref, wg_ref, wu_ref, wd_ref, post_g_ref, *rest, n_cast):
    cast_in = rest[:n_cast]
    o_ref = rest[n_cast]
    cast_out = rest[n_cast + 1:2 * n_cast + 1]
    xn_ref, acc_ref = rest[2 * n_cast + 1:]
    k = pl.program_id(1)

    @pl.when(k == 0)
    def _():
        xf = x_ref[...]
        xn_ref[...] = (xf * _rms_scale(xf) * pre_g_ref[...]).astype(BF16)
        acc_ref[...] = jnp.zeros_like(acc_ref)
        for src, dst in zip(cast_in, cast_out):
            dst[...] = src[...].astype(BF16)

    xn = xn_ref[...]
    half = wg_ref.shape[1] // 2
    pre = []
    for c in range(2):
        cols = slice(c * half, (c + 1) * half)
        pre.append((jnp.dot(xn, wg_ref[:, cols], preferred_element_type=F32),
                    jnp.dot(xn, wu_ref[:, cols], preferred_element_type=F32)))
    out = acc_ref[...]
    for c, (gate, up) in enumerate(pre):
        hidden = (gate * jax.nn.sigmoid(gate) * up).astype(BF16)
        out = out + jnp.dot(hidden, wd_ref[c * half:(c + 1) * half, :], preferred_element_type=F32)
    acc_ref[...] = out

    @pl.when(k == pl.num_programs(1) - 1)
    def _():
        h = acc_ref[...]
        o_ref[...] = x_ref[...] + 0.5 * (h * _rms_scale(h) * post_g_ref[...])


def _ffn(x, pre_g, wg, wu, wd, post_g, *, tm, tf, cast=()):
    t, d = x.shape
    f = wg.shape[1]
    n_tiles = t // tm
    cast_specs = [pl.BlockSpec((w.shape[0] // n_tiles, w.shape[1]), lambda i, k: (i, 0)) for w in cast]
    outs = pl.pallas_call(
        functools.partial(_ffn_kernel, n_cast=len(cast)),
        grid=(n_tiles, f // tf),
        in_specs=[
            pl.BlockSpec((tm, d), lambda i, k: (i, 0)),
            pl.BlockSpec((1, d), lambda i, k: (0, 0)),
            pl.BlockSpec((d, tf), lambda i, k: (0, k)),
            pl.BlockSpec((d, tf), lambda i, k: (0, k)),
            pl.BlockSpec((tf, d), lambda i, k: (k, 0)),
            pl.BlockSpec((1, d), lambda i, k: (0, 0)),
        ] + cast_specs,
        out_specs=[pl.BlockSpec((tm, d), lambda i, k: (i, 0))] + cast_specs,
        out_shape=[jax.ShapeDtypeStruct((t, d), F32)] + [jax.ShapeDtypeStruct(w.shape, BF16) for w in cast],
        scratch_shapes=[pltpu.VMEM((tm, d), BF16), pltpu.VMEM((tm, d), F32)],
        compiler_params=_params(("parallel", "arbitrary")),
        name="ffn",
    )(x, pre_g, wg, wu, wd, post_g, *cast)
    return outs


def _in_proj_kernel(x_ref, g_ref, w_ref, qkv_ref, u_ref, hn_ref, *, q_scale):
    n = pl.program_id(1)

    @pl.when(n == 0)
    def _():
        _rmsnorm_rows_bf16(x_ref, g_ref, hn_ref)

    acc = jnp.dot(hn_ref[...], w_ref[...], preferred_element_type=F32)

    @pl.when(n == 0)
    def _():
        qkv_ref[...] = (acc * q_scale).astype(BF16)

    @pl.when(jnp.logical_and(n > 0, n < 3))
    def _():
        qkv_ref[...] = acc.astype(BF16)

    @pl.when(n == 3)
    def _():
        u_ref[...] = acc


def _in_proj(x, g, w, *, tm, q_scale):
    t, d = x.shape
    tn = w.shape[1] // 4
    return pl.pallas_call(
        functools.partial(_in_proj_kernel, q_scale=q_scale),
        grid=(t // tm, 4),
        in_specs=[
            pl.BlockSpec((tm, d), lambda i, n: (i, 0)),
            pl.BlockSpec((1, d), lambda i, n: (0, 0)),
            pl.BlockSpec((d, tn), lambda i, n: (0, n)),
        ],
        out_specs=[
            pl.BlockSpec((tm, tn), lambda i, n: (i, jnp.minimum(n, 2))),
            pl.BlockSpec((tm, tn), lambda i, n: (i, 0)),
        ],
        out_shape=[jax.ShapeDtypeStruct((t, 3 * tn), BF16),
                   jax.ShapeDtypeStruct((t, tn), F32)],
        scratch_shapes=[pltpu.VMEM((tm, d), BF16)],
        compiler_params=_params(("parallel", "arbitrary")),
        name="in_proj",
    )(x, g, w)


def _bias_band_kernel(rb_ref, o_ref, *, tq, scale):
    h = pl.program_id(0)
    key = lax.broadcasted_iota(jnp.int32, (3 * tq, tq), 0)
    qry = lax.broadcasted_iota(jnp.int32, (3 * tq, tq), 1)
    n = jnp.maximum(2 * tq + qry - key, 0)
    max_exact = N_BUCKETS // 2
    nf = jnp.maximum(n, 1).astype(F32)
    large = max_exact + (jnp.log(nf / max_exact) / math.log(MAX_DISTANCE / max_exact)
                         * (N_BUCKETS - max_exact)).astype(jnp.int32)
    large = jnp.minimum(large, N_BUCKETS - 1)
    bucket = jnp.where(n < max_exact, n, large)
    val = jnp.zeros((3 * tq, tq), F32)
    for b in range(N_BUCKETS):
        val = jnp.where(bucket == b, rb_ref[b, h], val)
    o_ref[0] = val * scale


def _bias_band(rel_bias, *, tq, scale):
    assert tq >= MAX_DISTANCE
    n_heads = rel_bias.shape[1]
    return pl.pallas_call(
        functools.partial(_bias_band_kernel, tq=tq, scale=scale),
        grid=(n_heads,),
        in_specs=[pl.BlockSpec(memory_space=pltpu.SMEM)],
        out_specs=pl.BlockSpec((1, 3 * tq, tq), lambda h: (h, 0, 0)),
        out_shape=jax.ShapeDtypeStruct((n_heads, 3 * tq, tq), F32),
        compiler_params=_params(("arbitrary",)),
        name="bias_band",
    )(rel_bias)


N_STREAMS = 4
VT_ROWS = V_HEAD_DIM + 16
LOG2_E = math.log2(math.e)


def _attn_kernel(lq1_ref, lk1_ref, lq2_ref, lk2_ref, q1_ref, q2_ref, k1_ref, k2_ref,
                 v_ref, band_ref, g_ref, *rest, tq, cast_steps):
    n_cast = len(cast_steps)
    cast_in = rest[:n_cast]
    o_ref = rest[n_cast]
    cast_out = rest[n_cast + 1:2 * n_cast + 1]
    vt_ref, qm_ref, s_ref, m_ref, acc_ref = rest[2 * n_cast + 1:]
    i = pl.program_id(2)
    step_id = (pl.program_id(0) * pl.num_programs(1) + pl.program_id(1)) * pl.num_programs(2) + i
    for src, dst, steps in zip(cast_in, cast_out, cast_steps):
        @pl.when(step_id < steps)
        def _(src=src, dst=dst):
            dst[...] = src[...].astype(BF16)

    n_chunks = v_ref.shape[1] // tq
    q_refs = (q1_ref, q2_ref)
    k_refs = (k1_ref, k2_ref)
    streams = [(hh, mi) for hh in range(2) for mi in range(2)]

    @pl.when(i == 0)
    def _():
        ones_pad = (lax.broadcasted_iota(jnp.int32, (VT_ROWS - V_HEAD_DIM, tq), 0) == 0).astype(BF16)
        for c in range(n_chunks):
            vt = v_ref[0, c * tq:(c + 1) * tq, :].astype(F32).T.astype(BF16)
            for hh in range(2):
                vt_ref[c, hh, :V_HEAD_DIM, :] = vt[hh * V_HEAD_DIM:(hh + 1) * V_HEAD_DIM]
                vt_ref[c, hh, V_HEAD_DIM:, :] = ones_pad

    feat = lax.broadcasted_iota(jnp.int32, (2 * ATTN_HEAD_DIM, tq), 0)
    for mi in range(2):
        qt = q_refs[mi][0].astype(F32).T
        for hh in range(2):
            qm_ref[2 * hh + mi] = jnp.where((feat >= ATTN_HEAD_DIM) == (hh == 1), qt, 0.0).astype(BF16)
    m_ref[...] = jnp.full(m_ref.shape, NEG_INF, F32)
    acc_ref[...] = jnp.zeros(acc_ref.shape, F32)

    def score_phase(j, buf, diagonal):
        rows = pl.ds(pl.multiple_of(j * tq, tq), tq)
        if diagonal:
            band_rows = pl.ds(2 * tq, tq)
            key = lax.broadcasted_iota(jnp.int32, (tq, tq), 0)
            qry = lax.broadcasted_iota(jnp.int32, (tq, tq), 1)
            mask = key <= qry
        else:
            band_rows = pl.ds(pl.multiple_of(jnp.where(j == i - 1, tq, 0), tq), tq)
        for si, (hh, mi) in enumerate(streams):
            s = jnp.dot(k_refs[mi][0, rows, :], qm_ref[si],
                        preferred_element_type=F32) + band_ref[hh, band_rows, :]
            if diagonal:
                s = jnp.where(mask, s, NEG_INF)
            s_ref[buf, si] = s

    def value_phase(j, buf):
        for si, (hh, mi) in enumerate(streams):
            s = s_ref[buf, si]
            m_old = m_ref[si]
            m_new = jnp.maximum(m_old, jnp.max(s, axis=0, keepdims=True))
            alpha = jnp.exp2(m_old - m_new)
            p = jnp.exp2(s - m_new)
            acc_ref[si] = alpha * acc_ref[si] + jnp.dot(vt_ref[j, hh], p.astype(BF16),
                                                        preferred_element_type=F32)
            m_ref[si] = m_new

    @pl.when(i == 0)
    def _():
        score_phase(0, 0, True)

    @pl.when(i > 0)
    def _():
        score_phase(0, 0, False)

    def step(j, parity, diagonal):
        score_phase(j + 1, 1 - parity, diagonal)
        value_phase(j, parity)

    n = i - 1

    def pair_body(jj, carry):
        step(2 * jj, 0, False)
        step(2 * jj + 1, 1, False)
        return carry

    lax.fori_loop(0, jnp.maximum(n, 0) // 2, pair_body, 0)

    @pl.when(jnp.logical_and(i > 0, lax.rem(n, 2) == 1))
    def _():
        step(n - 1, 0, False)
        step(n, 1, True)

    @pl.when(jnp.logical_and(i > 0, lax.rem(n, 2) == 0))
    def _():
        step(n, 0, True)

    value_phase(i, lax.rem(i, 2))

    lam = (jnp.exp(jnp.sum(lq1_ref[...] * lk1_ref[...], axis=-1, keepdims=True))
           - jnp.exp(jnp.sum(lq2_ref[...] * lk2_ref[...], axis=-1, keepdims=True)) + LAMBDA_INIT)
    for hh in range(2):
        s1, s2 = 2 * hh, 2 * hh + 1
        den = V_HEAD_DIM
        a = (acc_ref[s1, :den] / acc_ref[s1, den:den + 1]
             - lam * (acc_ref[s2, :den] / acc_ref[s2, den:den + 1]))
        scale = lax.rsqrt(jnp.mean(a * a, axis=0, keepdims=True) + RMS_EPS)
        a = a * scale * g_ref[...] * (1.0 - LAMBDA_INIT)
        o_ref[0, :, hh * V_HEAD_DIM:(hh + 1) * V_HEAD_DIM] = a.T.astype(o_ref.dtype)


def _attention(qkv, band, lq1, lk1, lq2, lk2, subln_g, *, batch, seq, n_heads, tq, cast=()):
    d_attn = n_heads * V_HEAD_DIM
    pair = 2 * ATTN_HEAD_DIM
    vpair = 2 * V_HEAD_DIM
    n_pairs = d_attn // 2 // pair
    nq = seq // tq
    n_steps = batch * n_pairs * nq
    cast_steps = []
    for w in cast:
        steps = n_steps
        while w.shape[0] % (16 * steps):
            steps //= 2
        cast_steps.append(steps)

    def cast_spec(w, steps):
        return pl.BlockSpec((w.shape[0] // steps, w.shape[1]),
                            lambda b, p, i: (jnp.minimum((b * n_pairs + p) * nq + i, steps - 1), 0))

    cast_specs = [cast_spec(w, steps) for w, steps in zip(cast, cast_steps)]
    qkv3 = qkv.reshape(batch, seq, 3 * d_attn)
    small = pl.BlockSpec((1, ATTN_HEAD_DIM), lambda b, p, i: (0, 0))
    return pl.pallas_call(
        functools.partial(_attn_kernel, tq=tq, cast_steps=tuple(cast_steps)),
        grid=(batch, n_pairs, nq),
        in_specs=[
            small, small, small, small,
            pl.BlockSpec((1, tq, pair), lambda b, p, i: (b, i, p)),
            pl.BlockSpec((1, tq, pair), lambda b, p, i: (b, i, n_pairs + p)),
            pl.BlockSpec((1, seq, pair), lambda b, p, i: (b, 0, 2 * n_pairs + p)),
            pl.BlockSpec((1, seq, pair), lambda b, p, i: (b, 0, 3 * n_pairs + p)),
            pl.BlockSpec((1, seq, vpair), lambda b, p, i: (b, 0, 2 * n_pairs + p)),
            pl.BlockSpec((2, 3 * tq, tq), lambda b, p, i: (p, 0, 0)),
            pl.BlockSpec((V_HEAD_DIM, 1), lambda b, p, i: (0, 0)),
        ] + cast_specs,
        out_specs=[pl.BlockSpec((1, tq, vpair), lambda b, p, i: (b, i, p))] + cast_specs,
        out_shape=[jax.ShapeDtypeStruct((batch, seq, d_attn), BF16)]
                  + [jax.ShapeDtypeStruct(w.shape, BF16) for w in cast],
        scratch_shapes=[pltpu.VMEM((seq // tq, 2, VT_ROWS, tq), BF16),
                        pltpu.VMEM((N_STREAMS, pair, tq), BF16),
                        pltpu.VMEM((2, N_STREAMS, tq, tq), F32),
                        pltpu.VMEM((N_STREAMS, 1, tq), F32),
                        pltpu.VMEM((N_STREAMS, VT_ROWS, tq), F32)],
        compiler_params=_params(("arbitrary", "arbitrary", "arbitrary")),
        name="diff_attention",
    )(lq1, lk1, lq2, lk2, qkv3, qkv3, qkv3, qkv3, qkv3, band, subln_g, *cast)


def _ssm_params_kernel(a_re_ref, a_im_ref, log_dt_ref, b_re_ref, b_im_ref,
                       ab_re_ref, ab_im_ref, bb_re_ref, bb_im_ref):
    ar = a_re_ref[...]
    ai = a_im_ref[...]
    dt = jnp.exp(log_dt_ref[...])
    decay = jnp.exp(dt * ar)
    ab_re = decay * jnp.cos(dt * ai)
    ab_im = decay * jnp.sin(dt * ai)
    den = ar * ar + ai * ai
    nr = ab_re - 1.0
    ni = ab_im
    coef_re = (nr * ar + ni * ai) / den
    coef_im = (ni * ar - nr * ai) / den
    br = b_re_ref[...]
    bi = b_im_ref[...]
    ab_re_ref[...] = ab_re
    ab_im_ref[...] = ab_im
    bb_re_ref[...] = coef_re * br - coef_im * bi
    bb_im_ref[...] = coef_re * bi + coef_im * br


def _ssm_params(a_re, a_im, log_dt, b_re_t, b_im_t):
    g, _, n = a_re.shape
    p = b_re_t.shape[1]
    return pl.pallas_call(
        _ssm_params_kernel,
        out_shape=[jax.ShapeDtypeStruct((g, 1, n), F32), jax.ShapeDtypeStruct((g, 1, n), F32),
                   jax.ShapeDtypeStruct((g, p, n), F32), jax.ShapeDtypeStruct((g, p, n), F32)],
        name="ssm_params",
    )(a_re, a_im, log_dt, b_re_t, b_im_t)


SSM_COL_CHUNK = 512
SSM_BLOCKS = 4


def _ssm_input_drive(u_ref, lhs_ref, bmat_re_ref, bmat_im_ref, bu_re_ref, bu_im_ref, *, tb):
    n_seg = u_ref.shape[1]
    n_slabs = lhs_ref.shape[0]
    for seg in range(n_seg):
        for c in range(n_slabs):
            lhs_ref[c, pl.ds(seg, tb, stride=n_seg), :] = u_ref[0, seg, :, c * V7X_LANES:(c + 1) * V7X_LANES]
    per_block = n_slabs // SSM_BLOCKS
    kst = bu_re_ref.shape[1] // SSM_BLOCKS
    for j in range(SSM_BLOCKS):
        uj = jnp.concatenate([lhs_ref[c] for c in range(j * per_block, (j + 1) * per_block)],
                             axis=-1).astype(BF16)
        bu_re_ref[:, j * kst:(j + 1) * kst] = jnp.dot(uj, bmat_re_ref[j], preferred_element_type=F32)
        bu_im_ref[:, j * kst:(j + 1) * kst] = jnp.dot(uj, bmat_im_ref[j], preferred_element_type=F32)


def _ssm_recurrence(ab_re_ref, ab_im_ref, st_re_ref, st_im_ref, bu_re_ref, bu_im_ref, *, tb, store):
    n_cols = st_re_ref.shape[1]
    for c in range(n_cols // SSM_COL_CHUNK):
        cs = slice(c * SSM_COL_CHUNK, (c + 1) * SSM_COL_CHUNK)
        ar = jnp.broadcast_to(ab_re_ref[:, cs], (V7X_SUBLANES, SSM_COL_CHUNK))
        ai = jnp.broadcast_to(ab_im_ref[:, cs], (V7X_SUBLANES, SSM_COL_CHUNK))

        def step(t, carry, cs=cs, ar=ar, ai=ai):
            xr, xi = carry
            rows = pl.ds(pl.multiple_of(t * V7X_SUBLANES, V7X_SUBLANES), V7X_SUBLANES)
            nxr = ar * xr - ai * xi + bu_re_ref[rows, cs]
            nxi = ar * xi + ai * xr + bu_im_ref[rows, cs]
            if store:
                bu_re_ref[rows, cs] = nxr
                bu_im_ref[rows, cs] = nxi
            return nxr, nxi

        xr, xi = lax.fori_loop(0, tb, step, (st_re_ref[:, cs], st_im_ref[:, cs]), unroll=4)
        st_re_ref[:, cs] = xr
        st_im_ref[:, cs] = xi


def _ssm_ends_kernel(u_ref, bmat_re_ref, bmat_im_ref, ab_re_ref, ab_im_ref, end_re_ref, end_im_ref,
                     lhs_ref, bu_re_ref, bu_im_ref, st_re_ref, st_im_ref, *, tb):
    tblk = pl.program_id(1)

    @pl.when(tblk == 0)
    def _():
        st_re_ref[...] = jnp.zeros_like(st_re_ref)
        st_im_ref[...] = jnp.zeros_like(st_im_ref)

    _ssm_input_drive(u_ref, lhs_ref, bmat_re_ref, bmat_im_ref, bu_re_ref, bu_im_ref, tb=tb)
    _ssm_recurrence(ab_re_ref, ab_im_ref, st_re_ref, st_im_ref, bu_re_ref, bu_im_ref, tb=tb, store=False)

    @pl.when(tblk == pl.num_programs(1) - 1)
    def _():
        end_re_ref[0] = st_re_ref[...]
        end_im_ref[0] = st_im_ref[...]


def _ssm_main_kernel(u_ref, end_re_ref, end_im_ref, bmat_re_ref, bmat_im_ref, ab_re_ref, ab_im_ref,
                     cmat_re_ref, cmat_im_ref, d_ref, wglu_ref, bglu_ref, o_ref,
                     lhs_ref, bu_re_ref, bu_im_ref, st_re_ref, st_im_ref, *, tb, seg_len):
    tblk = pl.program_id(1)
    n_seg = st_re_ref.shape[0]

    @pl.when(tblk == 0)
    def _():
        pr = ab_re_ref[...]
        pi = ab_im_ref[...]
        for _ in range(int(math.log2(seg_len))):
            pr, pi = pr * pr - pi * pi, 2.0 * pr * pi
        er = jnp.zeros_like(pr)
        ei = jnp.zeros_like(pi)
        st_re_ref[0:1, :] = er
        st_im_ref[0:1, :] = ei
        for j in range(1, n_seg):
            lr = end_re_ref[0, j - 1:j, :]
            li = end_im_ref[0, j - 1:j, :]
            er, ei = lr + pr * er - pi * ei, li + pr * ei + pi * er
            st_re_ref[j:j + 1, :] = er
            st_im_ref[j:j + 1, :] = ei

    _ssm_input_drive(u_ref, lhs_ref, bmat_re_ref, bmat_im_ref, bu_re_ref, bu_im_ref, tb=tb)
    _ssm_recurrence(ab_re_ref, ab_im_ref, st_re_ref, st_im_ref, bu_re_ref, bu_im_ref, tb=tb, store=True)

    n_slabs = lhs_ref.shape[0]
    kst = bu_re_ref.shape[1] // SSM_BLOCKS
    ys = []
    for j in range(SSM_BLOCKS):
        xr = bu_re_ref[:, j * kst:(j + 1) * kst].astype(BF16)
        xi = bu_im_ref[:, j * kst:(j + 1) * kst].astype(BF16)
        ys.append(jnp.dot(xr, cmat_re_ref[j], preferred_element_type=F32)
                  - jnp.dot(xi, cmat_im_ref[j], preferred_element_type=F32))
    u_rows = jnp.concatenate([lhs_ref[c] for c in range(n_slabs)], axis=-1)
    y = jnp.concatenate(ys, axis=-1) + d_ref[...] * u_rows
    z = jax.nn.gelu(y)
    gate = jnp.dot(z.astype(BF16), wglu_ref[...], preferred_element_type=F32) + bglu_ref[...]
    s = z * jax.nn.sigmoid(gate)
    for c in range(n_slabs):
        lhs_ref[c] = s[:, c * V7X_LANES:(c + 1) * V7X_LANES]
    for seg in range(n_seg):
        for c in range(n_slabs):
            o_ref[0, seg, :, c * V7X_LANES:(c + 1) * V7X_LANES] = lhs_ref[c, pl.ds(seg, tb, stride=n_seg), :]


def _ssm(u_seg, bmat_re, bmat_im, ab_re, ab_im, cmat_re, cmat_im, d_skip, w_glu, b_glu, *, tb):
    batch, n_seg, seg_len, d_ssm = u_seg.shape
    n_cols = ab_re.shape[1]
    rows = tb * n_seg
    grid = (batch, seg_len // tb)
    u_spec = pl.BlockSpec((1, n_seg, tb, d_ssm), lambda b, t: (b, 0, t, 0))
    end_spec = pl.BlockSpec((1, n_seg, n_cols), lambda b, t: (b, 0, 0))

    def whole(a):
        return pl.BlockSpec(a.shape, lambda b, t: (0,) * a.ndim)

    scratch = [pltpu.VMEM((d_ssm // V7X_LANES, rows, V7X_LANES), F32), pltpu.VMEM((rows, n_cols), F32),
               pltpu.VMEM((rows, n_cols), F32), pltpu.VMEM((n_seg, n_cols), F32),
               pltpu.VMEM((n_seg, n_cols), F32)]
    end_re, end_im = pl.pallas_call(
        functools.partial(_ssm_ends_kernel, tb=tb),
        grid=grid,
        in_specs=[u_spec, whole(bmat_re), whole(bmat_im), whole(ab_re), whole(ab_im)],
        out_specs=[end_spec, end_spec],
        out_shape=[jax.ShapeDtypeStruct((batch, n_seg, n_cols), F32)] * 2,
        scratch_shapes=scratch,
        compiler_params=_params(("parallel", "arbitrary")),
        name="ssm_ends",
    )(u_seg, bmat_re, bmat_im, ab_re, ab_im)
    return pl.pallas_call(
        functools.partial(_ssm_main_kernel, tb=tb, seg_len=seg_len),
        grid=grid,
        in_specs=[u_spec, end_spec, end_spec, whole(bmat_re), whole(bmat_im), whole(ab_re), whole(ab_im),
                  whole(cmat_re), whole(cmat_im), whole(d_skip), whole(w_glu), whole(b_glu)],
        out_specs=u_spec,
        out_shape=jax.ShapeDtypeStruct(u_seg.shape, F32),
        scratch_shapes=scratch,
        compiler_params=_params(("parallel", "arbitrary")),
        name="ssm_main",
    )(u_seg, end_re, end_im, bmat_re, bmat_im, ab_re, ab_im, cmat_re, cmat_im, d_skip, w_glu, b_glu)


def _out_proj_kernel(a_ref, s_ref, wa_ref, ws_ref, x_ref, g_ref, o_ref):
    m = (jnp.dot(a_ref[...], wa_ref[...], preferred_element_type=F32)
         + jnp.dot(s_ref[...].astype(BF16), ws_ref[...], preferred_element_type=F32))
    o_ref[...] = x_ref[...] + m * _rms_scale(m) * g_ref[...]


def _out_proj(a, s, w, x, g, *, tm):
    t, d = x.shape
    da = a.shape[1]
    ds = s.shape[1]
    assert da == ds and w.shape[0] == da + ds
    return pl.pallas_call(
        _out_proj_kernel,
        grid=(t // tm,),
        in_specs=[
            pl.BlockSpec((tm, da), lambda i: (i, 0)),
            pl.BlockSpec((tm, ds), lambda i: (i, 0)),
            pl.BlockSpec((da, d), lambda i: (0, 0)),
            pl.BlockSpec((ds, d), lambda i: (1, 0)),
            pl.BlockSpec((tm, d), lambda i: (i, 0)),
            pl.BlockSpec((1, d), lambda i: (0, 0)),
        ],
        out_specs=pl.BlockSpec((tm, d), lambda i: (i, 0)),
        out_shape=jax.ShapeDtypeStruct((t, d), F32),
        compiler_params=_params(("parallel",)),
        name="out_proj",
    )(a, s, w, w, x, g)


def _block_diag(w, n_blocks, spec):
    g = w.shape[0]
    gl = g // n_blocks
    eye = jnp.eye(gl, dtype=w.dtype)
    w4 = w.reshape(n_blocks, gl, w.shape[1], w.shape[2])
    out = jnp.einsum(spec, w4, eye)
    return out.reshape(n_blocks, out.shape[1] * out.shape[2], out.shape[3] * out.shape[4]).astype(BF16)


def kernel(x, ffn1_pre_g, ffn1_w_gate, ffn1_w_up, ffn1_w_down, ffn1_post_g, mix_pre_g, w_in, lambda_q1, lambda_k1, lambda_q2, lambda_k2, subln_g, rel_bias, ssm_a_re, ssm_a_im, ssm_b_re, ssm_b_im, ssm_c_re, ssm_c_im, ssm_d, ssm_log_dt, w_glu, b_glu, w_out, mix_post_g, ffn2_pre_g, ffn2_w_gate, ffn2_w_up, ffn2_w_down, ffn2_post_g):
    batch, seq, d_model = x.shape
    depth = ffn1_pre_g.shape[0]
    assert depth == 1, "LAMBDA_INIT is specialised to a single layer"
    n_heads = rel_bias.shape[1]
    d_attn = n_heads * V_HEAD_DIM
    n_groups, n_state = ssm_a_re.shape[1:]
    d_ssm = n_groups * SSM_GROUP
    n_seg = V7X_SUBLANES
    seg_len = seq // n_seg
    tokens = batch * seq
    tm, tf, tq, tb = 512, 512, 256, 32

    xt = x.reshape(tokens, d_model)
    row = lambda v: v.reshape(1, -1)
    l = 0

    x1, w_in_bf, w_glu_bf = _ffn(
        xt, row(ffn1_pre_g[l]), ffn1_w_gate[l].astype(BF16), ffn1_w_up[l].astype(BF16),
        ffn1_w_down[l].astype(BF16), row(ffn1_post_g[l]), tm=tm, tf=tf, cast=(w_in[l], w_glu[l]))

    qkv, u = _in_proj(x1, row(mix_pre_g[l]), w_in_bf, tm=tm, q_scale=ATTN_HEAD_DIM ** -0.5 * LOG2_E)

    band = _bias_band(rel_bias, tq=tq, scale=LOG2_E)
    a, wg2, wu2, wd2, wo = _attention(
        qkv, band, row(lambda_q1[l]), row(lambda_k1[l]), row(lambda_q2[l]), row(lambda_k2[l]),
        subln_g[l].reshape(-1, 1), batch=batch, seq=seq, n_heads=n_heads, tq=tq,
        cast=(ffn2_w_gate[l], ffn2_w_up[l], ffn2_w_down[l], w_out[l]))

    ab_re, ab_im, bb_re, bb_im = _ssm_params(
        ssm_a_re[l].reshape(n_groups, 1, n_state), ssm_a_im[l].reshape(n_groups, 1, n_state),
        ssm_log_dt[l].reshape(n_groups, 1, 1),
        jnp.swapaxes(ssm_b_re[l], 1, 2), jnp.swapaxes(ssm_b_im[l], 1, 2))
    bmat_re = _block_diag(bb_re, SSM_BLOCKS, "jgpn,gh->jgphn")
    bmat_im = _block_diag(bb_im, SSM_BLOCKS, "jgpn,gh->jgphn")
    cmat_re = _block_diag(ssm_c_re[l], SSM_BLOCKS, "jgpn,gh->jhngp")
    cmat_im = _block_diag(ssm_c_im[l], SSM_BLOCKS, "jgpn,gh->jhngp")
    u_seg = u.reshape(batch, n_seg, seg_len, d_ssm)
    s_seg = _ssm(u_seg, bmat_re, bmat_im, ab_re.reshape(1, -1), ab_im.reshape(1, -1), cmat_re, cmat_im,
                 row(ssm_d[l]), w_glu_bf, row(b_glu[l]), tb=tb)
    s = s_seg.reshape(tokens, d_ssm)

    x2 = _out_proj(a.reshape(tokens, d_attn), s, wo, x1, row(mix_post_g[l]), tm=tm)

    (x3,) = _ffn(x2, row(ffn2_pre_g[l]), wg2, wu2, wd2, row(ffn2_post_g[l]), tm=tm, tf=tf)
    return x3.reshape(batch, seq, d_model)
```

```python
import functools
import math

import jax
import jax.numpy as jnp
from jax import lax
from jax.experimental import pallas as pl
from jax.experimental.pallas import tpu as pltpu

V7X_LANES = 128
V7X_SUBLANES = 8
V7X_VMEM_LIMIT_BYTES = 56 * 1024 * 1024

RMS_EPS = 1e-6
NEG_INF = -1e30
N_BUCKETS = 32
MAX_DISTANCE = 128
ATTN_HEAD_DIM = 64
V_HEAD_DIM = 128
SSM_GROUP = 16
SSM_STATE = 64
LAMBDA_INIT = 0.8 - 0.6 * math.exp(-0.3 * 0)

F32 = jnp.float32
BF16 = jnp.bfloat16


def _params(semantics):
    return pltpu.CompilerParams(dimension_semantics=semantics,
                                vmem_limit_bytes=V7X_VMEM_LIMIT_BYTES)


def _rms_scale(v):
    return lax.rsqrt(jnp.mean(v * v, axis=-1, keepdims=True) + RMS_EPS)


def _ffn_kernel(x_ref, pre_g_ref, wg_ref, wu_ref, wd_ref, post_g_ref, *rest, n_cast):
    cast_in = rest[:n_cast]
    o_ref = rest[n_cast]
    cast_out = rest[n_cast + 1:2 * n_cast + 1]
    xn_ref, acc_ref = rest[2 * n_cast + 1:]
    k = pl.program_id(1)

    @pl.when(k == 0)
    def _():
        xf = x_ref[...]
        xn_ref[...] = (xf * _rms_scale(xf) * pre_g_ref[...]).astype(BF16)
        acc_ref[...] = jnp.zeros_like(acc_ref)
        for src, dst in zip(cast_in, cast_out):
            dst[...] = src[...].astype(BF16)

    xn = xn_ref[...]
    half = wg_ref.shape[1] // 2
    pre = []
    for c in range(2):
        cols = slice(c * half, (c + 1) * half)
        pre.append((jnp.dot(xn, wg_ref[:, cols], preferred_element_type=F32),
                    jnp.dot(xn, wu_ref[:, cols], preferred_element_type=F32)))
    out = acc_ref[...]
    for c, (gate, up) in enumerate(pre):
        hidden = (gate * jax.nn.sigmoid(gate) * up).astype(BF16)
        out = out + jnp.dot(hidden, wd_ref[c * half:(c + 1) * half, :], preferred_element_type=F32)
    acc_ref[...] = out

    @pl.when(k == pl.num_programs(1) - 1)
    def _():
        h = acc_ref[...]
        o_ref[...] = x_ref[...] + 0.5 * (h * _rms_scale(h) * post_g_ref[...])


def _ffn(x, pre_g, wg, wu, wd, post_g, *, tm, tf, cast=()):
    t, d = x.shape
    f = wg.shape[1]
    n_tiles = t // tm
    cast_specs = [pl.BlockSpec((w.shape[0] // n_tiles, w.shape[1]), lambda i, k: (i, 0)) for w in cast]
    outs = pl.pallas_call(
        functools.partial(_ffn_kernel, n_cast=len(cast)),
        grid=(n_tiles, f // tf),
        in_specs=[
            pl.BlockSpec((tm, d), lambda i, k: (i, 0)),
            pl.BlockSpec((1, d), lambda i, k: (0, 0)),
            pl.BlockSpec((d, tf), lambda i, k: (0, k)),
            pl.BlockSpec((d, tf), lambda i, k: (0, k)),
            pl.BlockSpec((tf, d), lambda i, k: (k, 0)),
            pl.BlockSpec((1, d), lambda i, k: (0, 0)),
        ] + cast_specs,
        out_specs=[pl.BlockSpec((tm, d), lambda i, k: (i, 0))] + cast_specs,
        out_shape=[jax.ShapeDtypeStruct((t, d), F32)] + [jax.ShapeDtypeStruct(w.shape, BF16) for w in cast],
        scratch_shapes=[pltpu.VMEM((tm, d), BF16), pltpu.VMEM((tm, d), F32)],
        compiler_params=_params(("parallel", "arbitrary")),
        name="ffn",
    )(x, pre_g, wg, wu, wd, post_g, *cast)
    return outs


def _in_proj_kernel(x_ref, g_ref, w_ref, qkv_ref, u_ref, *, q_scale):
    xf = x_ref[...]
    hn = (xf * _rms_scale(xf) * g_ref[...]).astype(BF16)
    tn = u_ref.shape[1]
    for n in range(4):
        acc = jnp.dot(hn, w_ref[:, n * tn:(n + 1) * tn], preferred_element_type=F32)
        if n == 0:
            qkv_ref[:, :tn] = (acc * q_scale).astype(BF16)
        elif n < 3:
            qkv_ref[:, n * tn:(n + 1) * tn] = acc.astype(BF16)
        else:
            u_ref[...] = acc


def _in_proj(x, g, w, *, tm, q_scale):
    t, d = x.shape
    tn = w.shape[1] // 4
    return pl.pallas_call(
        functools.partial(_in_proj_kernel, q_scale=q_scale),
        grid=(t // tm,),
        in_specs=[
            pl.BlockSpec((tm, d), lambda i: (i, 0)),
            pl.BlockSpec((1, d), lambda i: (0, 0)),
            pl.BlockSpec(w.shape, lambda i: (0, 0), pipeline_mode=pl.Buffered(1)),
        ],
        out_specs=[
            pl.BlockSpec((tm, 3 * tn), lambda i: (i, 0)),
            pl.BlockSpec((tm, tn), lambda i: (i, 0)),
        ],
        out_shape=[jax.ShapeDtypeStruct((t, 3 * tn), BF16),
                   jax.ShapeDtypeStruct((t, tn), F32)],
        compiler_params=_params(("parallel",)),
        name="in_proj",
    )(x, g, w)


def _bias_band_kernel(rb_ref, o_ref, *, tq, scale):
    h = pl.program_id(0)
    key = lax.broadcasted_iota(jnp.int32, (3 * tq, tq), 0)
    qry = lax.broadcasted_iota(jnp.int32, (3 * tq, tq), 1)
    n = jnp.maximum(2 * tq + qry - key, 0)
    max_exact = N_BUCKETS // 2
    nf = jnp.maximum(n, 1).astype(F32)
    large = max_exact + (jnp.log(nf / max_exact) / math.log(MAX_DISTANCE / max_exact)
                         * (N_BUCKETS - max_exact)).astype(jnp.int32)
    large = jnp.minimum(large, N_BUCKETS - 1)
    bucket = jnp.where(n < max_exact, n, large)
    val = jnp.zeros((3 * tq, tq), F32)
    for b in range(N_BUCKETS):
        val = jnp.where(bucket == b, rb_ref[b, h], val)
    o_ref[0] = val * scale


def _bias_band(rel_bias, *, tq, scale):
    assert tq >= MAX_DISTANCE
    n_heads = rel_bias.shape[1]
    return pl.pallas_call(
        functools.partial(_bias_band_kernel, tq=tq, scale=scale),
        grid=(n_heads,),
        in_specs=[pl.BlockSpec(memory_space=pltpu.SMEM)],
        out_specs=pl.BlockSpec((1, 3 * tq, tq), lambda h: (h, 0, 0)),
        out_shape=jax.ShapeDtypeStruct((n_heads, 3 * tq, tq), F32),
        compiler_params=_params(("arbitrary",)),
        name="bias_band",
    )(rel_bias)


N_STREAMS = 4
VT_ROWS = V_HEAD_DIM + 16
LOG2_E = math.log2(math.e)


def _attn_kernel(lq1_ref, lk1_ref, lq2_ref, lk2_ref, q1_ref, q2_ref, k1_ref, k2_ref,
                 v_ref, band_ref, g_ref, *rest, tq, cast_steps):
    n_cast = len(cast_steps)
    cast_in = rest[:n_cast]
    o_ref = rest[n_cast]
    cast_out = rest[n_cast + 1:2 * n_cast + 1]
    vt_ref, qm_ref, s_ref, m_ref, acc_ref = rest[2 * n_cast + 1:]
    i = pl.program_id(2)
    step_id = (pl.program_id(0) * pl.num_programs(1) + pl.program_id(1)) * pl.num_programs(2) + i
    for src, dst, steps in zip(cast_in, cast_out, cast_steps):
        @pl.when(step_id < steps)
        def _(src=src, dst=dst):
            dst[...] = src[...].astype(BF16)

    n_chunks = v_ref.shape[1] // tq
    q_refs = (q1_ref, q2_ref)
    k_refs = (k1_ref, k2_ref)
    streams = [(hh, mi) for hh in range(2) for mi in range(2)]

    @pl.when(i == 0)
    def _():
        ones_pad = (lax.broadcasted_iota(jnp.int32, (VT_ROWS - V_HEAD_DIM, tq), 0) == 0).astype(BF16)
        for c in range(n_chunks):
            vt = v_ref[0, c * tq:(c + 1) * tq, :].astype(F32).T.astype(BF16)
            for hh in range(2):
                vt_ref[c, hh, :V_HEAD_DIM, :] = vt[hh * V_HEAD_DIM:(hh + 1) * V_HEAD_DIM]
                vt_ref[c, hh, V_HEAD_DIM:, :] = ones_pad

    feat = lax.broadcasted_iota(jnp.int32, (2 * ATTN_HEAD_DIM, tq), 0)
    for mi in range(2):
        qt = q_refs[mi][0].astype(F32).T
        for hh in range(2):
            qm_ref[2 * hh + mi] = jnp.where((feat >= ATTN_HEAD_DIM) == (hh == 1), qt, 0.0).astype(BF16)
    m_ref[...] = jnp.full(m_ref.shape, NEG_INF, F32)
    acc_ref[...] = jnp.zeros(acc_ref.shape, F32)

    def score_phase(j, buf, diagonal):
        rows = pl.ds(pl.multiple_of(j * tq, tq), tq)
        if diagonal:
            band_rows = pl.ds(2 * tq, tq)
            key = lax.broadcasted_iota(jnp.int32, (tq, tq), 0)
            qry = lax.broadcasted_iota(jnp.int32, (tq, tq), 1)
            mask = key <= qry
        else:
            band_rows = pl.ds(pl.multiple_of(jnp.where(j == i - 1, tq, 0), tq), tq)
        for si, (hh, mi) in enumerate(streams):
            s = jnp.dot(k_refs[mi][0, rows, :], qm_ref[si],
                        preferred_element_type=F32) + band_ref[hh, band_rows, :]
            if diagonal:
                s = jnp.where(mask, s, NEG_INF)
            s_ref[buf, si] = s

    def value_phase(j, buf):
        for si, (hh, mi) in enumerate(streams):
            s = s_ref[buf, si]
            m_old = m_ref[si]
            m_new = jnp.maximum(m_old, jnp.max(s, axis=0, keepdims=True))
            alpha = jnp.exp2(m_old - m_new)
            p = jnp.exp2(s - m_new)
            acc_ref[si] = alpha * acc_ref[si] + jnp.dot(vt_ref[j, hh], p.astype(BF16),
                                                        preferred_element_type=F32)
            m_ref[si] = m_new

    @pl.when(i == 0)
    def _():
        score_phase(0, 0, True)

    @pl.when(i > 0)
    def _():
        score_phase(0, 0, False)

    def step(j, parity, diagonal):
        score_phase(j + 1, 1 - parity, diagonal)
        value_phase(j, parity)

    n = i - 1

    def pair_body(jj, carry):
        step(2 * jj, 0, False)
        step(2 * jj + 1, 1, False)
        return carry

    lax.fori_loop(0, jnp.maximum(n, 0) // 2, pair_body, 0)

    @pl.when(jnp.logical_and(i > 0, lax.rem(n, 2) == 1))
    def _():
        step(n - 1, 0, False)
        step(n, 1, True)

    @pl.when(jnp.logical_and(i > 0, lax.rem(n, 2) == 0))
    def _():
        step(n, 0, True)

    value_phase(i, lax.rem(i, 2))

    lam = (jnp.exp(jnp.sum(lq1_ref[...] * lk1_ref[...], axis=-1, keepdims=True))
           - jnp.exp(jnp.sum(lq2_ref[...] * lk2_ref[...], axis=-1, keepdims=True)) + LAMBDA_INIT)
    for hh in range(2):
        s1, s2 = 2 * hh, 2 * hh + 1
        den = V_HEAD_DIM
        a = (acc_ref[s1, :den] / acc_ref[s1, den:den + 1]
             - lam * (acc_ref[s2, :den] / acc_ref[s2, den:den + 1]))
        scale = lax.rsqrt(jnp.mean(a * a, axis=0, keepdims=True) + RMS_EPS)
        a = a * scale * g_ref[...] * (1.0 - LAMBDA_INIT)
        o_ref[0, :, hh * V_HEAD_DIM:(hh + 1) * V_HEAD_DIM] = a.T.astype(o_ref.dtype)


def _attention(qkv, band, lq1, lk1, lq2, lk2, subln_g, *, batch, seq, n_heads, tq, cast=()):
    d_attn = n_heads * V_HEAD_DIM
    pair = 2 * ATTN_HEAD_DIM
    vpair = 2 * V_HEAD_DIM
    n_pairs = d_attn // 2 // pair
    nq = seq // tq
    n_steps = batch * n_pairs * nq
    cast_steps = []
    for w in cast:
        steps = n_steps
        while w.shape[0] % (16 * steps):
            steps //= 2
        cast_steps.append(steps)

    def cast_spec(w, steps):
        return pl.BlockSpec((w.shape[0] // steps, w.shape[1]),
                            lambda b, p, i: (jnp.minimum((b * n_pairs + p) * nq + i, steps - 1), 0))

    cast_specs = [cast_spec(w, steps) for w, steps in zip(cast, cast_steps)]
    qkv3 = qkv.reshape(batch, seq, 3 * d_attn)
    small = pl.BlockSpec((1, ATTN_HEAD_DIM), lambda b, p, i: (0, 0))
    return pl.pallas_call(
        functools.partial(_attn_kernel, tq=tq, cast_steps=tuple(cast_steps)),
        grid=(batch, n_pairs, nq),
        in_specs=[
            small, small, small, small,
            pl.BlockSpec((1, tq, pair), lambda b, p, i: (b, i, p)),
            pl.BlockSpec((1, tq, pair), lambda b, p, i: (b, i, n_pairs + p)),
            pl.BlockSpec((1, seq, pair), lambda b, p, i: (b, 0, 2 * n_pairs + p)),
            pl.BlockSpec((1, seq, pair), lambda b, p, i: (b, 0, 3 * n_pairs + p)),
            pl.BlockSpec((1, seq, vpair), lambda b, p, i: (b, 0, 2 * n_pairs + p)),
            pl.BlockSpec((2, 3 * tq, tq), lambda b, p, i: (p, 0, 0)),
            pl.BlockSpec((V_HEAD_DIM, 1), lambda b, p, i: (0, 0)),
        ] + cast_specs,
        out_specs=[pl.BlockSpec((1, tq, vpair), lambda b, p, i: (b, i, p))] + cast_specs,
        out_shape=[jax.ShapeDtypeStruct((batch, seq, d_attn), BF16)]
                  + [jax.ShapeDtypeStruct(w.shape, BF16) for w in cast],
        scratch_shapes=[pltpu.VMEM((seq // tq, 2, VT_ROWS, tq), BF16),
                        pltpu.VMEM((N_STREAMS, pair, tq), BF16),
                        pltpu.VMEM((2, N_STREAMS, tq, tq), F32),
                        pltpu.VMEM((N_STREAMS, 1, tq), F32),
                        pltpu.VMEM((N_STREAMS, VT_ROWS, tq), F32)],
        compiler_params=_params(("arbitrary", "arbitrary", "arbitrary")),
        name="diff_attention",
    )(lq1, lk1, lq2, lk2, qkv3, qkv3, qkv3, qkv3, qkv3, band, subln_g, *cast)


def _block_diag_blocks(w, dst_ref):
    n_blocks, rows, cols = dst_ref.shape
    g, p, n = w.shape
    gl = g // n_blocks
    row_group = lax.broadcasted_iota(jnp.int32, (rows, cols), 0) // p
    col_group = lax.broadcasted_iota(jnp.int32, (rows, cols), 1) // n
    for j in range(n_blocks):
        stacked = w[j * gl:(j + 1) * gl].reshape(rows, n)
        tiled = jnp.concatenate([stacked] * gl, axis=1)
        dst_ref[j] = jnp.where(row_group == col_group, tiled, 0.0).astype(BF16)


def _ssm_params_kernel(a_re_ref, a_im_ref, log_dt_ref, b_re_ref, b_im_ref, c_re_ref, c_im_ref,
                       ab_re_ref, ab_im_ref, bd_re_ref, bd_im_ref, cd_re_ref, cd_im_ref):
    ar = a_re_ref[...]
    ai = a_im_ref[...]
    dt = jnp.exp(log_dt_ref[...])
    decay = jnp.exp(dt * ar)
    ab_re = decay * jnp.cos(dt * ai)
    ab_im = decay * jnp.sin(dt * ai)
    den = ar * ar + ai * ai
    nr = ab_re - 1.0
    ni = ab_im
    coef_re = (nr * ar + ni * ai) / den
    coef_im = (ni * ar - nr * ai) / den
    br = b_re_ref[...]
    bi = b_im_ref[...]
    ab_re_ref[...] = ab_re
    ab_im_ref[...] = ab_im
    _block_diag_blocks(coef_re * br - coef_im * bi, bd_re_ref)
    _block_diag_blocks(coef_re * bi + coef_im * br, bd_im_ref)
    _block_diag_blocks(c_re_ref[...], cd_re_ref)
    _block_diag_blocks(c_im_ref[...], cd_im_ref)


def _ssm_params(a_re, a_im, log_dt, b_re_t, b_im_t, c_re, c_im):
    g, _, n = a_re.shape
    p = b_re_t.shape[1]
    gl = g // SSM_BLOCKS
    dense = jax.ShapeDtypeStruct((SSM_BLOCKS, gl * p, gl * n), BF16)
    return pl.pallas_call(
        _ssm_params_kernel,
        out_shape=[jax.ShapeDtypeStruct((g, 1, n), F32), jax.ShapeDtypeStruct((g, 1, n), F32),
                   dense, dense, dense, dense],
        compiler_params=pltpu.CompilerParams(vmem_limit_bytes=V7X_VMEM_LIMIT_BYTES),
        name="ssm_params",
    )(a_re, a_im, log_dt, b_re_t, b_im_t, c_re, c_im)


SSM_COL_CHUNK = 512
SSM_BLOCKS = 4


def _ssm_input_drive(u_ref, lhs_ref, bmat_re_ref, bmat_im_ref, bu_re_ref, bu_im_ref, *, tb):
    n_seg = u_ref.shape[1]
    n_slabs = lhs_ref.shape[0]
    for seg in range(n_seg):
        for c in range(n_slabs):
            lhs_ref[c, pl.ds(seg, tb, stride=n_seg), :] = u_ref[0, seg, :, c * V7X_LANES:(c + 1) * V7X_LANES]
    per_block = n_slabs // SSM_BLOCKS
    kst = bu_re_ref.shape[1] // SSM_BLOCKS
    for j in range(SSM_BLOCKS):
        uj = jnp.concatenate([lhs_ref[c] for c in range(j * per_block, (j + 1) * per_block)],
                             axis=-1).astype(BF16)
        bu_re_ref[:, j * kst:(j + 1) * kst] = jnp.dot(uj, bmat_re_ref[j], preferred_element_type=F32)
        bu_im_ref[:, j * kst:(j + 1) * kst] = jnp.dot(uj, bmat_im_ref[j], preferred_element_type=F32)


def _ssm_recurrence(ab_re_ref, ab_im_ref, st_re_ref, st_im_ref, bu_re_ref, bu_im_ref, *, tb,
                    x_re_ref=None, x_im_ref=None):
    n_cols = st_re_ref.shape[1]
    two = 2 * V7X_SUBLANES
    for c in range(n_cols // SSM_COL_CHUNK):
        cs = slice(c * SSM_COL_CHUNK, (c + 1) * SSM_COL_CHUNK)
        ar = jnp.broadcast_to(ab_re_ref[:, cs], (V7X_SUBLANES, SSM_COL_CHUNK))
        ai = jnp.broadcast_to(ab_im_ref[:, cs], (V7X_SUBLANES, SSM_COL_CHUNK))

        def step(t2, carry, cs=cs, ar=ar, ai=ai):
            xr, xi = carry
            rows = pl.ds(pl.multiple_of(t2 * two, two), two)
            bur = bu_re_ref[rows, cs]
            bui = bu_im_ref[rows, cs]
            xr1 = ar * xr - ai * xi + bur[:V7X_SUBLANES]
            xi1 = ar * xi + ai * xr + bui[:V7X_SUBLANES]
            xr2 = ar * xr1 - ai * xi1 + bur[V7X_SUBLANES:]
            xi2 = ar * xi1 + ai * xr1 + bui[V7X_SUBLANES:]
            if x_re_ref is not None:
                x_re_ref[rows, cs] = jnp.concatenate([xr1, xr2], axis=0).astype(BF16)
                x_im_ref[rows, cs] = jnp.concatenate([xi1, xi2], axis=0).astype(BF16)
            return xr2, xi2

        xr, xi = lax.fori_loop(0, tb // 2, step, (st_re_ref[:, cs], st_im_ref[:, cs]), unroll=2)
        st_re_ref[:, cs] = xr
        st_im_ref[:, cs] = xi


def _ssm_ends_kernel(u_ref, bmat_re_ref, bmat_im_ref, ab_re_ref, ab_im_ref, end_re_ref, end_im_ref,
                     lhs_ref, bu_re_ref, bu_im_ref, st_re_ref, st_im_ref, *, tb):
    tblk = pl.program_id(1)

    @pl.when(tblk == 0)
    def _():
        st_re_ref[...] = jnp.zeros_like(st_re_ref)
        st_im_ref[...] = jnp.zeros_like(st_im_ref)

    _ssm_input_drive(u_ref, lhs_ref, bmat_re_ref, bmat_im_ref, bu_re_ref, bu_im_ref, tb=tb)
    _ssm_recurrence(ab_re_ref, ab_im_ref, st_re_ref, st_im_ref, bu_re_ref, bu_im_ref, tb=tb)

    @pl.when(tblk == pl.num_programs(1) - 1)
    def _():
        end_re_ref[0] = st_re_ref[...]
        end_im_ref[0] = st_im_ref[...]


def _ssm_main_kernel(u_ref, end_re_ref, end_im_ref, bmat_re_ref, bmat_im_ref, ab_re_ref, ab_im_ref,
                     cmat_re_ref, cmat_im_ref, d_ref, wglu_ref, bglu_ref, o_ref,
                     lhs_ref, bu_re_ref, bu_im_ref, st_re_ref, st_im_ref, x_re_ref, x_im_ref, *, tb, seg_len):
    tblk = pl.program_id(1)
    n_seg = st_re_ref.shape[0]

    @pl.when(tblk == 0)
    def _():
        pr = ab_re_ref[...]
        pi = ab_im_ref[...]
        for _ in range(int(math.log2(seg_len))):
            pr, pi = pr * pr - pi * pi, 2.0 * pr * pi
        er = jnp.zeros_like(pr)
        ei = jnp.zeros_like(pi)
        st_re_ref[0:1, :] = er
        st_im_ref[0:1, :] = ei
        for j in range(1, n_seg):
            lr = end_re_ref[0, j - 1:j, :]
            li = end_im_ref[0, j - 1:j, :]
            er, ei = lr + pr * er - pi * ei, li + pr * ei + pi * er
            st_re_ref[j:j + 1, :] = er
            st_im_ref[j:j + 1, :] = ei

    _ssm_input_drive(u_ref, lhs_ref, bmat_re_ref, bmat_im_ref, bu_re_ref, bu_im_ref, tb=tb)
    _ssm_recurrence(ab_re_ref, ab_im_ref, st_re_ref, st_im_ref, bu_re_ref, bu_im_ref, tb=tb,
                    x_re_ref=x_re_ref, x_im_ref=x_im_ref)

    n_slabs = lhs_ref.shape[0]
    kst = bu_re_ref.shape[1] // SSM_BLOCKS
    ys = []
    for j in range(SSM_BLOCKS):
        xr = x_re_ref[:, j * kst:(j + 1) * kst]
        xi = x_im_ref[:, j * kst:(j + 1) * kst]
        nt = (((1,), (1,)), ((), ()))
        ys.append(lax.dot_general(xr, cmat_re_ref[j], nt, preferred_element_type=F32)
                  - lax.dot_general(xi, cmat_im_ref[j], nt, preferred_element_type=F32))
    u_rows = jnp.concatenate([lhs_ref[c] for c in range(n_slabs)], axis=-1)
    y = jnp.concatenate(ys, axis=-1) + d_ref[...] * u_rows
    z = jax.nn.gelu(y)
    gate = jnp.dot(z.astype(BF16), wglu_ref[...], preferred_element_type=F32) + bglu_ref[...]
    s = z * jax.nn.sigmoid(gate)
    for c in range(n_slabs):
        lhs_ref[c] = s[:, c * V7X_LANES:(c + 1) * V7X_LANES]
    for seg in range(n_seg):
        for c in range(n_slabs):
            o_ref[0, seg, :, c * V7X_LANES:(c + 1) * V7X_LANES] = lhs_ref[c, pl.ds(seg, tb, stride=n_seg), :]


def _ssm(u_seg, bmat_re, bmat_im, ab_re, ab_im, cmat_re, cmat_im, d_skip, w_glu, b_glu, *, tb):
    batch, n_seg, seg_len, d_ssm = u_seg.shape
    n_cols = ab_re.shape[1]
    rows = tb * n_seg
    grid = (batch, seg_len // tb)
    u_spec = pl.BlockSpec((1, n_seg, tb, d_ssm), lambda b, t: (b, 0, t, 0))
    end_spec = pl.BlockSpec((1, n_seg, n_cols), lambda b, t: (b, 0, 0))

    def whole(a):
        return pl.BlockSpec(a.shape, lambda b, t: (0,) * a.ndim, pipeline_mode=pl.Buffered(1))

    scratch = [pltpu.VMEM((d_ssm // V7X_LANES, rows, V7X_LANES), F32), pltpu.VMEM((rows, n_cols), F32),
               pltpu.VMEM((rows, n_cols), F32), pltpu.VMEM((n_seg, n_cols), F32),
               pltpu.VMEM((n_seg, n_cols), F32)]
    end_re, end_im = pl.pallas_call(
        functools.partial(_ssm_ends_kernel, tb=tb),
        grid=grid,
        in_specs=[u_spec, whole(bmat_re), whole(bmat_im), whole(ab_re), whole(ab_im)],
        out_specs=[end_spec, end_spec],
        out_shape=[jax.ShapeDtypeStruct((batch, n_seg, n_cols), F32)] * 2,
        scratch_shapes=scratch,
        compiler_params=_params(("parallel", "arbitrary")),
        name="ssm_ends",
    )(u_seg, bmat_re, bmat_im, ab_re, ab_im)
    return pl.pallas_call(
        functools.partial(_ssm_main_kernel, tb=tb, seg_len=seg_len),
        grid=grid,
        in_specs=[u_spec, end_spec, end_spec, whole(bmat_re), whole(bmat_im), whole(ab_re), whole(ab_im),
                  whole(cmat_re), whole(cmat_im), whole(d_skip), whole(w_glu), whole(b_glu)],
        out_specs=u_spec,
        out_shape=jax.ShapeDtypeStruct(u_seg.shape, F32),
        scratch_shapes=scratch + [pltpu.VMEM((rows, n_cols), BF16), pltpu.VMEM((rows, n_cols), BF16)],
        compiler_params=_params(("parallel", "arbitrary")),
        name="ssm_main",
    )(u_seg, end_re, end_im, bmat_re, bmat_im, ab_re, ab_im, cmat_re, cmat_im, d_skip, w_glu, b_glu)


def _out_proj_kernel(a_ref, s_ref, wa_ref, ws_ref, x_ref, g_ref, o_ref):
    m = (jnp.dot(a_ref[...], wa_ref[...], preferred_element_type=F32)
         + jnp.dot(s_ref[...].astype(BF16), ws_ref[...], preferred_element_type=F32))
    o_ref[...] = x_ref[...] + m * _rms_scale(m) * g_ref[...]


def _out_proj(a, s, w, x, g, *, tm):
    t, d = x.shape
    da = a.shape[1]
    ds = s.shape[1]
    assert da == ds and w.shape[0] == da + ds
    return pl.pallas_call(
        _out_proj_kernel,
        grid=(t // tm,),
        in_specs=[
            pl.BlockSpec((tm, da), lambda i: (i, 0)),
            pl.BlockSpec((tm, ds), lambda i: (i, 0)),
            pl.BlockSpec((da, d), lambda i: (0, 0)),
            pl.BlockSpec((ds, d), lambda i: (1, 0)),
            pl.BlockSpec((tm, d), lambda i: (i, 0)),
            pl.BlockSpec((1, d), lambda i: (0, 0)),
        ],
        out_specs=pl.BlockSpec((tm, d), lambda i: (i, 0)),
        out_shape=jax.ShapeDtypeStruct((t, d), F32),
        compiler_params=_params(("parallel",)),
        name="out_proj",
    )(a, s, w, w, x, g)


def kernel(x, ffn1_pre_g, ffn1_w_gate, ffn1_w_up, ffn1_w_down, ffn1_post_g, mix_pre_g, w_in, lambda_q1, lambda_k1, lambda_q2, lambda_k2, subln_g, rel_bias, ssm_a_re, ssm_a_im, ssm_b_re, ssm_b_im, ssm_c_re, ssm_c_im, ssm_d, ssm_log_dt, w_glu, b_glu, w_out, mix_post_g, ffn2_pre_g, ffn2_w_gate, ffn2_w_up, ffn2_w_down, ffn2_post_g):
    batch, seq, d_model = x.shape
    depth = ffn1_pre_g.shape[0]
    assert depth == 1, "LAMBDA_INIT is specialised to a single layer"
    n_heads = rel_bias.shape[1]
    d_attn = n_heads * V_HEAD_DIM
    n_groups, n_state = ssm_a_re.shape[1:]
    d_ssm = n_groups * SSM_GROUP
    n_seg = V7X_SUBLANES
    seg_len = seq // n_seg
    tokens = batch * seq
    tm, tf, tq, tb = 512, 512, 256, 32

    xt = x.reshape(tokens, d_model)
    row = lambda v: v.reshape(1, -1)
    l = 0

    x1, w_in_bf, w_glu_bf = _ffn(
        xt, row(ffn1_pre_g[l]), ffn1_w_gate[l].astype(BF16), ffn1_w_up[l].astype(BF16),
        ffn1_w_down[l].astype(BF16), row(ffn1_post_g[l]), tm=tm, tf=tf, cast=(w_in[l], w_glu[l]))

    qkv, u = _in_proj(x1, row(mix_pre_g[l]), w_in_bf, tm=tm, q_scale=ATTN_HEAD_DIM ** -0.5 * LOG2_E)

    band = _bias_band(rel_bias, tq=tq, scale=LOG2_E)
    a, wg2, wu2, wd2, wo = _attention(
        qkv, band, row(lambda_q1[l]), row(lambda_k1[l]), row(lambda_q2[l]), row(lambda_k2[l]),
        subln_g[l].reshape(-1, 1), batch=batch, seq=seq, n_heads=n_heads, tq=tq,
        cast=(ffn2_w_gate[l], ffn2_w_up[l], ffn2_w_down[l], w_out[l]))

    ab_re, ab_im, bmat_re, bmat_im, cmat_re, cmat_im = _ssm_params(
        ssm_a_re[l].reshape(n_groups, 1, n_state), ssm_a_im[l].reshape(n_groups, 1, n_state),
        ssm_log_dt[l].reshape(n_groups, 1, 1),
        jnp.swapaxes(ssm_b_re[l], 1, 2), jnp.swapaxes(ssm_b_im[l], 1, 2), ssm_c_re[l], ssm_c_im[l])
    u_seg = u.reshape(batch, n_seg, seg_len, d_ssm)
    s_seg = _ssm(u_seg, bmat_re, bmat_im, ab_re.reshape(1, -1), ab_im.reshape(1, -1), cmat_re, cmat_im,
                 row(ssm_d[l]), w_glu_bf, row(b_glu[l]), tb=tb)
    s = s_seg.reshape(tokens, d_ssm)

    x2 = _out_proj(a.reshape(tokens, d_attn), s, wo, x1, row(mix_post_g[l]), tm=tm)

    (x3,) = _ffn(x2, row(ffn2_pre_g[l]), wg2, wu2, wd2, row(ffn2_post_g[l]), tm=tm, tf=tf)
    return x3.reshape(batch, seq, d_model)
```

```python
import functools
import math

import jax
import jax.numpy as jnp
from jax import lax
from jax.experimental import pallas as pl
from jax.experimental.pallas import tpu as pltpu

V7X_LANES = 128
V7X_SUBLANES = 8
V7X_MXU_COLS = 256
V7X_VMEM_LIMIT_BYTES = 56 * 1024 * 1024

RMS_EPS = 1e-6
NEG_INF = -1e30
N_BUCKETS = 32
MAX_DISTANCE = 128
ATTN_HEAD_DIM = 64
V_HEAD_DIM = 128
SSM_GROUP = 16
SSM_STATE = 64
LAMBDA_INIT = 0.8 - 0.6 * math.exp(-0.3 * 0)

F32 = jnp.float32
BF16 = jnp.bfloat16
EPILOGUE_ROWS = 128


def _params(semantics):
    return pltpu.CompilerParams(dimension_semantics=semantics,
                                vmem_limit_bytes=V7X_VMEM_LIMIT_BYTES)


def _rms_scale(v):
    return lax.rsqrt(jnp.mean(v * v, axis=-1, keepdims=True) + RMS_EPS)


def _ffn_prologue(x_ref, pre_g_ref, xn_ref, acc_ref):
    xf = x_ref[...]
    xn_ref[...] = (xf * _rms_scale(xf) * pre_g_ref[...]).astype(BF16)
    acc_ref[...] = jnp.zeros_like(acc_ref)


def _ffn_chunk(xn_ref, acc_ref, wg_ref, wu_ref, wd_ref):
    xn = xn_ref[...]
    n_split = 2 if wg_ref.shape[1] >= 2 * V7X_MXU_COLS else 1
    half = wg_ref.shape[1] // n_split
    pre = []
    for c in range(n_split):
        cols = slice(c * half, (c + 1) * half)
        pre.append((jnp.dot(xn, wg_ref[:, cols], preferred_element_type=F32),
                    jnp.dot(xn, wu_ref[:, cols], preferred_element_type=F32)))
    out = acc_ref[...]
    for c, (gate, up) in enumerate(pre):
        hidden = (gate * jax.nn.sigmoid(gate) * up).astype(BF16)
        out = out + jnp.dot(hidden, wd_ref[c * half:(c + 1) * half, :], preferred_element_type=F32)
    acc_ref[...] = out


def _ffn_epilogue(x_ref, acc_ref, post_g_ref, o_ref):
    scale = _rms_scale(acc_ref[...])
    g_half = 0.5 * post_g_ref[...]
    for r in range(0, acc_ref.shape[0], EPILOGUE_ROWS):
        rows = slice(r, r + EPILOGUE_ROWS)
        o_ref[rows, :] = x_ref[rows, :] + acc_ref[rows, :] * scale[rows] * g_half


def _ffn_kernel(*refs, aliased):
    x_ref, pre_g_ref, wg_ref, wu_ref, wd_ref, post_g_ref, o_ref, xn_ref, acc_ref = refs[int(aliased):]
    k = pl.program_id(1)

    @pl.when(k == 0)
    def _():
        _ffn_prologue(x_ref, pre_g_ref, xn_ref, acc_ref)

    _ffn_chunk(xn_ref, acc_ref, wg_ref, wu_ref, wd_ref)

    @pl.when(k == pl.num_programs(1) - 1)
    def _():
        _ffn_epilogue(x_ref, acc_ref, post_g_ref, o_ref)


def _ffn_first_kernel(x_ref, pre_g_ref, wg32_ref, wu32_ref, wd32_ref, post_g_ref, *rest, cast_steps):
    n_cast = len(cast_steps)
    cast_in = rest[:n_cast]
    o_ref, wg_ref, wu_ref, wd_ref = rest[n_cast:n_cast + 4]
    cast_out = rest[n_cast + 4:2 * n_cast + 4]
    xn_ref, acc_ref = rest[2 * n_cast + 4:]
    k = pl.program_id(0)

    @pl.when(k == 0)
    def _():
        _ffn_prologue(x_ref, pre_g_ref, xn_ref, acc_ref)

    for src, dst, steps in zip(cast_in, cast_out, cast_steps):
        @pl.when(k < steps)
        def _(src=src, dst=dst):
            dst[...] = src[...].astype(BF16)

    wg_ref[...] = wg32_ref[...].astype(BF16)
    wu_ref[...] = wu32_ref[...].astype(BF16)
    wd_ref[...] = wd32_ref[...].astype(BF16)
    _ffn_chunk(xn_ref, acc_ref, wg_ref, wu_ref, wd_ref)

    @pl.when(k == pl.num_programs(0) - 1)
    def _():
        _ffn_epilogue(x_ref, acc_ref, post_g_ref, o_ref)


def _ffn_first(x, pre_g, wg32, wu32, wd32, post_g, *, tm, tf, cast=()):
    t, d = x.shape
    f = wg32.shape[1]
    n_chunks = f // tf
    cast_steps = []
    for w in cast:
        steps = 1
        while steps * 2 <= n_chunks and w.shape[0] % (16 * steps * 2) == 0:
            steps *= 2
        cast_steps.append(steps)
    cast_specs = [pl.BlockSpec((w.shape[0] // steps, w.shape[1]),
                               lambda k, steps=steps: (jnp.minimum(k, steps - 1), 0))
                  for w, steps in zip(cast, cast_steps)]
    w_specs = [pl.BlockSpec((d, tf), lambda k: (0, k)), pl.BlockSpec((d, tf), lambda k: (0, k)),
               pl.BlockSpec((tf, d), lambda k: (k, 0))]
    return pl.pallas_call(
        functools.partial(_ffn_first_kernel, cast_steps=tuple(cast_steps)),
        grid=(n_chunks,),
        in_specs=[pl.BlockSpec((tm, d), lambda k: (0, 0), pipeline_mode=pl.Buffered(1)),
                  pl.BlockSpec((1, d), lambda k: (0, 0))]
                 + w_specs + [pl.BlockSpec((1, d), lambda k: (0, 0))] + cast_specs,
        out_specs=[pl.BlockSpec((tm, d), lambda k: (0, 0))] + w_specs + cast_specs,
        out_shape=[jax.ShapeDtypeStruct((t, d), F32)]
                  + [jax.ShapeDtypeStruct(w.shape, BF16) for w in (wg32, wu32, wd32) + tuple(cast)],
        scratch_shapes=[pltpu.VMEM((tm, d), BF16), pltpu.VMEM((tm, d), F32)],
        compiler_params=_params(("arbitrary",)),
        name="ffn_first",
    )(x, pre_g, wg32, wu32, wd32, post_g, *cast)


def _ffn(x, pre_g, wg, wu, wd, post_g, *, tm, tf, y_first=None):
    t, d = x.shape
    f = wg.shape[1]
    aliased = y_first is not None
    skip = int(aliased)
    return pl.pallas_call(
        functools.partial(_ffn_kernel, aliased=aliased),
        grid=(t // tm - skip, f // tf),
        in_specs=([pl.BlockSpec(memory_space=pl.ANY)] if aliased else []) + [
            pl.BlockSpec((tm, d), lambda i, k: (i + skip, 0)),
            pl.BlockSpec((1, d), lambda i, k: (0, 0)),
            pl.BlockSpec((d, tf), lambda i, k: (0, k)),
            pl.BlockSpec((d, tf), lambda i, k: (0, k)),
            pl.BlockSpec((tf, d), lambda i, k: (k, 0)),
            pl.BlockSpec((1, d), lambda i, k: (0, 0)),
        ],
        out_specs=pl.BlockSpec((tm, d), lambda i, k: (i + skip, 0)),
        out_shape=jax.ShapeDtypeStruct((t, d), F32),
        input_output_aliases={0: 0} if aliased else {},
        scratch_shapes=[pltpu.VMEM((tm, d), BF16), pltpu.VMEM((tm, d), F32)],
        compiler_params=_params(("parallel", "arbitrary")),
        name="ffn",
    )(*((y_first,) if aliased else ()), x, pre_g, wg, wu, wd, post_g)


def _in_proj_kernel(x_ref, g_ref, w_ref, qkv_ref, u_ref, *, q_scale):
    xf = x_ref[...]
    hn = (xf * _rms_scale(xf) * g_ref[...]).astype(BF16)
    tn = u_ref.shape[1]
    for n in range(4):
        acc = jnp.dot(hn, w_ref[:, n * tn:(n + 1) * tn], preferred_element_type=F32)
        if n == 0:
            qkv_ref[:, :tn] = (acc * q_scale).astype(BF16)
        elif n < 3:
            qkv_ref[:, n * tn:(n + 1) * tn] = acc.astype(BF16)
        else:
            u_ref[...] = acc


def _in_proj(x, g, w, *, tm, q_scale):
    t, d = x.shape
    tn = w.shape[1] // 4
    return pl.pallas_call(
        functools.partial(_in_proj_kernel, q_scale=q_scale),
        grid=(t // tm,),
        in_specs=[
            pl.BlockSpec((tm, d), lambda i: (i, 0)),
            pl.BlockSpec((1, d), lambda i: (0, 0)),
            pl.BlockSpec(w.shape, lambda i: (0, 0), pipeline_mode=pl.Buffered(1)),
        ],
        out_specs=[
            pl.BlockSpec((tm, 3 * tn), lambda i: (i, 0)),
            pl.BlockSpec((tm, tn), lambda i: (i, 0)),
        ],
        out_shape=[jax.ShapeDtypeStruct((t, 3 * tn), BF16),
                   jax.ShapeDtypeStruct((t, tn), F32)],
        compiler_params=_params(("parallel",)),
        name="in_proj",
    )(x, g, w)


def _bias_band_kernel(rb_ref, o_ref, *, tq, scale):
    h = pl.program_id(0)
    key = lax.broadcasted_iota(jnp.int32, (3 * tq, tq), 0)
    qry = lax.broadcasted_iota(jnp.int32, (3 * tq, tq), 1)
    n = jnp.maximum(2 * tq + qry - key, 0)
    max_exact = N_BUCKETS // 2
    nf = jnp.maximum(n, 1).astype(F32)
    large = max_exact + (jnp.log(nf / max_exact) / math.log(MAX_DISTANCE / max_exact)
                         * (N_BUCKETS - max_exact)).astype(jnp.int32)
    large = jnp.minimum(large, N_BUCKETS - 1)
    bucket = jnp.where(n < max_exact, n, large)
    val = jnp.zeros((3 * tq, tq), F32)
    for b in range(N_BUCKETS):
        val = jnp.where(bucket == b, rb_ref[b, h], val)
    o_ref[0] = val * scale


def _bias_band(rel_bias, *, tq, scale):
    assert tq >= MAX_DISTANCE
    n_heads = rel_bias.shape[1]
    return pl.pallas_call(
        functools.partial(_bias_band_kernel, tq=tq, scale=scale),
        grid=(n_heads,),
        in_specs=[pl.BlockSpec(memory_space=pltpu.SMEM)],
        out_specs=pl.BlockSpec((1, 3 * tq, tq), lambda h: (h, 0, 0)),
        out_shape=jax.ShapeDtypeStruct((n_heads, 3 * tq, tq), F32),
        compiler_params=_params(("arbitrary",)),
        name="bias_band",
    )(rel_bias)


N_STREAMS = 4
VT_ROWS = V_HEAD_DIM + 16
LOG2_E = math.log2(math.e)


def _attn_kernel(lq1_ref, lk1_ref, lq2_ref, lk2_ref, q1_ref, q2_ref, k1_ref, k2_ref,
                 v_ref, band_ref, g_ref, *rest, tq, cast_steps):
    n_cast = len(cast_steps)
    cast_in = rest[:n_cast]
    o_ref = rest[n_cast]
    cast_out = rest[n_cast + 1:2 * n_cast + 1]
    vt_ref, qm_ref, s_ref, m_ref, acc_ref = rest[2 * n_cast + 1:]
    i = pl.program_id(2)
    step_id = (pl.program_id(0) * pl.num_programs(1) + pl.program_id(1)) * pl.num_programs(2) + i
    for src, dst, steps in zip(cast_in, cast_out, cast_steps):
        @pl.when(step_id < steps)
        def _(src=src, dst=dst):
            dst[...] = src[...].astype(BF16)

    n_chunks = v_ref.shape[1] // tq
    q_refs = (q1_ref, q2_ref)
    k_refs = (k1_ref, k2_ref)
    streams = [(hh, mi) for hh in range(2) for mi in range(2)]

    @pl.when(i == 0)
    def _():
        ones_pad = (lax.broadcasted_iota(jnp.int32, (VT_ROWS - V_HEAD_DIM, tq), 0) == 0).astype(BF16)
        for c in range(n_chunks):
            vt = v_ref[0, c * tq:(c + 1) * tq, :].astype(F32).T.astype(BF16)
            for hh in range(2):
                vt_ref[c, hh, :V_HEAD_DIM, :] = vt[hh * V_HEAD_DIM:(hh + 1) * V_HEAD_DIM]
                vt_ref[c, hh, V_HEAD_DIM:, :] = ones_pad

    feat = lax.broadcasted_iota(jnp.int32, (2 * ATTN_HEAD_DIM, tq), 0)
    for mi in range(2):
        qt = q_refs[mi][0].astype(F32).T
        for hh in range(2):
            qm_ref[2 * hh + mi] = jnp.where((feat >= ATTN_HEAD_DIM) == (hh == 1), qt, 0.0).astype(BF16)
    m_ref[...] = jnp.full(m_ref.shape, NEG_INF, F32)
    acc_ref[...] = jnp.zeros(acc_ref.shape, F32)

    def score_phase(j, buf, diagonal):
        rows = pl.ds(pl.multiple_of(j * tq, tq), tq)
        if diagonal:
            band_rows = pl.ds(2 * tq, tq)
            key = lax.broadcasted_iota(jnp.int32, (tq, tq), 0)
            qry = lax.broadcasted_iota(jnp.int32, (tq, tq), 1)
            mask = key <= qry
        else:
            band_rows = pl.ds(pl.multiple_of(jnp.where(j == i - 1, tq, 0), tq), tq)
        for si, (hh, mi) in enumerate(streams):
            s = jnp.dot(k_refs[mi][0, rows, :], qm_ref[si],
                        preferred_element_type=F32) + band_ref[hh, band_rows, :]
            if diagonal:
                s = jnp.where(mask, s, NEG_INF)
            s_ref[buf, si] = s

    def value_phase(j, buf):
        for si, (hh, mi) in enumerate(streams):
            s = s_ref[buf, si]
            m_old = m_ref[si]
            m_new = jnp.maximum(m_old, jnp.max(s, axis=0, keepdims=True))
            alpha = jnp.exp2(m_old - m_new)
            p = jnp.exp2(s - m_new)
            acc_ref[si] = alpha * acc_ref[si] + jnp.dot(vt_ref[j, hh], p.astype(BF16),
                                                        preferred_element_type=F32)
            m_ref[si] = m_new

    @pl.when(i == 0)
    def _():
        score_phase(0, 0, True)

    @pl.when(i > 0)
    def _():
        score_phase(0, 0, False)

    def step(j, parity, diagonal):
        score_phase(j + 1, 1 - parity, diagonal)
        value_phase(j, parity)

    n = i - 1

    def pair_body(jj, carry):
        step(2 * jj, 0, False)
        step(2 * jj + 1, 1, False)
        return carry

    lax.fori_loop(0, jnp.maximum(n, 0) // 2, pair_body, 0)

    @pl.when(jnp.logical_and(i > 0, lax.rem(n, 2) == 1))
    def _():
        step(n - 1, 0, False)
        step(n, 1, True)

    @pl.when(jnp.logical_and(i > 0, lax.rem(n, 2) == 0))
    def _():
        step(n, 0, True)

    value_phase(i, lax.rem(i, 2))

    lam = (jnp.exp(jnp.sum(lq1_ref[...] * lk1_ref[...], axis=-1, keepdims=True))
           - jnp.exp(jnp.sum(lq2_ref[...] * lk2_ref[...], axis=-1, keepdims=True)) + LAMBDA_INIT)
    for hh in range(2):
        s1, s2 = 2 * hh, 2 * hh + 1
        den = V_HEAD_DIM
        a = (acc_ref[s1, :den] / acc_ref[s1, den:den + 1]
             - lam * (acc_ref[s2, :den] / acc_ref[s2, den:den + 1]))
        scale = lax.rsqrt(jnp.mean(a * a, axis=0, keepdims=True) + RMS_EPS)
        a = a * scale * g_ref[...] * (1.0 - LAMBDA_INIT)
        o_ref[0, :, hh * V_HEAD_DIM:(hh + 1) * V_HEAD_DIM] = a.T.astype(o_ref.dtype)


def _attention(qkv, band, lq1, lk1, lq2, lk2, subln_g, *, batch, seq, n_heads, tq, cast=()):
    d_attn = n_heads * V_HEAD_DIM
    pair = 2 * ATTN_HEAD_DIM
    vpair = 2 * V_HEAD_DIM
    n_pairs = d_attn // 2 // pair
    nq = seq // tq
    n_steps = batch * n_pairs * nq
    cast_steps = []
    for w in cast:
        steps = n_steps
        while w.shape[0] % (16 * steps):
            steps //= 2
        cast_steps.append(steps)

    def cast_spec(w, steps):
        return pl.BlockSpec((w.shape[0] // steps, w.shape[1]),
                            lambda b, p, i: (jnp.minimum((b * n_pairs + p) * nq + i, steps - 1), 0))

    cast_specs = [cast_spec(w, steps) for w, steps in zip(cast, cast_steps)]
    qkv3 = qkv.reshape(batch, seq, 3 * d_attn)
    small = pl.BlockSpec((1, ATTN_HEAD_DIM), lambda b, p, i: (0, 0))
    return pl.pallas_call(
        functools.partial(_attn_kernel, tq=tq, cast_steps=tuple(cast_steps)),
        grid=(batch, n_pairs, nq),
        in_specs=[
            small, small, small, small,
            pl.BlockSpec((1, tq, pair), lambda b, p, i: (b, i, p)),
            pl.BlockSpec((1, tq, pair), lambda b, p, i: (b, i, n_pairs + p)),
            pl.BlockSpec((1, seq, pair), lambda b, p, i: (b, 0, 2 * n_pairs + p)),
            pl.BlockSpec((1, seq, pair), lambda b, p, i: (b, 0, 3 * n_pairs + p)),
            pl.BlockSpec((1, seq, vpair), lambda b, p, i: (b, 0, 2 * n_pairs + p)),
            pl.BlockSpec((2, 3 * tq, tq), lambda b, p, i: (p, 0, 0)),
            pl.BlockSpec((V_HEAD_DIM, 1), lambda b, p, i: (0, 0)),
        ] + cast_specs,
        out_specs=[pl.BlockSpec((1, tq, vpair), lambda b, p, i: (b, i, p))] + cast_specs,
        out_shape=[jax.ShapeDtypeStruct((batch, seq, d_attn), BF16)]
                  + [jax.ShapeDtypeStruct(w.shape, BF16) for w in cast],
        scratch_shapes=[pltpu.VMEM((seq // tq, 2, VT_ROWS, tq), BF16),
                        pltpu.VMEM((N_STREAMS, pair, tq), BF16),
                        pltpu.VMEM((2, N_STREAMS, tq, tq), F32),
                        pltpu.VMEM((N_STREAMS, 1, tq), F32),
                        pltpu.VMEM((N_STREAMS, VT_ROWS, tq), F32)],
        compiler_params=_params(("arbitrary", "arbitrary", "arbitrary")),
        name="diff_attention",
    )(lq1, lk1, lq2, lk2, qkv3, qkv3, qkv3, qkv3, qkv3, band, subln_g, *cast)


def _block_diag_blocks(w, dst_ref):
    n_blocks, rows, cols = dst_ref.shape
    g, p, n = w.shape
    gl = g // n_blocks
    row_group = lax.broadcasted_iota(jnp.int32, (rows, cols), 0) // p
    col_group = lax.broadcasted_iota(jnp.int32, (rows, cols), 1) // n
    for j in range(n_blocks):
        stacked = w[j * gl:(j + 1) * gl].reshape(rows, n)
        tiled = jnp.concatenate([stacked] * gl, axis=1)
        dst_ref[j] = jnp.where(row_group == col_group, tiled, 0.0).astype(BF16)


def _ssm_params_kernel(a_re_ref, a_im_ref, log_dt_ref, b_re_ref, b_im_ref, c_re_ref, c_im_ref,
                       ab_re_ref, ab_im_ref, bd_re_ref, bd_im_ref, cd_re_ref, cd_im_ref):
    ar = a_re_ref[...]
    ai = a_im_ref[...]
    dt = jnp.exp(log_dt_ref[...])
    decay = jnp.exp(dt * ar)
    ab_re = decay * jnp.cos(dt * ai)
    ab_im = decay * jnp.sin(dt * ai)
    den = ar * ar + ai * ai
    nr = ab_re - 1.0
    ni = ab_im
    coef_re = (nr * ar + ni * ai) / den
    coef_im = (ni * ar - nr * ai) / den
    br = b_re_ref[...]
    bi = b_im_ref[...]
    ab_re_ref[...] = ab_re
    ab_im_ref[...] = ab_im
    _block_diag_blocks(coef_re * br - coef_im * bi, bd_re_ref)
    _block_diag_blocks(coef_re * bi + coef_im * br, bd_im_ref)
    _block_diag_blocks(c_re_ref[...], cd_re_ref)
    _block_diag_blocks(c_im_ref[...], cd_im_ref)


def _ssm_params(a_re, a_im, log_dt, b_re_t, b_im_t, c_re, c_im):
    g, _, n = a_re.shape
    p = b_re_t.shape[1]
    gl = g // SSM_BLOCKS
    dense = jax.ShapeDtypeStruct((SSM_BLOCKS, gl * p, gl * n), BF16)
    return pl.pallas_call(
        _ssm_params_kernel,
        out_shape=[jax.ShapeDtypeStruct((g, 1, n), F32), jax.ShapeDtypeStruct((g, 1, n), F32),
                   dense, dense, dense, dense],
        compiler_params=pltpu.CompilerParams(vmem_limit_bytes=V7X_VMEM_LIMIT_BYTES),
        name="ssm_params",
    )(a_re, a_im, log_dt, b_re_t, b_im_t, c_re, c_im)


SSM_COL_CHUNK = 512
SSM_BLOCKS = 4


def _ssm_input_drive(u_ref, lhs_ref, bmat_re_ref, bmat_im_ref, bu_re_ref, bu_im_ref, *, tb):
    n_seg = u_ref.shape[1]
    n_slabs = lhs_ref.shape[0]
    for seg in range(n_seg):
        for c in range(n_slabs):
            lhs_ref[c, pl.ds(seg, tb, stride=n_seg), :] = u_ref[0, seg, :, c * V7X_LANES:(c + 1) * V7X_LANES]
    per_block = n_slabs // SSM_BLOCKS
    kst = bu_re_ref.shape[1] // SSM_BLOCKS
    for j in range(SSM_BLOCKS):
        uj = jnp.concatenate([lhs_ref[c] for c in range(j * per_block, (j + 1) * per_block)],
                             axis=-1).astype(BF16)
        bu_re_ref[:, j * kst:(j + 1) * kst] = jnp.dot(uj, bmat_re_ref[j], preferred_element_type=F32)
        bu_im_ref[:, j * kst:(j + 1) * kst] = jnp.dot(uj, bmat_im_ref[j], preferred_element_type=F32)


def _ssm_recurrence(ab_re_ref, ab_im_ref, st_re_ref, st_im_ref, bu_re_ref, bu_im_ref, *, tb,
                    x_re_ref=None, x_im_ref=None):
    n_cols = st_re_ref.shape[1]
    two = 2 * V7X_SUBLANES
    for c in range(n_cols // SSM_COL_CHUNK):
        cs = slice(c * SSM_COL_CHUNK, (c + 1) * SSM_COL_CHUNK)
        ar = jnp.broadcast_to(ab_re_ref[:, cs], (V7X_SUBLANES, SSM_COL_CHUNK))
        ai = jnp.broadcast_to(ab_im_ref[:, cs], (V7X_SUBLANES, SSM_COL_CHUNK))

        def step(t2, carry, cs=cs, ar=ar, ai=ai):
            xr, xi = carry
            rows = pl.ds(pl.multiple_of(t2 * two, two), two)
            bur = bu_re_ref[rows, cs]
            bui = bu_im_ref[rows, cs]
            xr1 = ar * xr - ai * xi + bur[:V7X_SUBLANES]
            xi1 = ar * xi + ai * xr + bui[:V7X_SUBLANES]
            xr2 = ar * xr1 - ai * xi1 + bur[V7X_SUBLANES:]
            xi2 = ar * xi1 + ai * xr1 + bui[V7X_SUBLANES:]
            if x_re_ref is not None:
                x_re_ref[rows, cs] = jnp.concatenate([xr1, xr2], axis=0).astype(BF16)
                x_im_ref[rows, cs] = jnp.concatenate([xi1, xi2], axis=0).astype(BF16)
            return xr2, xi2

        xr, xi = lax.fori_loop(0, tb // 2, step, (st_re_ref[:, cs], st_im_ref[:, cs]), unroll=2)
        st_re_ref[:, cs] = xr
        st_im_ref[:, cs] = xi


def _ssm_ends_kernel(u_ref, bmat_re_ref, bmat_im_ref, ab_re_ref, ab_im_ref, end_re_ref, end_im_ref,
                     lhs_ref, bu_re_ref, bu_im_ref, st_re_ref, st_im_ref, *, tb):
    tblk = pl.program_id(1)

    @pl.when(tblk == 0)
    def _():
        st_re_ref[...] = jnp.zeros_like(st_re_ref)
        st_im_ref[...] = jnp.zeros_like(st_im_ref)

    _ssm_input_drive(u_ref, lhs_ref, bmat_re_ref, bmat_im_ref, bu_re_ref, bu_im_ref, tb=tb)
    _ssm_recurrence(ab_re_ref, ab_im_ref, st_re_ref, st_im_ref, bu_re_ref, bu_im_ref, tb=tb)

    @pl.when(tblk == pl.num_programs(1) - 1)
    def _():
        end_re_ref[0] = st_re_ref[...]
        end_im_ref[0] = st_im_ref[...]


def _ssm_main_kernel(u_ref, end_re_ref, end_im_ref, bmat_re_ref, bmat_im_ref, ab_re_ref, ab_im_ref,
                     cmat_re_ref, cmat_im_ref, d_ref, wglu_ref, bglu_ref, o_ref,
                     lhs_ref, bu_re_ref, bu_im_ref, st_re_ref, st_im_ref, x_re_ref, x_im_ref, *, tb, seg_len):
    tblk = pl.program_id(1)
    n_seg = st_re_ref.shape[0]

    @pl.when(tblk == 0)
    def _():
        pr = ab_re_ref[...]
        pi = ab_im_ref[...]
        for _ in range(int(math.log2(seg_len))):
            pr, pi = pr * pr - pi * pi, 2.0 * pr * pi
        er = jnp.zeros_like(pr)
        ei = jnp.zeros_like(pi)
        st_re_ref[0:1, :] = er
        st_im_ref[0:1, :] = ei
        for j in range(1, n_seg):
            lr = end_re_ref[0, j - 1:j, :]
            li = end_im_ref[0, j - 1:j, :]
            er, ei = lr + pr * er - pi * ei, li + pr * ei + pi * er
            st_re_ref[j:j + 1, :] = er
            st_im_ref[j:j + 1, :] = ei

    _ssm_input_drive(u_ref, lhs_ref, bmat_re_ref, bmat_im_ref, bu_re_ref, bu_im_ref, tb=tb)
    _ssm_recurrence(ab_re_ref, ab_im_ref, st_re_ref, st_im_ref, bu_re_ref, bu_im_ref, tb=tb,
                    x_re_ref=x_re_ref, x_im_ref=x_im_ref)

    n_slabs = lhs_ref.shape[0]
    kst = bu_re_ref.shape[1] // SSM_BLOCKS
    ys = []
    for j in range(SSM_BLOCKS):
        xr = x_re_ref[:, j * kst:(j + 1) * kst]
        xi = x_im_ref[:, j * kst:(j + 1) * kst]
        nt = (((1,), (1,)), ((), ()))
        ys.append(lax.dot_general(xr, cmat_re_ref[j], nt, preferred_element_type=F32)
                  - lax.dot_general(xi, cmat_im_ref[j], nt, preferred_element_type=F32))
    u_rows = jnp.concatenate([lhs_ref[c] for c in range(n_slabs)], axis=-1)
    y = jnp.concatenate(ys, axis=-1) + d_ref[...] * u_rows
    z = jax.nn.gelu(y)
    gate = jnp.dot(z.astype(BF16), wglu_ref[...], preferred_element_type=F32) + bglu_ref[...]
    s = z * jax.nn.sigmoid(gate)
    for c in range(n_slabs):
        lhs_ref[c] = s[:, c * V7X_LANES:(c + 1) * V7X_LANES]
    for seg in range(n_seg):
        for c in range(n_slabs):
            o_ref[0, seg, :, c * V7X_LANES:(c + 1) * V7X_LANES] = lhs_ref[c, pl.ds(seg, tb, stride=n_seg), :]


def _ssm(u_seg, bmat_re, bmat_im, ab_re, ab_im, cmat_re, cmat_im, d_skip, w_glu, b_glu, *, tb):
    batch, n_seg, seg_len, d_ssm = u_seg.shape
    n_cols = ab_re.shape[1]
    rows = tb * n_seg
    grid = (batch, seg_len // tb)
    u_spec = pl.BlockSpec((1, n_seg, tb, d_ssm), lambda b, t: (b, 0, t, 0))
    end_spec = pl.BlockSpec((1, n_seg, n_cols), lambda b, t: (b, 0, 0))

    def whole(a):
        return pl.BlockSpec(a.shape, lambda b, t: (0,) * a.ndim, pipeline_mode=pl.Buffered(1))

    scratch = [pltpu.VMEM((d_ssm // V7X_LANES, rows, V7X_LANES), F32), pltpu.VMEM((rows, n_cols), F32),
               pltpu.VMEM((rows, n_cols), F32), pltpu.VMEM((n_seg, n_cols), F32),
               pltpu.VMEM((n_seg, n_cols), F32)]
    end_re, end_im = pl.pallas_call(
        functools.partial(_ssm_ends_kernel, tb=tb),
        grid=grid,
        in_specs=[u_spec, whole(bmat_re), whole(bmat_im), whole(ab_re), whole(ab_im)],
        out_specs=[end_spec, end_spec],
        out_shape=[jax.ShapeDtypeStruct((batch, n_seg, n_cols), F32)] * 2,
        scratch_shapes=scratch,
        compiler_params=_params(("parallel", "arbitrary")),
        name="ssm_ends",
    )(u_seg, bmat_re, bmat_im, ab_re, ab_im)
    return pl.pallas_call(
        functools.partial(_ssm_main_kernel, tb=tb, seg_len=seg_len),
        grid=grid,
        in_specs=[u_spec, end_spec, end_spec, whole(bmat_re), whole(bmat_im), whole(ab_re), whole(ab_im),
                  whole(cmat_re), whole(cmat_im), whole(d_skip), whole(w_glu), whole(b_glu)],
        out_specs=u_spec,
        out_shape=jax.ShapeDtypeStruct(u_seg.shape, F32),
        scratch_shapes=scratch + [pltpu.VMEM((rows, n_cols), BF16), pltpu.VMEM((rows, n_cols), BF16)],
        compiler_params=_params(("parallel", "arbitrary")),
        name="ssm_main",
    )(u_seg, end_re, end_im, bmat_re, bmat_im, ab_re, ab_im, cmat_re, cmat_im, d_skip, w_glu, b_glu)


def _out_proj_kernel(a_ref, s_ref, wa_ref, ws_ref, x_ref, g_ref, o_ref):
    m = (jnp.dot(a_ref[...], wa_ref[...], preferred_element_type=F32)
         + jnp.dot(s_ref[...].astype(BF16), ws_ref[...], preferred_element_type=F32))
    o_ref[...] = x_ref[...] + m * _rms_scale(m) * g_ref[...]


def _out_proj(a, s, w, x, g, *, tm):
    t, d = x.shape
    da = a.shape[1]
    ds = s.shape[1]
    assert da == ds and w.shape[0] == da + ds
    return pl.pallas_call(
        _out_proj_kernel,
        grid=(t // tm,),
        in_specs=[
            pl.BlockSpec((tm, da), lambda i: (i, 0)),
            pl.BlockSpec((tm, ds), lambda i: (i, 0)),
            pl.BlockSpec((da, d), lambda i: (0, 0)),
            pl.BlockSpec((ds, d), lambda i: (1, 0)),
            pl.BlockSpec((tm, d), lambda i: (i, 0)),
            pl.BlockSpec((1, d), lambda i: (0, 0)),
        ],
        out_specs=pl.BlockSpec((tm, d), lambda i: (i, 0)),
        out_shape=jax.ShapeDtypeStruct((t, d), F32),
        compiler_params=_params(("parallel",)),
        name="out_proj",
    )(a, s, w, w, x, g)


def kernel(x, ffn1_pre_g, ffn1_w_gate, ffn1_w_up, ffn1_w_down, ffn1_post_g, mix_pre_g, w_in, lambda_q1, lambda_k1, lambda_q2, lambda_k2, subln_g, rel_bias, ssm_a_re, ssm_a_im, ssm_b_re, ssm_b_im, ssm_c_re, ssm_c_im, ssm_d, ssm_log_dt, w_glu, b_glu, w_out, mix_post_g, ffn2_pre_g, ffn2_w_gate, ffn2_w_up, ffn2_w_down, ffn2_post_g):
    batch, seq, d_model = x.shape
    depth = ffn1_pre_g.shape[0]
    assert depth == 1, "LAMBDA_INIT is specialised to a single layer"
    n_heads = rel_bias.shape[1]
    d_attn = n_heads * V_HEAD_DIM
    n_groups, n_state = ssm_a_re.shape[1:]
    d_ssm = n_groups * SSM_GROUP
    n_seg = V7X_SUBLANES
    seg_len = seq // n_seg
    tokens = batch * seq
    tm, tf, tq, tb = 512, 512, 256, 32

    xt = x.reshape(tokens, d_model)
    row = lambda v: v.reshape(1, -1)
    l = 0

    y_first, wg1, wu1, wd1, w_in_bf, w_glu_bf = _ffn_first(
        xt, row(ffn1_pre_g[l]), ffn1_w_gate[l], ffn1_w_up[l], ffn1_w_down[l], row(ffn1_post_g[l]),
        tm=tm, tf=tf // 2, cast=(w_in[l], w_glu[l]))
    x1 = _ffn(xt, row(ffn1_pre_g[l]), wg1, wu1, wd1, row(ffn1_post_g[l]), tm=tm, tf=tf, y_first=y_first)

    qkv, u = _in_proj(x1, row(mix_pre_g[l]), w_in_bf, tm=tm, q_scale=ATTN_HEAD_DIM ** -0.5 * LOG2_E)

    band = _bias_band(rel_bias, tq=tq, scale=LOG2_E)
    a, wg2, wu2, wd2, wo = _attention(
        qkv, band, row(lambda_q1[l]), row(lambda_k1[l]), row(lambda_q2[l]), row(lambda_k2[l]),
        subln_g[l].reshape(-1, 1), batch=batch, seq=seq, n_heads=n_heads, tq=tq,
        cast=(ffn2_w_gate[l], ffn2_w_up[l], ffn2_w_down[l], w_out[l]))

    ab_re, ab_im, bmat_re, bmat_im, cmat_re, cmat_im = _ssm_params(
        ssm_a_re[l].reshape(n_groups, 1, n_state), ssm_a_im[l].reshape(n_groups, 1, n_state),
        ssm_log_dt[l].reshape(n_groups, 1, 1),
        jnp.swapaxes(ssm_b_re[l], 1, 2), jnp.swapaxes(ssm_b_im[l], 1, 2), ssm_c_re[l], ssm_c_im[l])
    u_seg = u.reshape(batch, n_seg, seg_len, d_ssm)
    s_seg = _ssm(u_seg, bmat_re, bmat_im, ab_re.reshape(1, -1), ab_im.reshape(1, -1), cmat_re, cmat_im,
                 row(ssm_d[l]), w_glu_bf, row(b_glu[l]), tb=tb)
    s = s_seg.reshape(tokens, d_ssm)

    x2 = _out_proj(a.reshape(tokens, d_attn), s, wo, x1, row(mix_post_g[l]), tm=tm)

    x3 = _ffn(x2, row(ffn2_pre_g[l]), wg2, wu2, wd2, row(ffn2_post_g[l]), tm=tm, tf=tf)
    return x3.reshape(batch, seq, d_model)
```

```python
import functools
import math

import jax
import jax.numpy as jnp
from jax import lax
from jax.experimental import pallas as pl
from jax.experimental.pallas import tpu as pltpu

V7X_LANES = 128
V7X_SUBLANES = 8
V7X_MXU_COLS = 256
V7X_VMEM_LIMIT_BYTES = 56 * 1024 * 1024

RMS_EPS = 1e-6
NEG_INF = -1e30
N_BUCKETS = 32
MAX_DISTANCE = 128
ATTN_HEAD_DIM = 64
V_HEAD_DIM = 128
SSM_GROUP = 16
SSM_STATE = 64
LAMBDA_INIT = 0.8 - 0.6 * math.exp(-0.3 * 0)

F32 = jnp.float32
BF16 = jnp.bfloat16
EPILOGUE_ROWS = 128


def _params(semantics):
    return pltpu.CompilerParams(dimension_semantics=semantics,
                                vmem_limit_bytes=V7X_VMEM_LIMIT_BYTES)


def _rms_scale(v):
    return lax.rsqrt(jnp.mean(v * v, axis=-1, keepdims=True) + RMS_EPS)


def _ffn_prologue(x_ref, pre_g_ref, xn_ref, acc_ref):
    xf = x_ref[...]
    xn_ref[...] = (xf * _rms_scale(xf) * pre_g_ref[...]).astype(BF16)
    acc_ref[...] = jnp.zeros_like(acc_ref)


def _ffn_chunk(xn_ref, acc_ref, wg_ref, wu_ref, wd_ref):
    xn = xn_ref[...]
    n_split = 2 if wg_ref.shape[1] >= 2 * V7X_MXU_COLS else 1
    half = wg_ref.shape[1] // n_split
    pre = []
    for c in range(n_split):
        cols = slice(c * half, (c + 1) * half)
        pre.append((jnp.dot(xn, wg_ref[:, cols], preferred_element_type=F32),
                    jnp.dot(xn, wu_ref[:, cols], preferred_element_type=F32)))
    out = acc_ref[...]
    for c, (gate, up) in enumerate(pre):
        hidden = (gate * jax.nn.sigmoid(gate) * up).astype(BF16)
        out = out + jnp.dot(hidden, wd_ref[c * half:(c + 1) * half, :], preferred_element_type=F32)
    acc_ref[...] = out


def _ffn_epilogue(x_ref, acc_ref, post_g_ref, o_ref):
    scale = _rms_scale(acc_ref[...])
    g_half = 0.5 * post_g_ref[...]
    for r in range(0, acc_ref.shape[0], EPILOGUE_ROWS):
        rows = slice(r, r + EPILOGUE_ROWS)
        o_ref[rows, :] = x_ref[rows, :] + acc_ref[rows, :] * scale[rows] * g_half


def _ffn_kernel(*refs, aliased):
    x_ref, pre_g_ref, wg_ref, wu_ref, wd_ref, post_g_ref, o_ref, xn_ref, acc_ref = refs[int(aliased):]
    k = pl.program_id(1)

    @pl.when(k == 0)
    def _():
        _ffn_prologue(x_ref, pre_g_ref, xn_ref, acc_ref)

    _ffn_chunk(xn_ref, acc_ref, wg_ref, wu_ref, wd_ref)

    @pl.when(k == pl.num_programs(1) - 1)
    def _():
        _ffn_epilogue(x_ref, acc_ref, post_g_ref, o_ref)


def _ffn_first_kernel(x_ref, pre_g_ref, wg32_ref, wu32_ref, wd32_ref, post_g_ref, *rest, cast_steps):
    n_cast = len(cast_steps)
    cast_in = rest[:n_cast]
    o_ref, wg_ref, wu_ref, wd_ref = rest[n_cast:n_cast + 4]
    cast_out = rest[n_cast + 4:2 * n_cast + 4]
    xn_ref, acc_ref = rest[2 * n_cast + 4:]
    k = pl.program_id(0)

    @pl.when(k == 0)
    def _():
        _ffn_prologue(x_ref, pre_g_ref, xn_ref, acc_ref)

    for src, dst, steps in zip(cast_in, cast_out, cast_steps):
        @pl.when(k < steps)
        def _(src=src, dst=dst):
            dst[...] = src[...].astype(BF16)

    wg = wg32_ref[...].astype(BF16)
    wu = wu32_ref[...].astype(BF16)
    wd = wd32_ref[...].astype(BF16)
    wg_ref[...] = wg
    wu_ref[...] = wu
    wd_ref[...] = wd
    _ffn_chunk(xn_ref, acc_ref, wg, wu, wd)

    @pl.when(k == pl.num_programs(0) - 1)
    def _():
        _ffn_epilogue(x_ref, acc_ref, post_g_ref, o_ref)


def _ffn_first(x, pre_g, wg32, wu32, wd32, post_g, *, tm, tf, cast=()):
    t, d = x.shape
    f = wg32.shape[1]
    n_chunks = f // tf
    cast_steps = []
    for w in cast:
        steps = 1
        while steps * 2 <= n_chunks and w.shape[0] % (16 * steps * 2) == 0:
            steps *= 2
        cast_steps.append(steps)
    cast_specs = [pl.BlockSpec((w.shape[0] // steps, w.shape[1]),
                               lambda k, steps=steps: (jnp.minimum(k, steps - 1), 0))
                  for w, steps in zip(cast, cast_steps)]
    w_specs = [pl.BlockSpec((d, tf), lambda k: (0, k)), pl.BlockSpec((d, tf), lambda k: (0, k)),
               pl.BlockSpec((tf, d), lambda k: (k, 0))]
    return pl.pallas_call(
        functools.partial(_ffn_first_kernel, cast_steps=tuple(cast_steps)),
        grid=(n_chunks,),
        in_specs=[pl.BlockSpec((tm, d), lambda k: (0, 0), pipeline_mode=pl.Buffered(1)),
                  pl.BlockSpec((1, d), lambda k: (0, 0))]
                 + w_specs + [pl.BlockSpec((1, d), lambda k: (0, 0))] + cast_specs,
        out_specs=[pl.BlockSpec((tm, d), lambda k: (0, 0), pipeline_mode=pl.Buffered(1))] + w_specs + cast_specs,
        out_shape=[jax.ShapeDtypeStruct((t, d), F32)]
                  + [jax.ShapeDtypeStruct(w.shape, BF16) for w in (wg32, wu32, wd32) + tuple(cast)],
        scratch_shapes=[pltpu.VMEM((tm, d), BF16), pltpu.VMEM((tm, d), F32)],
        compiler_params=_params(("arbitrary",)),
        name="ffn_first",
    )(x, pre_g, wg32, wu32, wd32, post_g, *cast)


def _ffn(x, pre_g, wg, wu, wd, post_g, *, tm, tf, y_first=None, rows_done=0):
    t, d = x.shape
    f = wg.shape[1]
    aliased = y_first is not None
    assert (rows_done > 0) == aliased and rows_done % tm == 0
    skip = rows_done // tm
    return pl.pallas_call(
        functools.partial(_ffn_kernel, aliased=aliased),
        grid=(t // tm - skip, f // tf),
        in_specs=([pl.BlockSpec(memory_space=pl.ANY)] if aliased else []) + [
            pl.BlockSpec((tm, d), lambda i, k: (i + skip, 0)),
            pl.BlockSpec((1, d), lambda i, k: (0, 0)),
            pl.BlockSpec((d, tf), lambda i, k: (0, k)),
            pl.BlockSpec((d, tf), lambda i, k: (0, k)),
            pl.BlockSpec((tf, d), lambda i, k: (k, 0)),
            pl.BlockSpec((1, d), lambda i, k: (0, 0)),
        ],
        out_specs=pl.BlockSpec((tm, d), lambda i, k: (i + skip, 0)),
        out_shape=jax.ShapeDtypeStruct((t, d), F32),
        input_output_aliases={0: 0} if aliased else {},
        scratch_shapes=[pltpu.VMEM((tm, d), BF16), pltpu.VMEM((tm, d), F32)],
        compiler_params=_params(("parallel", "arbitrary")),
        name="ffn",
    )(*((y_first,) if aliased else ()), x, pre_g, wg, wu, wd, post_g)


def _in_proj_kernel(x_ref, g_ref, w_ref, qkv_ref, u_ref, *, q_scale):
    xf = x_ref[...]
    hn = (xf * _rms_scale(xf) * g_ref[...]).astype(BF16)
    tn = u_ref.shape[1]
    for n in range(4):
        acc = jnp.dot(hn, w_ref[:, n * tn:(n + 1) * tn], preferred_element_type=F32)
        if n == 0:
            qkv_ref[:, :tn] = (acc * q_scale).astype(BF16)
        elif n < 3:
            qkv_ref[:, n * tn:(n + 1) * tn] = acc.astype(BF16)
        else:
            u_ref[...] = acc


def _in_proj(x, g, w, *, tm, q_scale):
    t, d = x.shape
    tn = w.shape[1] // 4
    return pl.pallas_call(
        functools.partial(_in_proj_kernel, q_scale=q_scale),
        grid=(t // tm,),
        in_specs=[
            pl.BlockSpec((tm, d), lambda i: (i, 0)),
            pl.BlockSpec((1, d), lambda i: (0, 0)),
            pl.BlockSpec(w.shape, lambda i: (0, 0), pipeline_mode=pl.Buffered(1)),
        ],
        out_specs=[
            pl.BlockSpec((tm, 3 * tn), lambda i: (i, 0)),
            pl.BlockSpec((tm, tn), lambda i: (i, 0)),
        ],
        out_shape=[jax.ShapeDtypeStruct((t, 3 * tn), BF16),
                   jax.ShapeDtypeStruct((t, tn), F32)],
        compiler_params=_params(("parallel",)),
        name="in_proj",
    )(x, g, w)


def _bias_band_kernel(rb_ref, o_ref, *, tq, scale):
    h = pl.program_id(0)
    key = lax.broadcasted_iota(jnp.int32, (3 * tq, tq), 0)
    qry = lax.broadcasted_iota(jnp.int32, (3 * tq, tq), 1)
    n = jnp.maximum(2 * tq + qry - key, 0)
    max_exact = N_BUCKETS // 2
    nf = jnp.maximum(n, 1).astype(F32)
    large = max_exact + (jnp.log(nf / max_exact) / math.log(MAX_DISTANCE / max_exact)
                         * (N_BUCKETS - max_exact)).astype(jnp.int32)
    large = jnp.minimum(large, N_BUCKETS - 1)
    bucket = jnp.where(n < max_exact, n, large)
    val = jnp.zeros((3 * tq, tq), F32)
    for b in range(N_BUCKETS):
        val = jnp.where(bucket == b, rb_ref[b, h], val)
    o_ref[0] = val * scale


def _bias_band(rel_bias, *, tq, scale):
    assert tq >= MAX_DISTANCE
    n_heads = rel_bias.shape[1]
    return pl.pallas_call(
        functools.partial(_bias_band_kernel, tq=tq, scale=scale),
        grid=(n_heads,),
        in_specs=[pl.BlockSpec(memory_space=pltpu.SMEM)],
        out_specs=pl.BlockSpec((1, 3 * tq, tq), lambda h: (h, 0, 0)),
        out_shape=jax.ShapeDtypeStruct((n_heads, 3 * tq, tq), F32),
        compiler_params=_params(("arbitrary",)),
        name="bias_band",
    )(rel_bias)


N_STREAMS = 4
VT_ROWS = V_HEAD_DIM + 16
LOG2_E = math.log2(math.e)


def _attn_kernel(lq1_ref, lk1_ref, lq2_ref, lk2_ref, q1_ref, q2_ref, k1_ref, k2_ref,
                 v_ref, band_ref, g_ref, *rest, tq, cast_steps):
    n_cast = len(cast_steps)
    cast_in = rest[:n_cast]
    o_ref = rest[n_cast]
    cast_out = rest[n_cast + 1:2 * n_cast + 1]
    vt_ref, qm_ref, s_ref, m_ref, acc_ref = rest[2 * n_cast + 1:]
    i = pl.program_id(2)
    step_id = (pl.program_id(0) * pl.num_programs(1) + pl.program_id(1)) * pl.num_programs(2) + i
    for src, dst, steps in zip(cast_in, cast_out, cast_steps):
        @pl.when(step_id < steps)
        def _(src=src, dst=dst):
            dst[...] = src[...].astype(BF16)

    n_chunks = v_ref.shape[1] // tq
    q_refs = (q1_ref, q2_ref)
    k_refs = (k1_ref, k2_ref)
    streams = [(hh, mi) for hh in range(2) for mi in range(2)]

    @pl.when(i == 0)
    def _():
        ones_pad = (lax.broadcasted_iota(jnp.int32, (VT_ROWS - V_HEAD_DIM, tq), 0) == 0).astype(BF16)
        for c in range(n_chunks):
            vt = v_ref[0, c * tq:(c + 1) * tq, :].astype(F32).T.astype(BF16)
            for hh in range(2):
                vt_ref[c, hh, :V_HEAD_DIM, :] = vt[hh * V_HEAD_DIM:(hh + 1) * V_HEAD_DIM]
                vt_ref[c, hh, V_HEAD_DIM:, :] = ones_pad

    feat = lax.broadcasted_iota(jnp.int32, (2 * ATTN_HEAD_DIM, tq), 0)
    for mi in range(2):
        qt = q_refs[mi][0].astype(F32).T
        for hh in range(2):
            qm_ref[2 * hh + mi] = jnp.where((feat >= ATTN_HEAD_DIM) == (hh == 1), qt, 0.0).astype(BF16)
    m_ref[...] = jnp.full(m_ref.shape, NEG_INF, F32)
    acc_ref[...] = jnp.zeros(acc_ref.shape, F32)

    def score_phase(j, buf, diagonal):
        rows = pl.ds(pl.multiple_of(j * tq, tq), tq)
        if diagonal:
            band_rows = pl.ds(2 * tq, tq)
            key = lax.broadcasted_iota(jnp.int32, (tq, tq), 0)
            qry = lax.broadcasted_iota(jnp.int32, (tq, tq), 1)
            mask = key <= qry
        else:
            band_rows = pl.ds(pl.multiple_of(jnp.where(j == i - 1, tq, 0), tq), tq)
        for si, (hh, mi) in enumerate(streams):
            s = jnp.dot(k_refs[mi][0, rows, :], qm_ref[si],
                        preferred_element_type=F32) + band_ref[hh, band_rows, :]
            if diagonal:
                s = jnp.where(mask, s, NEG_INF)
            s_ref[buf, si] = s

    def value_phase(j, buf):
        for si, (hh, mi) in enumerate(streams):
            s = s_ref[buf, si]
            m_old = m_ref[si]
            m_new = jnp.maximum(m_old, jnp.max(s, axis=0, keepdims=True))
            alpha = jnp.exp2(m_old - m_new)
            p = jnp.exp2(s - m_new)
            acc_ref[si] = alpha * acc_ref[si] + jnp.dot(vt_ref[j, hh], p.astype(BF16),
                                                        preferred_element_type=F32)
            m_ref[si] = m_new

    @pl.when(i == 0)
    def _():
        score_phase(0, 0, True)

    @pl.when(i > 0)
    def _():
        score_phase(0, 0, False)

    def step(j, parity, diagonal):
        score_phase(j + 1, 1 - parity, diagonal)
        value_phase(j, parity)

    n = i - 1

    def pair_body(jj, carry):
        step(2 * jj, 0, False)
        step(2 * jj + 1, 1, False)
        return carry

    lax.fori_loop(0, jnp.maximum(n, 0) // 2, pair_body, 0)

    @pl.when(jnp.logical_and(i > 0, lax.rem(n, 2) == 1))
    def _():
        step(n - 1, 0, False)
        step(n, 1, True)

    @pl.when(jnp.logical_and(i > 0, lax.rem(n, 2) == 0))
    def _():
        step(n, 0, True)

    value_phase(i, lax.rem(i, 2))

    lam = (jnp.exp(jnp.sum(lq1_ref[...] * lk1_ref[...], axis=-1, keepdims=True))
           - jnp.exp(jnp.sum(lq2_ref[...] * lk2_ref[...], axis=-1, keepdims=True)) + LAMBDA_INIT)
    for hh in range(2):
        s1, s2 = 2 * hh, 2 * hh + 1
        den = V_HEAD_DIM
        a = (acc_ref[s1, :den] / acc_ref[s1, den:den + 1]
             - lam * (acc_ref[s2, :den] / acc_ref[s2, den:den + 1]))
        scale = lax.rsqrt(jnp.mean(a * a, axis=0, keepdims=True) + RMS_EPS)
        a = a * scale * g_ref[...] * (1.0 - LAMBDA_INIT)
        o_ref[0, :, hh * V_HEAD_DIM:(hh + 1) * V_HEAD_DIM] = a.T.astype(o_ref.dtype)


def _attention(qkv, band, lq1, lk1, lq2, lk2, subln_g, *, batch, seq, n_heads, tq, cast=()):
    d_attn = n_heads * V_HEAD_DIM
    pair = 2 * ATTN_HEAD_DIM
    vpair = 2 * V_HEAD_DIM
    n_pairs = d_attn // 2 // pair
    nq = seq // tq
    n_steps = batch * n_pairs * nq
    cast_steps = []
    for w in cast:
        steps = n_steps
        while w.shape[0] % (16 * steps):
            steps //= 2
        cast_steps.append(steps)

    def cast_spec(w, steps):
        return pl.BlockSpec((w.shape[0] // steps, w.shape[1]),
                            lambda b, p, i: (jnp.minimum((b * n_pairs + p) * nq + i, steps - 1), 0))

    cast_specs = [cast_spec(w, steps) for w, steps in zip(cast, cast_steps)]
    qkv3 = qkv.reshape(batch, seq, 3 * d_attn)
    small = pl.BlockSpec((1, ATTN_HEAD_DIM), lambda b, p, i: (0, 0))
    return pl.pallas_call(
        functools.partial(_attn_kernel, tq=tq, cast_steps=tuple(cast_steps)),
        grid=(batch, n_pairs, nq),
        in_specs=[
            small, small, small, small,
            pl.BlockSpec((1, tq, pair), lambda b, p, i: (b, i, p)),
            pl.BlockSpec((1, tq, pair), lambda b, p, i: (b, i, n_pairs + p)),
            pl.BlockSpec((1, seq, pair), lambda b, p, i: (b, 0, 2 * n_pairs + p)),
            pl.BlockSpec((1, seq, pair), lambda b, p, i: (b, 0, 3 * n_pairs + p)),
            pl.BlockSpec((1, seq, vpair), lambda b, p, i: (b, 0, 2 * n_pairs + p)),
            pl.BlockSpec((2, 3 * tq, tq), lambda b, p, i: (p, 0, 0)),
            pl.BlockSpec((V_HEAD_DIM, 1), lambda b, p, i: (0, 0)),
        ] + cast_specs,
        out_specs=[pl.BlockSpec((1, tq, vpair), lambda b, p, i: (b, i, p))] + cast_specs,
        out_shape=[jax.ShapeDtypeStruct((batch, seq, d_attn), BF16)]
                  + [jax.ShapeDtypeStruct(w.shape, BF16) for w in cast],
        scratch_shapes=[pltpu.VMEM((seq // tq, 2, VT_ROWS, tq), BF16),
                        pltpu.VMEM((N_STREAMS, pair, tq), BF16),
                        pltpu.VMEM((2, N_STREAMS, tq, tq), F32),
                        pltpu.VMEM((N_STREAMS, 1, tq), F32),
                        pltpu.VMEM((N_STREAMS, VT_ROWS, tq), F32)],
        compiler_params=_params(("arbitrary", "arbitrary", "arbitrary")),
        name="diff_attention",
    )(lq1, lk1, lq2, lk2, qkv3, qkv3, qkv3, qkv3, qkv3, band, subln_g, *cast)


def _block_diag_blocks(w, dst_ref):
    n_blocks, rows, cols = dst_ref.shape
    g, p, n = w.shape
    gl = g // n_blocks
    row_group = lax.broadcasted_iota(jnp.int32, (rows, cols), 0) // p
    col_group = lax.broadcasted_iota(jnp.int32, (rows, cols), 1) // n
    for j in range(n_blocks):
        stacked = w[j * gl:(j + 1) * gl].reshape(rows, n)
        tiled = jnp.concatenate([stacked] * gl, axis=1)
        dst_ref[j] = jnp.where(row_group == col_group, tiled, 0.0).astype(BF16)


def _ssm_params_kernel(a_re_ref, a_im_ref, log_dt_ref, b_re_ref, b_im_ref, c_re_ref, c_im_ref,
                       ab_re_ref, ab_im_ref, bd_re_ref, bd_im_ref, cd_re_ref, cd_im_ref):
    ar = a_re_ref[...]
    ai = a_im_ref[...]
    dt = jnp.exp(log_dt_ref[...])
    decay = jnp.exp(dt * ar)
    ab_re = decay * jnp.cos(dt * ai)
    ab_im = decay * jnp.sin(dt * ai)
    den = ar * ar + ai * ai
    nr = ab_re - 1.0
    ni = ab_im
    coef_re = (nr * ar + ni * ai) / den
    coef_im = (ni * ar - nr * ai) / den
    br = b_re_ref[...]
    bi = b_im_ref[...]
    ab_re_ref[...] = ab_re
    ab_im_ref[...] = ab_im
    _block_diag_blocks(coef_re * br - coef_im * bi, bd_re_ref)
    _block_diag_blocks(coef_re * bi + coef_im * br, bd_im_ref)
    _block_diag_blocks(c_re_ref[...], cd_re_ref)
    _block_diag_blocks(c_im_ref[...], cd_im_ref)


def _ssm_params(a_re, a_im, log_dt, b_re_t, b_im_t, c_re, c_im):
    g, _, n = a_re.shape
    p = b_re_t.shape[1]
    gl = g // SSM_BLOCKS
    dense = jax.ShapeDtypeStruct((SSM_BLOCKS, gl * p, gl * n), BF16)
    return pl.pallas_call(
        _ssm_params_kernel,
        out_shape=[jax.ShapeDtypeStruct((g, 1, n), F32), jax.ShapeDtypeStruct((g, 1, n), F32),
                   dense, dense, dense, dense],
        compiler_params=pltpu.CompilerParams(vmem_limit_bytes=V7X_VMEM_LIMIT_BYTES),
        name="ssm_params",
    )(a_re, a_im, log_dt, b_re_t, b_im_t, c_re, c_im)


SSM_COL_CHUNK = 512
SSM_BLOCKS = 4


def _ssm_input_drive(u_ref, lhs_ref, bmat_re_ref, bmat_im_ref, bu_re_ref, bu_im_ref, *, tb):
    n_seg = u_ref.shape[1]
    n_slabs = lhs_ref.shape[0]
    for seg in range(n_seg):
        for c in range(n_slabs):
            lhs_ref[c, pl.ds(seg, tb, stride=n_seg), :] = u_ref[0, seg, :, c * V7X_LANES:(c + 1) * V7X_LANES]
    per_block = n_slabs // SSM_BLOCKS
    kst = bu_re_ref.shape[1] // SSM_BLOCKS
    for j in range(SSM_BLOCKS):
        uj = jnp.concatenate([lhs_ref[c] for c in range(j * per_block, (j + 1) * per_block)],
                             axis=-1).astype(BF16)
        bu_re_ref[:, j * kst:(j + 1) * kst] = jnp.dot(uj, bmat_re_ref[j], preferred_element_type=F32)
        bu_im_ref[:, j * kst:(j + 1) * kst] = jnp.dot(uj, bmat_im_ref[j], preferred_element_type=F32)


def _ssm_recurrence(ab_re_ref, ab_im_ref, st_re_ref, st_im_ref, bu_re_ref, bu_im_ref, *, tb,
                    x_re_ref=None, x_im_ref=None):
    n_cols = st_re_ref.shape[1]
    two = 2 * V7X_SUBLANES
    for c in range(n_cols // SSM_COL_CHUNK):
        cs = slice(c * SSM_COL_CHUNK, (c + 1) * SSM_COL_CHUNK)
        ar = jnp.broadcast_to(ab_re_ref[:, cs], (V7X_SUBLANES, SSM_COL_CHUNK))
        ai = jnp.broadcast_to(ab_im_ref[:, cs], (V7X_SUBLANES, SSM_COL_CHUNK))

        def step(t2, carry, cs=cs, ar=ar, ai=ai):
            xr, xi = carry
            rows = pl.ds(pl.multiple_of(t2 * two, two), two)
            bur = bu_re_ref[rows, cs]
            bui = bu_im_ref[rows, cs]
            xr1 = ar * xr - ai * xi + bur[:V7X_SUBLANES]
            xi1 = ar * xi + ai * xr + bui[:V7X_SUBLANES]
            xr2 = ar * xr1 - ai * xi1 + bur[V7X_SUBLANES:]
            xi2 = ar * xi1 + ai * xr1 + bui[V7X_SUBLANES:]
            if x_re_ref is not None:
                x_re_ref[rows, cs] = jnp.concatenate([xr1, xr2], axis=0).astype(BF16)
                x_im_ref[rows, cs] = jnp.concatenate([xi1, xi2], axis=0).astype(BF16)
            return xr2, xi2

        xr, xi = lax.fori_loop(0, tb // 2, step, (st_re_ref[:, cs], st_im_ref[:, cs]), unroll=2)
        st_re_ref[:, cs] = xr
        st_im_ref[:, cs] = xi


def _ssm_ends_kernel(u_ref, bmat_re_ref, bmat_im_ref, ab_re_ref, ab_im_ref, end_re_ref, end_im_ref,
                     lhs_ref, bu_re_ref, bu_im_ref, st_re_ref, st_im_ref, *, tb):
    tblk = pl.program_id(1)

    @pl.when(tblk == 0)
    def _():
        st_re_ref[...] = jnp.zeros_like(st_re_ref)
        st_im_ref[...] = jnp.zeros_like(st_im_ref)

    _ssm_input_drive(u_ref, lhs_ref, bmat_re_ref, bmat_im_ref, bu_re_ref, bu_im_ref, tb=tb)
    _ssm_recurrence(ab_re_ref, ab_im_ref, st_re_ref, st_im_ref, bu_re_ref, bu_im_ref, tb=tb)

    @pl.when(tblk == pl.num_programs(1) - 1)
    def _():
        end_re_ref[0] = st_re_ref[...]
        end_im_ref[0] = st_im_ref[...]


def _ssm_main_kernel(u_ref, end_re_ref, end_im_ref, bmat_re_ref, bmat_im_ref, ab_re_ref, ab_im_ref,
                     cmat_re_ref, cmat_im_ref, d_ref, wglu_ref, bglu_ref, o_ref,
                     lhs_ref, bu_re_ref, bu_im_ref, st_re_ref, st_im_ref, x_re_ref, x_im_ref, *, tb, seg_len):
    tblk = pl.program_id(1)
    n_seg = st_re_ref.shape[0]

    @pl.when(tblk == 0)
    def _():
        pr = ab_re_ref[...]
        pi = ab_im_ref[...]
        for _ in range(int(math.log2(seg_len))):
            pr, pi = pr * pr - pi * pi, 2.0 * pr * pi
        er = jnp.zeros_like(pr)
        ei = jnp.zeros_like(pi)
        st_re_ref[0:1, :] = er
        st_im_ref[0:1, :] = ei
        for j in range(1, n_seg):
            lr = end_re_ref[0, j - 1:j, :]
            li = end_im_ref[0, j - 1:j, :]
            er, ei = lr + pr * er - pi * ei, li + pr * ei + pi * er
            st_re_ref[j:j + 1, :] = er
            st_im_ref[j:j + 1, :] = ei

    _ssm_input_drive(u_ref, lhs_ref, bmat_re_ref, bmat_im_ref, bu_re_ref, bu_im_ref, tb=tb)
    _ssm_recurrence(ab_re_ref, ab_im_ref, st_re_ref, st_im_ref, bu_re_ref, bu_im_ref, tb=tb,
                    x_re_ref=x_re_ref, x_im_ref=x_im_ref)

    n_slabs = lhs_ref.shape[0]
    kst = bu_re_ref.shape[1] // SSM_BLOCKS
    ys = []
    for j in range(SSM_BLOCKS):
        xr = x_re_ref[:, j * kst:(j + 1) * kst]
        xi = x_im_ref[:, j * kst:(j + 1) * kst]
        nt = (((1,), (1,)), ((), ()))
        ys.append(lax.dot_general(xr, cmat_re_ref[j], nt, preferred_element_type=F32)
                  - lax.dot_general(xi, cmat_im_ref[j], nt, preferred_element_type=F32))
    u_rows = jnp.concatenate([lhs_ref[c] for c in range(n_slabs)], axis=-1)
    y = jnp.concatenate(ys, axis=-1) + d_ref[...] * u_rows
    z = jax.nn.gelu(y)
    gate = jnp.dot(z.astype(BF16), wglu_ref[...], preferred_element_type=F32) + bglu_ref[...]
    s = z * jax.nn.sigmoid(gate)
    for c in range(n_slabs):
        lhs_ref[c] = s[:, c * V7X_LANES:(c + 1) * V7X_LANES]
    for seg in range(n_seg):
        for c in range(n_slabs):
            o_ref[0, seg, :, c * V7X_LANES:(c + 1) * V7X_LANES] = lhs_ref[c, pl.ds(seg, tb, stride=n_seg), :]


def _ssm(u_seg, bmat_re, bmat_im, ab_re, ab_im, cmat_re, cmat_im, d_skip, w_glu, b_glu, *, tb):
    batch, n_seg, seg_len, d_ssm = u_seg.shape
    n_cols = ab_re.shape[1]
    rows = tb * n_seg
    grid = (batch, seg_len // tb)
    u_spec = pl.BlockSpec((1, n_seg, tb, d_ssm), lambda b, t: (b, 0, t, 0))
    end_spec = pl.BlockSpec((1, n_seg, n_cols), lambda b, t: (b, 0, 0))

    def whole(a):
        return pl.BlockSpec(a.shape, lambda b, t: (0,) * a.ndim, pipeline_mode=pl.Buffered(1))

    scratch = [pltpu.VMEM((d_ssm // V7X_LANES, rows, V7X_LANES), F32), pltpu.VMEM((rows, n_cols), F32),
               pltpu.VMEM((rows, n_cols), F32), pltpu.VMEM((n_seg, n_cols), F32),
               pltpu.VMEM((n_seg, n_cols), F32)]
    end_re, end_im = pl.pallas_call(
        functools.partial(_ssm_ends_kernel, tb=tb),
        grid=grid,
        in_specs=[u_spec, whole(bmat_re), whole(bmat_im), whole(ab_re), whole(ab_im)],
        out_specs=[end_spec, end_spec],
        out_shape=[jax.ShapeDtypeStruct((batch, n_seg, n_cols), F32)] * 2,
        scratch_shapes=scratch,
        compiler_params=_params(("parallel", "arbitrary")),
        name="ssm_ends",
    )(u_seg, bmat_re, bmat_im, ab_re, ab_im)
    return pl.pallas_call(
        functools.partial(_ssm_main_kernel, tb=tb, seg_len=seg_len),
        grid=grid,
        in_specs=[u_spec, end_spec, end_spec, whole(bmat_re), whole(bmat_im), whole(ab_re), whole(ab_im),
                  whole(cmat_re), whole(cmat_im), whole(d_skip), whole(w_glu), whole(b_glu)],
        out_specs=u_spec,
        out_shape=jax.ShapeDtypeStruct(u_seg.shape, F32),
        scratch_shapes=scratch + [pltpu.VMEM((rows, n_cols), BF16), pltpu.VMEM((rows, n_cols), BF16)],
        compiler_params=_params(("parallel", "arbitrary")),
        name="ssm_main",
    )(u_seg, end_re, end_im, bmat_re, bmat_im, ab_re, ab_im, cmat_re, cmat_im, d_skip, w_glu, b_glu)


def _out_proj_kernel(a_ref, s_ref, wa_ref, ws_ref, x_ref, g_ref, o_ref):
    m = (jnp.dot(a_ref[...], wa_ref[...], preferred_element_type=F32)
         + jnp.dot(s_ref[...].astype(BF16), ws_ref[...], preferred_element_type=F32))
    o_ref[...] = x_ref[...] + m * _rms_scale(m) * g_ref[...]


def _out_proj(a, s, w, x, g, *, tm):
    t, d = x.shape
    da = a.shape[1]
    ds = s.shape[1]
    assert da == ds and w.shape[0] == da + ds
    return pl.pallas_call(
        _out_proj_kernel,
        grid=(t // tm,),
        in_specs=[
            pl.BlockSpec((tm, da), lambda i: (i, 0)),
            pl.BlockSpec((tm, ds), lambda i: (i, 0)),
            pl.BlockSpec((da, d), lambda i: (0, 0)),
            pl.BlockSpec((ds, d), lambda i: (1, 0)),
            pl.BlockSpec((tm, d), lambda i: (i, 0)),
            pl.BlockSpec((1, d), lambda i: (0, 0)),
        ],
        out_specs=pl.BlockSpec((tm, d), lambda i: (i, 0)),
        out_shape=jax.ShapeDtypeStruct((t, d), F32),
        compiler_params=_params(("parallel",)),
        name="out_proj",
    )(a, s, w, w, x, g)


def kernel(x, ffn1_pre_g, ffn1_w_gate, ffn1_w_up, ffn1_w_down, ffn1_post_g, mix_pre_g, w_in, lambda_q1, lambda_k1, lambda_q2, lambda_k2, subln_g, rel_bias, ssm_a_re, ssm_a_im, ssm_b_re, ssm_b_im, ssm_c_re, ssm_c_im, ssm_d, ssm_log_dt, w_glu, b_glu, w_out, mix_post_g, ffn2_pre_g, ffn2_w_gate, ffn2_w_up, ffn2_w_down, ffn2_post_g):
    batch, seq, d_model = x.shape
    depth = ffn1_pre_g.shape[0]
    assert depth == 1, "LAMBDA_INIT is specialised to a single layer"
    n_heads = rel_bias.shape[1]
    d_attn = n_heads * V_HEAD_DIM
    n_groups, n_state = ssm_a_re.shape[1:]
    d_ssm = n_groups * SSM_GROUP
    n_seg = V7X_SUBLANES
    seg_len = seq // n_seg
    tokens = batch * seq
    tm, tf, tq, tb = 512, 512, 256, 32
    tm_first = 2 * tm

    xt = x.reshape(tokens, d_model)
    row = lambda v: v.reshape(1, -1)
    l = 0

    y_first, wg1, wu1, wd1, w_in_bf, w_glu_bf = _ffn_first(
        xt, row(ffn1_pre_g[l]), ffn1_w_gate[l], ffn1_w_up[l], ffn1_w_down[l], row(ffn1_post_g[l]),
        tm=tm_first, tf=tf // 2, cast=(w_in[l], w_glu[l]))
    x1 = _ffn(xt, row(ffn1_pre_g[l]), wg1, wu1, wd1, row(ffn1_post_g[l]), tm=tm, tf=tf, y_first=y_first,
              rows_done=tm_first)

    qkv, u = _in_proj(x1, row(mix_pre_g[l]), w_in_bf, tm=tm, q_scale=ATTN_HEAD_DIM ** -0.5 * LOG2_E)

    band = _bias_band(rel_bias, tq=tq, scale=LOG2_E)
    a, wg2, wu2, wd2, wo = _attention(
        qkv, band, row(lambda_q1[l]), row(lambda_k1[l]), row(lambda_q2[l]), row(lambda_k2[l]),
        subln_g[l].reshape(-1, 1), batch=batch, seq=seq, n_heads=n_heads, tq=tq,
        cast=(ffn2_w_gate[l], ffn2_w_up[l], ffn2_w_down[l], w_out[l]))

    ab_re, ab_im, bmat_re, bmat_im, cmat_re, cmat_im = _ssm_params(
        ssm_a_re[l].reshape(n_groups, 1, n_state), ssm_a_im[l].reshape(n_groups, 1, n_state),
        ssm_log_dt[l].reshape(n_groups, 1, 1),
        jnp.swapaxes(ssm_b_re[l], 1, 2), jnp.swapaxes(ssm_b_im[l], 1, 2), ssm_c_re[l], ssm_c_im[l])
    u_seg = u.reshape(batch, n_seg, seg_len, d_ssm)
    s_seg = _ssm(u_seg, bmat_re, bmat_im, ab_re.reshape(1, -1), ab_im.reshape(1, -1), cmat_re, cmat_im,
                 row(ssm_d[l]), w_glu_bf, row(b_glu[l]), tb=tb)
    s = s_seg.reshape(tokens, d_ssm)

    x2 = _out_proj(a.reshape(tokens, d_attn), s, wo, x1, row(mix_post_g[l]), tm=tm)

    x3 = _ffn(x2, row(ffn2_pre_g[l]), wg2, wu2, wd2, row(ffn2_post_g[l]), tm=tm, tf=tf)
    return x3.reshape(batch, seq, d_model)
```

```python
import functools
import math

import jax
import jax.numpy as jnp
from jax import lax
from jax.experimental import pallas as pl
from jax.experimental.pallas import tpu as pltpu

V7X_LANES = 128
V7X_SUBLANES = 8
V7X_MXU_COLS = 256
V7X_VMEM_LIMIT_BYTES = 56 * 1024 * 1024

RMS_EPS = 1e-6
NEG_INF = -1e30
N_BUCKETS = 32
MAX_DISTANCE = 128
ATTN_HEAD_DIM = 64
V_HEAD_DIM = 128
SSM_GROUP = 16
SSM_STATE = 64
LAMBDA_INIT = 0.8 - 0.6 * math.exp(-0.3 * 0)

F32 = jnp.float32
BF16 = jnp.bfloat16
EPILOGUE_ROWS = 128


def _params(semantics):
    return pltpu.CompilerParams(dimension_semantics=semantics,
                                vmem_limit_bytes=V7X_VMEM_LIMIT_BYTES)


def _rms_scale(v):
    return lax.rsqrt(jnp.mean(v * v, axis=-1, keepdims=True) + RMS_EPS)


def _ffn_prologue(x_ref, pre_g_ref, xn_ref, acc_ref):
    xf = x_ref[...]
    xn_ref[...] = (xf * _rms_scale(xf) * pre_g_ref[...]).astype(BF16)
    acc_ref[...] = jnp.zeros_like(acc_ref)


def _ffn_chunk(xn_ref, acc_ref, wg_ref, wu_ref, wd_ref):
    xn = xn_ref[...]
    n_split = 2 if wg_ref.shape[1] >= 2 * V7X_MXU_COLS else 1
    half = wg_ref.shape[1] // n_split
    pre = []
    for c in range(n_split):
        cols = slice(c * half, (c + 1) * half)
        pre.append((jnp.dot(xn, wg_ref[:, cols], preferred_element_type=F32),
                    jnp.dot(xn, wu_ref[:, cols], preferred_element_type=F32)))
    out = acc_ref[...]
    for c, (gate, up) in enumerate(pre):
        hidden = (gate * jax.nn.sigmoid(gate) * up).astype(BF16)
        out = out + jnp.dot(hidden, wd_ref[c * half:(c + 1) * half, :], preferred_element_type=F32)
    acc_ref[...] = out


def _ffn_epilogue(x_ref, acc_ref, post_g_ref, o_ref):
    scale = _rms_scale(acc_ref[...])
    g_half = 0.5 * post_g_ref[...]
    for r in range(0, acc_ref.shape[0], EPILOGUE_ROWS):
        rows = slice(r, r + EPILOGUE_ROWS)
        o_ref[rows, :] = x_ref[rows, :] + acc_ref[rows, :] * scale[rows] * g_half


def _ffn_kernel(x_ref, pre_g_ref, wg_ref, wu_ref, wd_ref, post_g_ref, o_ref, xn_ref, acc_ref):
    k = pl.program_id(1)

    @pl.when(k == 0)
    def _():
        _ffn_prologue(x_ref, pre_g_ref, xn_ref, acc_ref)

    _ffn_chunk(xn_ref, acc_ref, wg_ref, wu_ref, wd_ref)

    @pl.when(k == pl.num_programs(1) - 1)
    def _():
        _ffn_epilogue(x_ref, acc_ref, post_g_ref, o_ref)


def _ffn_first_kernel(x_ref, pre_g_ref, wg32_ref, wu32_ref, wd32_ref, post_g_ref, *rest, cast_steps):
    n_cast = len(cast_steps)
    cast_in = rest[:n_cast]
    o_ref, wg_ref, wu_ref, wd_ref = rest[n_cast:n_cast + 4]
    cast_out = rest[n_cast + 4:2 * n_cast + 4]
    xn_ref, acc_ref = rest[2 * n_cast + 4:]
    k = pl.program_id(0)

    @pl.when(k == 0)
    def _():
        _ffn_prologue(x_ref, pre_g_ref, xn_ref, acc_ref)

    for src, dst, steps in zip(cast_in, cast_out, cast_steps):
        @pl.when(k < steps)
        def _(src=src, dst=dst):
            dst[...] = src[...].astype(BF16)

    wg = wg32_ref[...].astype(BF16)
    wu = wu32_ref[...].astype(BF16)
    wd = wd32_ref[...].astype(BF16)
    wg_ref[...] = wg
    wu_ref[...] = wu
    wd_ref[...] = wd
    _ffn_chunk(xn_ref, acc_ref, wg, wu, wd)

    @pl.when(k == pl.num_programs(0) - 1)
    def _():
        _ffn_epilogue(x_ref, acc_ref, post_g_ref, o_ref)


def _ffn_first(x, pre_g, wg32, wu32, wd32, post_g, *, tm, tf, cast=()):
    t, d = x.shape
    f = wg32.shape[1]
    n_chunks = f // tf
    cast_steps = []
    for w in cast:
        steps = 1
        while steps * 2 <= n_chunks and w.shape[0] % (16 * steps * 2) == 0:
            steps *= 2
        cast_steps.append(steps)
    cast_specs = [pl.BlockSpec((w.shape[0] // steps, w.shape[1]),
                               lambda k, steps=steps: (jnp.minimum(k, steps - 1), 0))
                  for w, steps in zip(cast, cast_steps)]
    w_specs = [pl.BlockSpec((d, tf), lambda k: (0, k)), pl.BlockSpec((d, tf), lambda k: (0, k)),
               pl.BlockSpec((tf, d), lambda k: (k, 0))]
    return pl.pallas_call(
        functools.partial(_ffn_first_kernel, cast_steps=tuple(cast_steps)),
        grid=(n_chunks,),
        in_specs=[pl.BlockSpec((tm, d), lambda k: (0, 0), pipeline_mode=pl.Buffered(1)),
                  pl.BlockSpec((1, d), lambda k: (0, 0))]
                 + w_specs + [pl.BlockSpec((1, d), lambda k: (0, 0))] + cast_specs,
        out_specs=[pl.BlockSpec((tm, d), lambda k: (0, 0), pipeline_mode=pl.Buffered(1))] + w_specs + cast_specs,
        out_shape=[jax.ShapeDtypeStruct((tm, d), F32)]
                  + [jax.ShapeDtypeStruct(w.shape, BF16) for w in (wg32, wu32, wd32) + tuple(cast)],
        scratch_shapes=[pltpu.VMEM((tm, d), BF16), pltpu.VMEM((tm, d), F32)],
        compiler_params=_params(("arbitrary",)),
        name="ffn_first",
    )(x, pre_g, wg32, wu32, wd32, post_g, *cast)


def _ffn(x, pre_g, wg, wu, wd, post_g, *, tm, tf, rows_done=0):
    t, d = x.shape
    f = wg.shape[1]
    assert rows_done % tm == 0
    skip = rows_done // tm
    return pl.pallas_call(
        _ffn_kernel,
        grid=(t // tm - skip, f // tf),
        in_specs=[
            pl.BlockSpec((tm, d), lambda i, k: (i + skip, 0)),
            pl.BlockSpec((1, d), lambda i, k: (0, 0)),
            pl.BlockSpec((d, tf), lambda i, k: (0, k)),
            pl.BlockSpec((d, tf), lambda i, k: (0, k)),
            pl.BlockSpec((tf, d), lambda i, k: (k, 0)),
            pl.BlockSpec((1, d), lambda i, k: (0, 0)),
        ],
        out_specs=pl.BlockSpec((tm, d), lambda i, k: (i, 0)),
        out_shape=jax.ShapeDtypeStruct((t - rows_done, d), F32),
        scratch_shapes=[pltpu.VMEM((tm, d), BF16), pltpu.VMEM((tm, d), F32)],
        compiler_params=_params(("parallel", "arbitrary")),
        name="ffn",
    )(x, pre_g, wg, wu, wd, post_g)


def _row_tile_specs(parts, tm):
    head, tail = parts
    n_head = head.shape[0] // tm
    assert head.shape[0] % tm == 0 and tail.shape[0] % tm == 0 and head.shape[1] == tail.shape[1]
    d = head.shape[1]
    specs = [pl.BlockSpec((tm, d), lambda i: (jnp.minimum(i, n_head - 1), 0)),
             pl.BlockSpec((tm, d), lambda i: (jnp.maximum(i - n_head, 0), 0))]
    return specs, n_head


def _row_tile(head_ref, tail_ref, n_head):
    return jnp.where(pl.program_id(0) < n_head, head_ref[...], tail_ref[...])


def _in_proj_kernel(xh_ref, xt_ref, g_ref, w_ref, qkv_ref, u_ref, *, q_scale, n_head):
    xf = _row_tile(xh_ref, xt_ref, n_head)
    hn = (xf * _rms_scale(xf) * g_ref[...]).astype(BF16)
    tn = u_ref.shape[1]
    for n in range(4):
        acc = jnp.dot(hn, w_ref[:, n * tn:(n + 1) * tn], preferred_element_type=F32)
        if n == 0:
            qkv_ref[:, :tn] = (acc * q_scale).astype(BF16)
        elif n < 3:
            qkv_ref[:, n * tn:(n + 1) * tn] = acc.astype(BF16)
        else:
            u_ref[...] = acc


def _in_proj(x_parts, g, w, *, tm, q_scale):
    t = x_parts[0].shape[0] + x_parts[1].shape[0]
    d = x_parts[0].shape[1]
    tn = w.shape[1] // 4
    x_specs, n_head = _row_tile_specs(x_parts, tm)
    return pl.pallas_call(
        functools.partial(_in_proj_kernel, q_scale=q_scale, n_head=n_head),
        grid=(t // tm,),
        in_specs=x_specs + [
            pl.BlockSpec((1, d), lambda i: (0, 0)),
            pl.BlockSpec(w.shape, lambda i: (0, 0), pipeline_mode=pl.Buffered(1)),
        ],
        out_specs=[
            pl.BlockSpec((tm, 3 * tn), lambda i: (i, 0)),
            pl.BlockSpec((tm, tn), lambda i: (i, 0)),
        ],
        out_shape=[jax.ShapeDtypeStruct((t, 3 * tn), BF16),
                   jax.ShapeDtypeStruct((t, tn), F32)],
        compiler_params=_params(("parallel",)),
        name="in_proj",
    )(*x_parts, g, w)


def _bias_band_kernel(rb_ref, o_ref, *, tq, scale):
    h = pl.program_id(0)
    key = lax.broadcasted_iota(jnp.int32, (3 * tq, tq), 0)
    qry = lax.broadcasted_iota(jnp.int32, (3 * tq, tq), 1)
    n = jnp.maximum(2 * tq + qry - key, 0)
    max_exact = N_BUCKETS // 2
    nf = jnp.maximum(n, 1).astype(F32)
    large = max_exact + (jnp.log(nf / max_exact) / math.log(MAX_DISTANCE / max_exact)
                         * (N_BUCKETS - max_exact)).astype(jnp.int32)
    large = jnp.minimum(large, N_BUCKETS - 1)
    bucket = jnp.where(n < max_exact, n, large)
    val = jnp.zeros((3 * tq, tq), F32)
    for b in range(N_BUCKETS):
        val = jnp.where(bucket == b, rb_ref[b, h], val)
    o_ref[0] = val * scale


def _bias_band(rel_bias, *, tq, scale):
    assert tq >= MAX_DISTANCE
    n_heads = rel_bias.shape[1]
    return pl.pallas_call(
        functools.partial(_bias_band_kernel, tq=tq, scale=scale),
        grid=(n_heads,),
        in_specs=[pl.BlockSpec(memory_space=pltpu.SMEM)],
        out_specs=pl.BlockSpec((1, 3 * tq, tq), lambda h: (h, 0, 0)),
        out_shape=jax.ShapeDtypeStruct((n_heads, 3 * tq, tq), F32),
        compiler_params=_params(("arbitrary",)),
        name="bias_band",
    )(rel_bias)


N_STREAMS = 4
VT_ROWS = V_HEAD_DIM + 16
LOG2_E = math.log2(math.e)


def _attn_kernel(lq1_ref, lk1_ref, lq2_ref, lk2_ref, q1_ref, q2_ref, k1_ref, k2_ref,
                 v_ref, band_ref, g_ref, *rest, tq, cast_steps):
    n_cast = len(cast_steps)
    cast_in = rest[:n_cast]
    o_ref = rest[n_cast]
    cast_out = rest[n_cast + 1:2 * n_cast + 1]
    vt_ref, qm_ref, s_ref, m_ref, acc_ref = rest[2 * n_cast + 1:]
    i = pl.program_id(2)
    step_id = (pl.program_id(0) * pl.num_programs(1) + pl.program_id(1)) * pl.num_programs(2) + i
    for src, dst, steps in zip(cast_in, cast_out, cast_steps):
        @pl.when(step_id < steps)
        def _(src=src, dst=dst):
            dst[...] = src[...].astype(BF16)

    n_chunks = v_ref.shape[1] // tq
    q_refs = (q1_ref, q2_ref)
    k_refs = (k1_ref, k2_ref)
    streams = [(hh, mi) for hh in range(2) for mi in range(2)]

    @pl.when(i == 0)
    def _():
        ones_pad = (lax.broadcasted_iota(jnp.int32, (VT_ROWS - V_HEAD_DIM, tq), 0) == 0).astype(BF16)
        for c in range(n_chunks):
            vt = v_ref[0, c * tq:(c + 1) * tq, :].astype(F32).T.astype(BF16)
            for hh in range(2):
                vt_ref[c, hh, :V_HEAD_DIM, :] = vt[hh * V_HEAD_DIM:(hh + 1) * V_HEAD_DIM]
                vt_ref[c, hh, V_HEAD_DIM:, :] = ones_pad

    feat = lax.broadcasted_iota(jnp.int32, (2 * ATTN_HEAD_DIM, tq), 0)
    for mi in range(2):
        qt = q_refs[mi][0].astype(F32).T
        for hh in range(2):
            qm_ref[2 * hh + mi] = jnp.where((feat >= ATTN_HEAD_DIM) == (hh == 1), qt, 0.0).astype(BF16)
    m_ref[...] = jnp.full(m_ref.shape, NEG_INF, F32)
    acc_ref[...] = jnp.zeros(acc_ref.shape, F32)

    def score_phase(j, buf, diagonal):
        rows = pl.ds(pl.multiple_of(j * tq, tq), tq)
        if diagonal:
            band_rows = pl.ds(2 * tq, tq)
            key = lax.broadcasted_iota(jnp.int32, (tq, tq), 0)
            qry = lax.broadcasted_iota(jnp.int32, (tq, tq), 1)
            mask = key <= qry
        else:
            band_rows = pl.ds(pl.multiple_of(jnp.where(j == i - 1, tq, 0), tq), tq)
        for si, (hh, mi) in enumerate(streams):
            s = jnp.dot(k_refs[mi][0, rows, :], qm_ref[si],
                        preferred_element_type=F32) + band_ref[hh, band_rows, :]
            if diagonal:
                s = jnp.where(mask, s, NEG_INF)
            s_ref[buf, si] = s

    def value_phase(j, buf):
        for si, (hh, mi) in enumerate(streams):
            s = s_ref[buf, si]
            m_old = m_ref[si]
            m_new = jnp.maximum(m_old, jnp.max(s, axis=0, keepdims=True))
            alpha = jnp.exp2(m_old - m_new)
            p = jnp.exp2(s - m_new)
            acc_ref[si] = alpha * acc_ref[si] + jnp.dot(vt_ref[j, hh], p.astype(BF16),
                                                        preferred_element_type=F32)
            m_ref[si] = m_new

    @pl.when(i == 0)
    def _():
        score_phase(0, 0, True)

    @pl.when(i > 0)
    def _():
        score_phase(0, 0, False)

    def step(j, parity, diagonal):
        score_phase(j + 1, 1 - parity, diagonal)
        value_phase(j, parity)

    n = i - 1

    def pair_body(jj, carry):
        step(2 * jj, 0, False)
        step(2 * jj + 1, 1, False)
        return carry

    lax.fori_loop(0, jnp.maximum(n, 0) // 2, pair_body, 0)

    @pl.when(jnp.logical_and(i > 0, lax.rem(n, 2) == 1))
    def _():
        step(n - 1, 0, False)
        step(n, 1, True)

    @pl.when(jnp.logical_and(i > 0, lax.rem(n, 2) == 0))
    def _():
        step(n, 0, True)

    value_phase(i, lax.rem(i, 2))

    lam = (jnp.exp(jnp.sum(lq1_ref[...] * lk1_ref[...], axis=-1, keepdims=True))
           - jnp.exp(jnp.sum(lq2_ref[...] * lk2_ref[...], axis=-1, keepdims=True)) + LAMBDA_INIT)
    for hh in range(2):
        s1, s2 = 2 * hh, 2 * hh + 1
        den = V_HEAD_DIM
        a = (acc_ref[s1, :den] / acc_ref[s1, den:den + 1]
             - lam * (acc_ref[s2, :den] / acc_ref[s2, den:den + 1]))
        scale = lax.rsqrt(jnp.mean(a * a, axis=0, keepdims=True) + RMS_EPS)
        a = a * scale * g_ref[...] * (1.0 - LAMBDA_INIT)
        o_ref[0, :, hh * V_HEAD_DIM:(hh + 1) * V_HEAD_DIM] = a.T.astype(o_ref.dtype)


def _attention(qkv, band, lq1, lk1, lq2, lk2, subln_g, *, batch, seq, n_heads, tq, cast=()):
    d_attn = n_heads * V_HEAD_DIM
    pair = 2 * ATTN_HEAD_DIM
    vpair = 2 * V_HEAD_DIM
    n_pairs = d_attn // 2 // pair
    nq = seq // tq
    n_steps = batch * n_pairs * nq
    cast_steps = []
    for w in cast:
        steps = n_steps
        while w.shape[0] % (16 * steps):
            steps //= 2
        cast_steps.append(steps)

    def cast_spec(w, steps):
        return pl.BlockSpec((w.shape[0] // steps, w.shape[1]),
                            lambda b, p, i: (jnp.minimum((b * n_pairs + p) * nq + i, steps - 1), 0))

    cast_specs = [cast_spec(w, steps) for w, steps in zip(cast, cast_steps)]
    qkv3 = qkv.reshape(batch, seq, 3 * d_attn)
    small = pl.BlockSpec((1, ATTN_HEAD_DIM), lambda b, p, i: (0, 0))
    return pl.pallas_call(
        functools.partial(_attn_kernel, tq=tq, cast_steps=tuple(cast_steps)),
        grid=(batch, n_pairs, nq),
        in_specs=[
            small, small, small, small,
            pl.BlockSpec((1, tq, pair), lambda b, p, i: (b, i, p)),
            pl.BlockSpec((1, tq, pair), lambda b, p, i: (b, i, n_pairs + p)),
            pl.BlockSpec((1, seq, pair), lambda b, p, i: (b, 0, 2 * n_pairs + p)),
            pl.BlockSpec((1, seq, pair), lambda b, p, i: (b, 0, 3 * n_pairs + p)),
            pl.BlockSpec((1, seq, vpair), lambda b, p, i: (b, 0, 2 * n_pairs + p)),
            pl.BlockSpec((2, 3 * tq, tq), lambda b, p, i: (p, 0, 0)),
            pl.BlockSpec((V_HEAD_DIM, 1), lambda b, p, i: (0, 0)),
        ] + cast_specs,
        out_specs=[pl.BlockSpec((1, tq, vpair), lambda b, p, i: (b, i, p))] + cast_specs,
        out_shape=[jax.ShapeDtypeStruct((batch, seq, d_attn), BF16)]
                  + [jax.ShapeDtypeStruct(w.shape, BF16) for w in cast],
        scratch_shapes=[pltpu.VMEM((seq // tq, 2, VT_ROWS, tq), BF16),
                        pltpu.VMEM((N_STREAMS, pair, tq), BF16),
                        pltpu.VMEM((2, N_STREAMS, tq, tq), F32),
                        pltpu.VMEM((N_STREAMS, 1, tq), F32),
                        pltpu.VMEM((N_STREAMS, VT_ROWS, tq), F32)],
        compiler_params=_params(("arbitrary", "arbitrary", "arbitrary")),
        name="diff_attention",
    )(lq1, lk1, lq2, lk2, qkv3, qkv3, qkv3, qkv3, qkv3, band, subln_g, *cast)


def _block_diag_blocks(w, dst_ref):
    n_blocks, rows, cols = dst_ref.shape
    g, p, n = w.shape
    gl = g // n_blocks
    row_group = lax.broadcasted_iota(jnp.int32, (rows, cols), 0) // p
    col_group = lax.broadcasted_iota(jnp.int32, (rows, cols), 1) // n
    for j in range(n_blocks):
        stacked = w[j * gl:(j + 1) * gl].reshape(rows, n)
        tiled = jnp.concatenate([stacked] * gl, axis=1)
        dst_ref[j] = jnp.where(row_group == col_group, tiled, 0.0).astype(BF16)


def _ssm_params_kernel(a_re_ref, a_im_ref, log_dt_ref, b_re_ref, b_im_ref, c_re_ref, c_im_ref,
                       ab_re_ref, ab_im_ref, bd_re_ref, bd_im_ref, cd_re_ref, cd_im_ref):
    ar = a_re_ref[...]
    ai = a_im_ref[...]
    dt = jnp.exp(log_dt_ref[...])
    decay = jnp.exp(dt * ar)
    ab_re = decay * jnp.cos(dt * ai)
    ab_im = decay * jnp.sin(dt * ai)
    den = ar * ar + ai * ai
    nr = ab_re - 1.0
    ni = ab_im
    coef_re = (nr * ar + ni * ai) / den
    coef_im = (ni * ar - nr * ai) / den
    br = b_re_ref[...]
    bi = b_im_ref[...]
    ab_re_ref[...] = ab_re
    ab_im_ref[...] = ab_im
    _block_diag_blocks(coef_re * br - coef_im * bi, bd_re_ref)
    _block_diag_blocks(coef_re * bi + coef_im * br, bd_im_ref)
    _block_diag_blocks(c_re_ref[...], cd_re_ref)
    _block_diag_blocks(c_im_ref[...], cd_im_ref)


def _ssm_params(a_re, a_im, log_dt, b_re_t, b_im_t, c_re, c_im):
    g, _, n = a_re.shape
    p = b_re_t.shape[1]
    gl = g // SSM_BLOCKS
    dense = jax.ShapeDtypeStruct((SSM_BLOCKS, gl * p, gl * n), BF16)
    return pl.pallas_call(
        _ssm_params_kernel,
        out_shape=[jax.ShapeDtypeStruct((g, 1, n), F32), jax.ShapeDtypeStruct((g, 1, n), F32),
                   dense, dense, dense, dense],
        compiler_params=pltpu.CompilerParams(vmem_limit_bytes=V7X_VMEM_LIMIT_BYTES),
        name="ssm_params",
    )(a_re, a_im, log_dt, b_re_t, b_im_t, c_re, c_im)


SSM_COL_CHUNK = 512
SSM_BLOCKS = 4


def _ssm_input_drive(u_ref, lhs_ref, bmat_re_ref, bmat_im_ref, bu_re_ref, bu_im_ref, *, tb):
    n_seg = u_ref.shape[1]
    n_slabs = lhs_ref.shape[0]
    for seg in range(n_seg):
        for c in range(n_slabs):
            lhs_ref[c, pl.ds(seg, tb, stride=n_seg), :] = u_ref[0, seg, :, c * V7X_LANES:(c + 1) * V7X_LANES]
    per_block = n_slabs // SSM_BLOCKS
    kst = bu_re_ref.shape[1] // SSM_BLOCKS
    for j in range(SSM_BLOCKS):
        uj = jnp.concatenate([lhs_ref[c] for c in range(j * per_block, (j + 1) * per_block)],
                             axis=-1).astype(BF16)
        bu_re_ref[:, j * kst:(j + 1) * kst] = jnp.dot(uj, bmat_re_ref[j], preferred_element_type=F32)
        bu_im_ref[:, j * kst:(j + 1) * kst] = jnp.dot(uj, bmat_im_ref[j], preferred_element_type=F32)


def _ssm_recurrence(ab_re_ref, ab_im_ref, st_re_ref, st_im_ref, bu_re_ref, bu_im_ref, *, tb,
                    x_re_ref=None, x_im_ref=None):
    n_cols = st_re_ref.shape[1]
    two = 2 * V7X_SUBLANES
    for c in range(n_cols // SSM_COL_CHUNK):
        cs = slice(c * SSM_COL_CHUNK, (c + 1) * SSM_COL_CHUNK)
        ar = jnp.broadcast_to(ab_re_ref[:, cs], (V7X_SUBLANES, SSM_COL_CHUNK))
        ai = jnp.broadcast_to(ab_im_ref[:, cs], (V7X_SUBLANES, SSM_COL_CHUNK))

        def step(t2, carry, cs=cs, ar=ar, ai=ai):
            xr, xi = carry
            rows = pl.ds(pl.multiple_of(t2 * two, two), two)
            bur = bu_re_ref[rows, cs]
            bui = bu_im_ref[rows, cs]
            xr1 = ar * xr - ai * xi + bur[:V7X_SUBLANES]
            xi1 = ar * xi + ai * xr + bui[:V7X_SUBLANES]
            xr2 = ar * xr1 - ai * xi1 + bur[V7X_SUBLANES:]
            xi2 = ar * xi1 + ai * xr1 + bui[V7X_SUBLANES:]
            if x_re_ref is not None:
                x_re_ref[rows, cs] = jnp.concatenate([xr1, xr2], axis=0).astype(BF16)
                x_im_ref[rows, cs] = jnp.concatenate([xi1, xi2], axis=0).astype(BF16)
            return xr2, xi2

        xr, xi = lax.fori_loop(0, tb // 2, step, (st_re_ref[:, cs], st_im_ref[:, cs]), unroll=2)
        st_re_ref[:, cs] = xr
        st_im_ref[:, cs] = xi


def _ssm_ends_kernel(u_ref, bmat_re_ref, bmat_im_ref, ab_re_ref, ab_im_ref, end_re_ref, end_im_ref,
                     lhs_ref, bu_re_ref, bu_im_ref, st_re_ref, st_im_ref, *, tb):
    tblk = pl.program_id(1)

    @pl.when(tblk == 0)
    def _():
        st_re_ref[...] = jnp.zeros_like(st_re_ref)
        st_im_ref[...] = jnp.zeros_like(st_im_ref)

    _ssm_input_drive(u_ref, lhs_ref, bmat_re_ref, bmat_im_ref, bu_re_ref, bu_im_ref, tb=tb)
    _ssm_recurrence(ab_re_ref, ab_im_ref, st_re_ref, st_im_ref, bu_re_ref, bu_im_ref, tb=tb)

    @pl.when(tblk == pl.num_programs(1) - 1)
    def _():
        end_re_ref[0] = st_re_ref[...]
        end_im_ref[0] = st_im_ref[...]


def _ssm_main_kernel(u_ref, end_re_ref, end_im_ref, bmat_re_ref, bmat_im_ref, ab_re_ref, ab_im_ref,
                     cmat_re_ref, cmat_im_ref, d_ref, wglu_ref, bglu_ref, o_ref,
                     lhs_ref, bu_re_ref, bu_im_ref, st_re_ref, st_im_ref, x_re_ref, x_im_ref, *, tb, seg_len):
    tblk = pl.program_id(1)
    n_seg = st_re_ref.shape[0]

    @pl.when(tblk == 0)
    def _():
        pr = ab_re_ref[...]
        pi = ab_im_ref[...]
        for _ in range(int(math.log2(seg_len))):
            pr, pi = pr * pr - pi * pi, 2.0 * pr * pi
        er = jnp.zeros_like(pr)
        ei = jnp.zeros_like(pi)
        st_re_ref[0:1, :] = er
        st_im_ref[0:1, :] = ei
        for j in range(1, n_seg):
            lr = end_re_ref[0, j - 1:j, :]
            li = end_im_ref[0, j - 1:j, :]
            er, ei = lr + pr * er - pi * ei, li + pr * ei + pi * er
            st_re_ref[j:j + 1, :] = er
            st_im_ref[j:j + 1, :] = ei

    _ssm_input_drive(u_ref, lhs_ref, bmat_re_ref, bmat_im_ref, bu_re_ref, bu_im_ref, tb=tb)
    _ssm_recurrence(ab_re_ref, ab_im_ref, st_re_ref, st_im_ref, bu_re_ref, bu_im_ref, tb=tb,
                    x_re_ref=x_re_ref, x_im_ref=x_im_ref)

    n_slabs = lhs_ref.shape[0]
    kst = bu_re_ref.shape[1] // SSM_BLOCKS
    ys = []
    for j in range(SSM_BLOCKS):
        xr = x_re_ref[:, j * kst:(j + 1) * kst]
        xi = x_im_ref[:, j * kst:(j + 1) * kst]
        nt = (((1,), (1,)), ((), ()))
        ys.append(lax.dot_general(xr, cmat_re_ref[j], nt, preferred_element_type=F32)
                  - lax.dot_general(xi, cmat_im_ref[j], nt, preferred_element_type=F32))
    u_rows = jnp.concatenate([lhs_ref[c] for c in range(n_slabs)], axis=-1)
    y = jnp.concatenate(ys, axis=-1) + d_ref[...] * u_rows
    z = jax.nn.gelu(y)
    gate = jnp.dot(z.astype(BF16), wglu_ref[...], preferred_element_type=F32) + bglu_ref[...]
    s = z * jax.nn.sigmoid(gate)
    for c in range(n_slabs):
        lhs_ref[c] = s[:, c * V7X_LANES:(c + 1) * V7X_LANES]
    for seg in range(n_seg):
        for c in range(n_slabs):
            o_ref[0, seg, :, c * V7X_LANES:(c + 1) * V7X_LANES] = lhs_ref[c, pl.ds(seg, tb, stride=n_seg), :]


def _ssm(u_seg, bmat_re, bmat_im, ab_re, ab_im, cmat_re, cmat_im, d_skip, w_glu, b_glu, *, tb):
    batch, n_seg, seg_len, d_ssm = u_seg.shape
    n_cols = ab_re.shape[1]
    rows = tb * n_seg
    grid = (batch, seg_len // tb)
    u_spec = pl.BlockSpec((1, n_seg, tb, d_ssm), lambda b, t: (b, 0, t, 0))
    end_spec = pl.BlockSpec((1, n_seg, n_cols), lambda b, t: (b, 0, 0))

    def whole(a):
        return pl.BlockSpec(a.shape, lambda b, t: (0,) * a.ndim, pipeline_mode=pl.Buffered(1))

    scratch = [pltpu.VMEM((d_ssm // V7X_LANES, rows, V7X_LANES), F32), pltpu.VMEM((rows, n_cols), F32),
               pltpu.VMEM((rows, n_cols), F32), pltpu.VMEM((n_seg, n_cols), F32),
               pltpu.VMEM((n_seg, n_cols), F32)]
    end_re, end_im = pl.pallas_call(
        functools.partial(_ssm_ends_kernel, tb=tb),
        grid=grid,
        in_specs=[u_spec, whole(bmat_re), whole(bmat_im), whole(ab_re), whole(ab_im)],
        out_specs=[end_spec, end_spec],
        out_shape=[jax.ShapeDtypeStruct((batch, n_seg, n_cols), F32)] * 2,
        scratch_shapes=scratch,
        compiler_params=_params(("parallel", "arbitrary")),
        name="ssm_ends",
    )(u_seg, bmat_re, bmat_im, ab_re, ab_im)
    return pl.pallas_call(
        functools.partial(_ssm_main_kernel, tb=tb, seg_len=seg_len),
        grid=grid,
        in_specs=[u_spec, end_spec, end_spec, whole(bmat_re), whole(bmat_im), whole(ab_re), whole(ab_im),
                  whole(cmat_re), whole(cmat_im), whole(d_skip), whole(w_glu), whole(b_glu)],
        out_specs=u_spec,
        out_shape=jax.ShapeDtypeStruct(u_seg.shape, F32),
        scratch_shapes=scratch + [pltpu.VMEM((rows, n_cols), BF16), pltpu.VMEM((rows, n_cols), BF16)],
        compiler_params=_params(("parallel", "arbitrary")),
        name="ssm_main",
    )(u_seg, end_re, end_im, bmat_re, bmat_im, ab_re, ab_im, cmat_re, cmat_im, d_skip, w_glu, b_glu)


def _out_proj_kernel(a_ref, s_ref, wa_ref, ws_ref, xh_ref, xt_ref, g_ref, o_ref, *, n_head):
    m = (jnp.dot(a_ref[...], wa_ref[...], preferred_element_type=F32)
         + jnp.dot(s_ref[...].astype(BF16), ws_ref[...], preferred_element_type=F32))
    o_ref[...] = _row_tile(xh_ref, xt_ref, n_head) + m * _rms_scale(m) * g_ref[...]


def _out_proj(a, s, w, x_parts, g, *, tm):
    t = a.shape[0]
    d = w.shape[1]
    da = a.shape[1]
    ds = s.shape[1]
    assert da == ds and w.shape[0] == da + ds
    x_specs, n_head = _row_tile_specs(x_parts, tm)
    return pl.pallas_call(
        functools.partial(_out_proj_kernel, n_head=n_head),
        grid=(t // tm,),
        in_specs=[
            pl.BlockSpec((tm, da), lambda i: (i, 0)),
            pl.BlockSpec((tm, ds), lambda i: (i, 0)),
            pl.BlockSpec((da, d), lambda i: (0, 0)),
            pl.BlockSpec((ds, d), lambda i: (1, 0)),
        ] + x_specs + [
            pl.BlockSpec((1, d), lambda i: (0, 0)),
        ],
        out_specs=pl.BlockSpec((tm, d), lambda i: (i, 0)),
        out_shape=jax.ShapeDtypeStruct((t, d), F32),
        compiler_params=_params(("parallel",)),
        name="out_proj",
    )(a, s, w, w, *x_parts, g)


def kernel(x, ffn1_pre_g, ffn1_w_gate, ffn1_w_up, ffn1_w_down, ffn1_post_g, mix_pre_g, w_in, lambda_q1, lambda_k1, lambda_q2, lambda_k2, subln_g, rel_bias, ssm_a_re, ssm_a_im, ssm_b_re, ssm_b_im, ssm_c_re, ssm_c_im, ssm_d, ssm_log_dt, w_glu, b_glu, w_out, mix_post_g, ffn2_pre_g, ffn2_w_gate, ffn2_w_up, ffn2_w_down, ffn2_post_g):
    batch, seq, d_model = x.shape
    depth = ffn1_pre_g.shape[0]
    assert depth == 1, "LAMBDA_INIT is specialised to a single layer"
    n_heads = rel_bias.shape[1]
    d_attn = n_heads * V_HEAD_DIM
    n_groups, n_state = ssm_a_re.shape[1:]
    d_ssm = n_groups * SSM_GROUP
    n_seg = V7X_SUBLANES
    seg_len = seq // n_seg
    tokens = batch * seq
    tm, tf, tq, tb = 512, 512, 256, 32
    tm_first = 2 * tm

    xt = x.reshape(tokens, d_model)
    row = lambda v: v.reshape(1, -1)
    l = 0

    x1_head, wg1, wu1, wd1, w_in_bf, w_glu_bf = _ffn_first(
        xt, row(ffn1_pre_g[l]), ffn1_w_gate[l], ffn1_w_up[l], ffn1_w_down[l], row(ffn1_post_g[l]),
        tm=tm_first, tf=tf // 2, cast=(w_in[l], w_glu[l]))
    x1_tail = _ffn(xt, row(ffn1_pre_g[l]), wg1, wu1, wd1, row(ffn1_post_g[l]), tm=tm, tf=tf,
                   rows_done=tm_first)
    x1 = (x1_head, x1_tail)

    qkv, u = _in_proj(x1, row(mix_pre_g[l]), w_in_bf, tm=tm, q_scale=ATTN_HEAD_DIM ** -0.5 * LOG2_E)

    band = _bias_band(rel_bias, tq=tq, scale=LOG2_E)
    a, wg2, wu2, wd2, wo = _attention(
        qkv, band, row(lambda_q1[l]), row(lambda_k1[l]), row(lambda_q2[l]), row(lambda_k2[l]),
        subln_g[l].reshape(-1, 1), batch=batch, seq=seq, n_heads=n_heads, tq=tq,
        cast=(ffn2_w_gate[l], ffn2_w_up[l], ffn2_w_down[l], w_out[l]))

    ab_re, ab_im, bmat_re, bmat_im, cmat_re, cmat_im = _ssm_params(
        ssm_a_re[l].reshape(n_groups, 1, n_state), ssm_a_im[l].reshape(n_groups, 1, n_state),
        ssm_log_dt[l].reshape(n_groups, 1, 1),
        jnp.swapaxes(ssm_b_re[l], 1, 2), jnp.swapaxes(ssm_b_im[l], 1, 2), ssm_c_re[l], ssm_c_im[l])
    u_seg = u.reshape(batch, n_seg, seg_len, d_ssm)
    s_seg = _ssm(u_seg, bmat_re, bmat_im, ab_re.reshape(1, -1), ab_im.reshape(1, -1), cmat_re, cmat_im,
                 row(ssm_d[l]), w_glu_bf, row(b_glu[l]), tb=tb)
    s = s_seg.reshape(tokens, d_ssm)

    x2 = _out_proj(a.reshape(tokens, d_attn), s, wo, x1, row(mix_post_g[l]), tm=tm)

    x3 = _ffn(x2, row(ffn2_pre_g[l]), wg2, wu2, wd2, row(ffn2_post_g[l]), tm=tm, tf=tf)
    return x3.reshape(batch, seq, d_model)
```

```python
import functools
import math

import jax
import jax.numpy as jnp
from jax import lax
from jax.experimental import pallas as pl
from jax.experimental.pallas import tpu as pltpu

V7X_LANES = 128
V7X_SUBLANES = 8
V7X_MXU_COLS = 256
V7X_VMEM_LIMIT_BYTES = 56 * 1024 * 1024

RMS_EPS = 1e-6
NEG_INF = -1e30
N_BUCKETS = 32
MAX_DISTANCE = 128
ATTN_HEAD_DIM = 64
V_HEAD_DIM = 128
SSM_GROUP = 16
SSM_STATE = 64
LAMBDA_INIT = 0.8 - 0.6 * math.exp(-0.3 * 0)

F32 = jnp.float32
BF16 = jnp.bfloat16
EPILOGUE_ROWS = 128


def _params(semantics):
    return pltpu.CompilerParams(dimension_semantics=semantics,
                                vmem_limit_bytes=V7X_VMEM_LIMIT_BYTES)


def _rms_scale(v):
    return lax.rsqrt(jnp.mean(v * v, axis=-1, keepdims=True) + RMS_EPS)


def _ffn_prologue(x_ref, pre_g_ref, xn_ref, acc_ref):
    xf = x_ref[...]
    xn_ref[...] = (xf * _rms_scale(xf) * pre_g_ref[...]).astype(BF16)
    acc_ref[...] = jnp.zeros_like(acc_ref)


def _ffn_chunk(xn_ref, acc_ref, wg_ref, wu_ref, wd_ref):
    xn = xn_ref[...]
    n_split = 2 if wg_ref.shape[1] >= 2 * V7X_MXU_COLS else 1
    half = wg_ref.shape[1] // n_split
    pre = []
    for c in range(n_split):
        cols = slice(c * half, (c + 1) * half)
        pre.append((jnp.dot(xn, wg_ref[:, cols], preferred_element_type=F32),
                    jnp.dot(xn, wu_ref[:, cols], preferred_element_type=F32)))
    out = acc_ref[...]
    for c, (gate, up) in enumerate(pre):
        hidden = (gate * jax.nn.sigmoid(gate) * up).astype(BF16)
        out = out + jnp.dot(hidden, wd_ref[c * half:(c + 1) * half, :], preferred_element_type=F32)
    acc_ref[...] = out


def _ffn_epilogue(x_ref, acc_ref, post_g_ref, o_ref):
    scale = _rms_scale(acc_ref[...])
    g_half = 0.5 * post_g_ref[...]
    for r in range(0, acc_ref.shape[0], EPILOGUE_ROWS):
        rows = slice(r, r + EPILOGUE_ROWS)
        o_ref[rows, :] = x_ref[rows, :] + acc_ref[rows, :] * scale[rows] * g_half


def _ffn_kernel(x_ref, pre_g_ref, wg_ref, wu_ref, wd_ref, post_g_ref, o_ref, xn_ref, acc_ref):
    k = pl.program_id(1)

    @pl.when(k == 0)
    def _():
        _ffn_prologue(x_ref, pre_g_ref, xn_ref, acc_ref)

    _ffn_chunk(xn_ref, acc_ref, wg_ref, wu_ref, wd_ref)

    @pl.when(k == pl.num_programs(1) - 1)
    def _():
        _ffn_epilogue(x_ref, acc_ref, post_g_ref, o_ref)


def _ffn_first_kernel(x_ref, pre_g_ref, wg32_ref, wu32_ref, wd32_ref, post_g_ref, *rest, cast_steps):
    n_cast = len(cast_steps)
    cast_in = rest[:n_cast]
    o_ref, wg_ref, wu_ref, wd_ref = rest[n_cast:n_cast + 4]
    cast_out = rest[n_cast + 4:2 * n_cast + 4]
    xn_ref, acc_ref = rest[2 * n_cast + 4:]
    k = pl.program_id(0)

    @pl.when(k == 0)
    def _():
        _ffn_prologue(x_ref, pre_g_ref, xn_ref, acc_ref)

    for src, dst, steps in zip(cast_in, cast_out, cast_steps):
        @pl.when(k < steps)
        def _(src=src, dst=dst):
            dst[...] = src[...].astype(BF16)

    wg = wg32_ref[...].astype(BF16)
    wu = wu32_ref[...].astype(BF16)
    wd = wd32_ref[...].astype(BF16)
    wg_ref[...] = wg
    wu_ref[...] = wu
    wd_ref[...] = wd
    _ffn_chunk(xn_ref, acc_ref, wg, wu, wd)

    @pl.when(k == pl.num_programs(0) - 1)
    def _():
        _ffn_epilogue(x_ref, acc_ref, post_g_ref, o_ref)


def _ffn_first(x, pre_g, wg32, wu32, wd32, post_g, *, tm, tf, cast=()):
    t, d = x.shape
    f = wg32.shape[1]
    n_chunks = f // tf
    cast_steps = []
    for w in cast:
        steps = 1
        while steps * 2 <= n_chunks and w.shape[0] % (16 * steps * 2) == 0:
            steps *= 2
        cast_steps.append(steps)
    cast_specs = [pl.BlockSpec((w.shape[0] // steps, w.shape[1]),
                               lambda k, steps=steps: (jnp.minimum(k, steps - 1), 0))
                  for w, steps in zip(cast, cast_steps)]
    w_specs = [pl.BlockSpec((d, tf), lambda k: (0, k)), pl.BlockSpec((d, tf), lambda k: (0, k)),
               pl.BlockSpec((tf, d), lambda k: (k, 0))]
    return pl.pallas_call(
        functools.partial(_ffn_first_kernel, cast_steps=tuple(cast_steps)),
        grid=(n_chunks,),
        in_specs=[pl.BlockSpec((tm, d), lambda k: (0, 0), pipeline_mode=pl.Buffered(1)),
                  pl.BlockSpec((1, d), lambda k: (0, 0))]
                 + w_specs + [pl.BlockSpec((1, d), lambda k: (0, 0))] + cast_specs,
        out_specs=[pl.BlockSpec((tm, d), lambda k: (0, 0), pipeline_mode=pl.Buffered(1))] + w_specs + cast_specs,
        out_shape=[jax.ShapeDtypeStruct((tm, d), F32)]
                  + [jax.ShapeDtypeStruct(w.shape, BF16) for w in (wg32, wu32, wd32) + tuple(cast)],
        scratch_shapes=[pltpu.VMEM((tm, d), BF16), pltpu.VMEM((tm, d), F32)],
        compiler_params=_params(("arbitrary",)),
        name="ffn_first",
    )(x, pre_g, wg32, wu32, wd32, post_g, *cast)


def _ffn(x, pre_g, wg, wu, wd, post_g, *, tm, tf, rows_done=0):
    t, d = x.shape
    f = wg.shape[1]
    assert rows_done % tm == 0
    skip = rows_done // tm
    return pl.pallas_call(
        _ffn_kernel,
        grid=(t // tm - skip, f // tf),
        in_specs=[
            pl.BlockSpec((tm, d), lambda i, k: (i + skip, 0)),
            pl.BlockSpec((1, d), lambda i, k: (0, 0)),
            pl.BlockSpec((d, tf), lambda i, k: (0, k)),
            pl.BlockSpec((d, tf), lambda i, k: (0, k)),
            pl.BlockSpec((tf, d), lambda i, k: (k, 0)),
            pl.BlockSpec((1, d), lambda i, k: (0, 0)),
        ],
        out_specs=pl.BlockSpec((tm, d), lambda i, k: (i, 0)),
        out_shape=jax.ShapeDtypeStruct((t - rows_done, d), F32),
        scratch_shapes=[pltpu.VMEM((tm, d), BF16), pltpu.VMEM((tm, d), F32)],
        compiler_params=_params(("parallel", "arbitrary")),
        name="ffn",
    )(x, pre_g, wg, wu, wd, post_g)


def _row_tile_specs(parts, tm):
    head, tail = parts
    n_head = head.shape[0] // tm
    assert head.shape[0] % tm == 0 and tail.shape[0] % tm == 0 and head.shape[1] == tail.shape[1]
    d = head.shape[1]
    specs = [pl.BlockSpec((tm, d), lambda i: (jnp.minimum(i, n_head - 1), 0)),
             pl.BlockSpec((tm, d), lambda i: (jnp.maximum(i - n_head, 0), 0))]
    return specs, n_head


def _row_tile(head_ref, tail_ref, n_head):
    return jnp.where(pl.program_id(0) < n_head, head_ref[...], tail_ref[...])


def _in_proj_kernel(xh_ref, xt_ref, g_ref, w_ref, qkv_ref, u_ref, *, q_scale, n_head):
    xf = _row_tile(xh_ref, xt_ref, n_head)
    hn = (xf * _rms_scale(xf) * g_ref[...]).astype(BF16)
    tn = u_ref.shape[1]
    for n in range(4):
        acc = jnp.dot(hn, w_ref[:, n * tn:(n + 1) * tn], preferred_element_type=F32)
        if n == 0:
            qkv_ref[:, :tn] = (acc * q_scale).astype(BF16)
        elif n < 3:
            qkv_ref[:, n * tn:(n + 1) * tn] = acc.astype(BF16)
        else:
            u_ref[...] = acc


def _in_proj(x_parts, g, w, *, tm, q_scale):
    t = x_parts[0].shape[0] + x_parts[1].shape[0]
    d = x_parts[0].shape[1]
    tn = w.shape[1] // 4
    x_specs, n_head = _row_tile_specs(x_parts, tm)
    return pl.pallas_call(
        functools.partial(_in_proj_kernel, q_scale=q_scale, n_head=n_head),
        grid=(t // tm,),
        in_specs=x_specs + [
            pl.BlockSpec((1, d), lambda i: (0, 0)),
            pl.BlockSpec(w.shape, lambda i: (0, 0), pipeline_mode=pl.Buffered(1)),
        ],
        out_specs=[
            pl.BlockSpec((tm, 3 * tn), lambda i: (i, 0)),
            pl.BlockSpec((tm, tn), lambda i: (i, 0)),
        ],
        out_shape=[jax.ShapeDtypeStruct((t, 3 * tn), BF16),
                   jax.ShapeDtypeStruct((t, tn), F32)],
        compiler_params=_params(("parallel",)),
        name="in_proj",
    )(*x_parts, g, w)


def _bias_band_kernel(rb_ref, o_ref, *, tq, scale):
    h = pl.program_id(0)
    key = lax.broadcasted_iota(jnp.int32, (3 * tq, tq), 0)
    qry = lax.broadcasted_iota(jnp.int32, (3 * tq, tq), 1)
    n = jnp.maximum(2 * tq + qry - key, 0)
    max_exact = N_BUCKETS // 2
    nf = jnp.maximum(n, 1).astype(F32)
    large = max_exact + (jnp.log(nf / max_exact) / math.log(MAX_DISTANCE / max_exact)
                         * (N_BUCKETS - max_exact)).astype(jnp.int32)
    large = jnp.minimum(large, N_BUCKETS - 1)
    bucket = jnp.where(n < max_exact, n, large)
    val = jnp.zeros((3 * tq, tq), F32)
    for b in range(N_BUCKETS):
        val = jnp.where(bucket == b, rb_ref[b, h], val)
    o_ref[0] = val * scale


def _bias_band(rel_bias, *, tq, scale):
    assert tq >= MAX_DISTANCE
    n_heads = rel_bias.shape[1]
    return pl.pallas_call(
        functools.partial(_bias_band_kernel, tq=tq, scale=scale),
        grid=(n_heads,),
        in_specs=[pl.BlockSpec(memory_space=pltpu.SMEM)],
        out_specs=pl.BlockSpec((1, 3 * tq, tq), lambda h: (h, 0, 0)),
        out_shape=jax.ShapeDtypeStruct((n_heads, 3 * tq, tq), F32),
        compiler_params=_params(("arbitrary",)),
        name="bias_band",
    )(rel_bias)


N_STREAMS = 4
VT_ROWS = V_HEAD_DIM + 16
LOG2_E = math.log2(math.e)


def _attn_kernel(lq1_ref, lk1_ref, lq2_ref, lk2_ref, q1_ref, q2_ref, k1_ref, k2_ref,
                 v_ref, band_ref, g_ref, *rest, tq, cast_rows):
    n_cast = len(cast_rows)
    cast_in = rest[:n_cast]
    o_ref = rest[n_cast]
    cast_out = rest[n_cast + 1:2 * n_cast + 1]
    vt_ref, qm_ref, s_ref, m_ref, acc_ref = rest[2 * n_cast + 1:]
    n_blocks = v_ref.shape[1] // tq
    n_pairs_total = n_blocks * (n_blocks + 1) // 2
    assert n_pairs_total % 2 == 0
    q_refs = (q1_ref, q2_ref)
    k_refs = (k1_ref, k2_ref)
    streams = [(hh, mi) for hh in range(2) for mi in range(2)]

    ones_pad = (lax.broadcasted_iota(jnp.int32, (VT_ROWS - V_HEAD_DIM, tq), 0) == 0).astype(BF16)
    feat = lax.broadcasted_iota(jnp.int32, (2 * ATTN_HEAD_DIM, tq), 0)
    for c in range(n_blocks):
        rows = slice(c * tq, (c + 1) * tq)
        vt = v_ref[0, rows, :].astype(F32).T.astype(BF16)
        for hh in range(2):
            vt_ref[c, hh, :V_HEAD_DIM, :] = vt[hh * V_HEAD_DIM:(hh + 1) * V_HEAD_DIM]
            vt_ref[c, hh, V_HEAD_DIM:, :] = ones_pad
        for mi in range(2):
            qt = q_refs[mi][0, rows, :].astype(F32).T
            for hh in range(2):
                qm_ref[c, 2 * hh + mi] = jnp.where((feat >= ATTN_HEAD_DIM) == (hh == 1), qt, 0.0).astype(BF16)

    def reset_state():
        m_ref[...] = jnp.full(m_ref.shape, NEG_INF, F32)
        acc_ref[...] = jnp.zeros(acc_ref.shape, F32)

    reset_state()
    lam = (jnp.exp(jnp.sum(lq1_ref[...] * lk1_ref[...], axis=-1, keepdims=True))
           - jnp.exp(jnp.sum(lq2_ref[...] * lk2_ref[...], axis=-1, keepdims=True)) + LAMBDA_INIT)
    key = lax.broadcasted_iota(jnp.int32, (tq, tq), 0)
    qry = lax.broadcasted_iota(jnp.int32, (tq, tq), 1)
    causal = key <= qry

    def score_phase(i, j, buf):
        rows = pl.ds(pl.multiple_of(j * tq, tq), tq)
        start = jnp.where(j == i, 2 * tq, jnp.where(j == i - 1, tq, 0))
        band_rows = pl.ds(pl.multiple_of(start, tq), tq)
        keep = jnp.logical_or(causal, j != i)
        for si, (hh, mi) in enumerate(streams):
            s = jnp.dot(k_refs[mi][0, rows, :], qm_ref[i, si],
                        preferred_element_type=F32) + band_ref[hh, band_rows, :]
            s_ref[buf, si] = jnp.where(keep, s, NEG_INF)

    def value_phase(j, buf):
        for si, (hh, mi) in enumerate(streams):
            s = s_ref[buf, si]
            m_old = jnp.where(j == 0, NEG_INF, m_ref[si])
            m_new = jnp.maximum(m_old, jnp.max(s, axis=0, keepdims=True))
            alpha = jnp.exp2(m_old - m_new)
            p = jnp.exp2(s - m_new)
            acc_ref[buf, si] = alpha * acc_ref[1 - buf, si] + jnp.dot(
                vt_ref[j, hh], p.astype(BF16), preferred_element_type=F32)
            m_ref[si] = m_new

    def finish_block(i, buf):
        out_rows = pl.ds(pl.multiple_of(i * tq, tq), tq)
        den = V_HEAD_DIM
        for hh in range(2):
            s1, s2 = 2 * hh, 2 * hh + 1
            a = (acc_ref[buf, s1, :den] / acc_ref[buf, s1, den:den + 1]
                 - lam * (acc_ref[buf, s2, :den] / acc_ref[buf, s2, den:den + 1]))
            scale = lax.rsqrt(jnp.mean(a * a, axis=0, keepdims=True) + RMS_EPS)
            a = a * scale * g_ref[...] * (1.0 - LAMBDA_INIT)
            o_ref[0, out_rows, hh * V_HEAD_DIM:(hh + 1) * V_HEAD_DIM] = a.T.astype(o_ref.dtype)

    def half_step(i, j, parity):
        last_chunk = j == i
        ni = jnp.where(last_chunk, i + 1, i)
        nj = jnp.where(last_chunk, 0, j + 1)
        past_end = ni >= n_blocks
        score_phase(jnp.where(past_end, i, ni), jnp.where(past_end, j, nj), 1 - parity)
        value_phase(j, parity)
        return ni, nj

    def sweep_body(it, carry):
        for src, dst, rows_per_it in zip(cast_in, cast_out, cast_rows):
            blk = jnp.minimum(it, src.shape[0] // rows_per_it - 1)
            rows = pl.ds(pl.multiple_of(blk * rows_per_it, rows_per_it), rows_per_it)
            dst[rows, :] = src[rows, :].astype(BF16)
        i0, j0 = carry
        i1, j1 = half_step(i0, j0, 0)
        i2, j2 = half_step(i1, j1, 1)
        for parity, (i, j) in enumerate(((i0, j0), (i1, j1))):
            @pl.when(j == i)
            def _(parity=parity, i=i):
                finish_block(i, parity)
        return i2, j2

    score_phase(0, 0, 0)
    lax.fori_loop(0, n_pairs_total // 2, sweep_body, (jnp.int32(0), jnp.int32(0)))


def _attention(qkv, band, lq1, lk1, lq2, lk2, subln_g, *, batch, seq, n_heads, tq, cast=()):
    d_attn = n_heads * V_HEAD_DIM
    pair = 2 * ATTN_HEAD_DIM
    vpair = 2 * V_HEAD_DIM
    n_pairs = d_attn // 2 // pair
    n_blocks = seq // tq
    n_steps = batch * n_pairs
    n_iters = n_blocks * (n_blocks + 1) // 4
    cast_rows = []
    for w in cast:
        block_rows = w.shape[0] // n_steps
        rows_per_it = 16
        while block_rows % rows_per_it or block_rows // rows_per_it > n_iters:
            rows_per_it += 16
        assert w.shape[0] % n_steps == 0 and rows_per_it <= block_rows
        cast_rows.append(rows_per_it)
    cast_specs = [pl.BlockSpec((w.shape[0] // n_steps, w.shape[1]), lambda b, p: (b * n_pairs + p, 0))
                  for w in cast]
    qkv3 = qkv.reshape(batch, seq, 3 * d_attn)
    small = pl.BlockSpec((1, ATTN_HEAD_DIM), lambda b, p: (0, 0))
    return pl.pallas_call(
        functools.partial(_attn_kernel, tq=tq, cast_rows=tuple(cast_rows)),
        grid=(batch, n_pairs),
        in_specs=[
            small, small, small, small,
            pl.BlockSpec((1, seq, pair), lambda b, p: (b, 0, p)),
            pl.BlockSpec((1, seq, pair), lambda b, p: (b, 0, n_pairs + p)),
            pl.BlockSpec((1, seq, pair), lambda b, p: (b, 0, 2 * n_pairs + p)),
            pl.BlockSpec((1, seq, pair), lambda b, p: (b, 0, 3 * n_pairs + p)),
            pl.BlockSpec((1, seq, vpair), lambda b, p: (b, 0, 2 * n_pairs + p)),
            pl.BlockSpec((2, 3 * tq, tq), lambda b, p: (p, 0, 0)),
            pl.BlockSpec((V_HEAD_DIM, 1), lambda b, p: (0, 0)),
        ] + cast_specs,
        out_specs=[pl.BlockSpec((1, seq, vpair), lambda b, p: (b, 0, p))] + cast_specs,
        out_shape=[jax.ShapeDtypeStruct((batch, seq, d_attn), BF16)]
                  + [jax.ShapeDtypeStruct(w.shape, BF16) for w in cast],
        scratch_shapes=[pltpu.VMEM((n_blocks, 2, VT_ROWS, tq), BF16),
                        pltpu.VMEM((n_blocks, N_STREAMS, pair, tq), BF16),
                        pltpu.VMEM((2, N_STREAMS, tq, tq), F32),
                        pltpu.VMEM((N_STREAMS, 1, tq), F32),
                        pltpu.VMEM((2, N_STREAMS, VT_ROWS, tq), F32)],
        compiler_params=_params(("parallel", "parallel")),
        name="diff_attention",
    )(lq1, lk1, lq2, lk2, qkv3, qkv3, qkv3, qkv3, qkv3, band, subln_g, *cast)


def _block_diag_blocks(w, dst_ref):
    n_blocks, rows, cols = dst_ref.shape
    g, p, n = w.shape
    gl = g // n_blocks
    row_group = lax.broadcasted_iota(jnp.int32, (rows, cols), 0) // p
    col_group = lax.broadcasted_iota(jnp.int32, (rows, cols), 1) // n
    for j in range(n_blocks):
        stacked = w[j * gl:(j + 1) * gl].reshape(rows, n)
        tiled = jnp.concatenate([stacked] * gl, axis=1)
        dst_ref[j] = jnp.where(row_group == col_group, tiled, 0.0).astype(BF16)


def _ssm_params_kernel(a_re_ref, a_im_ref, log_dt_ref, b_re_ref, b_im_ref, c_re_ref, c_im_ref,
                       ab_re_ref, ab_im_ref, bd_re_ref, bd_im_ref, cd_re_ref, cd_im_ref):
    ar = a_re_ref[...]
    ai = a_im_ref[...]
    dt = jnp.exp(log_dt_ref[...])
    decay = jnp.exp(dt * ar)
    ab_re = decay * jnp.cos(dt * ai)
    ab_im = decay * jnp.sin(dt * ai)
    den = ar * ar + ai * ai
    nr = ab_re - 1.0
    ni = ab_im
    coef_re = (nr * ar + ni * ai) / den
    coef_im = (ni * ar - nr * ai) / den
    br = b_re_ref[...]
    bi = b_im_ref[...]
    ab_re_ref[...] = ab_re
    ab_im_ref[...] = ab_im
    _block_diag_blocks(coef_re * br - coef_im * bi, bd_re_ref)
    _block_diag_blocks(coef_re * bi + coef_im * br, bd_im_ref)
    _block_diag_blocks(c_re_ref[...], cd_re_ref)
    _block_diag_blocks(c_im_ref[...], cd_im_ref)


def _ssm_params(a_re, a_im, log_dt, b_re_t, b_im_t, c_re, c_im):
    g, _, n = a_re.shape
    p = b_re_t.shape[1]
    gl = g // SSM_BLOCKS
    dense = jax.ShapeDtypeStruct((SSM_BLOCKS, gl * p, gl * n), BF16)
    return pl.pallas_call(
        _ssm_params_kernel,
        out_shape=[jax.ShapeDtypeStruct((g, 1, n), F32), jax.ShapeDtypeStruct((g, 1, n), F32),
                   dense, dense, dense, dense],
        compiler_params=pltpu.CompilerParams(vmem_limit_bytes=V7X_VMEM_LIMIT_BYTES),
        name="ssm_params",
    )(a_re, a_im, log_dt, b_re_t, b_im_t, c_re, c_im)


SSM_COL_CHUNK = 512
SSM_BLOCKS = 4


def _ssm_input_drive(u_ref, lhs_ref, bmat_re_ref, bmat_im_ref, bu_re_ref, bu_im_ref, *, tb):
    n_seg = u_ref.shape[1]
    n_slabs = lhs_ref.shape[0]
    for seg in range(n_seg):
        for c in range(n_slabs):
            lhs_ref[c, pl.ds(seg, tb, stride=n_seg), :] = u_ref[0, seg, :, c * V7X_LANES:(c + 1) * V7X_LANES]
    per_block = n_slabs // SSM_BLOCKS
    kst = bu_re_ref.shape[1] // SSM_BLOCKS
    for j in range(SSM_BLOCKS):
        uj = jnp.concatenate([lhs_ref[c] for c in range(j * per_block, (j + 1) * per_block)],
                             axis=-1).astype(BF16)
        bu_re_ref[:, j * kst:(j + 1) * kst] = jnp.dot(uj, bmat_re_ref[j], preferred_element_type=F32)
        bu_im_ref[:, j * kst:(j + 1) * kst] = jnp.dot(uj, bmat_im_ref[j], preferred_element_type=F32)


def _ssm_recurrence(ab_re_ref, ab_im_ref, st_re_ref, st_im_ref, bu_re_ref, bu_im_ref, *, tb,
                    x_re_ref=None, x_im_ref=None):
    n_cols = st_re_ref.shape[1]
    two = 2 * V7X_SUBLANES
    for c in range(n_cols // SSM_COL_CHUNK):
        cs = slice(c * SSM_COL_CHUNK, (c + 1) * SSM_COL_CHUNK)
        ar = jnp.broadcast_to(ab_re_ref[:, cs], (V7X_SUBLANES, SSM_COL_CHUNK))
        ai = jnp.broadcast_to(ab_im_ref[:, cs], (V7X_SUBLANES, SSM_COL_CHUNK))

        def step(t2, carry, cs=cs, ar=ar, ai=ai):
            xr, xi = carry
            rows = pl.ds(pl.multiple_of(t2 * two, two), two)
            bur = bu_re_ref[rows, cs]
            bui = bu_im_ref[rows, cs]
            xr1 = ar * xr - ai * xi + bur[:V7X_SUBLANES]
            xi1 = ar * xi + ai * xr + bui[:V7X_SUBLANES]
            xr2 = ar * xr1 - ai * xi1 + bur[V7X_SUBLANES:]
            xi2 = ar * xi1 + ai * xr1 + bui[V7X_SUBLANES:]
            if x_re_ref is not None:
                x_re_ref[rows, cs] = jnp.concatenate([xr1, xr2], axis=0).astype(BF16)
                x_im_ref[rows, cs] = jnp.concatenate([xi1, xi2], axis=0).astype(BF16)
            return xr2, xi2

        xr, xi = lax.fori_loop(0, tb // 2, step, (st_re_ref[:, cs], st_im_ref[:, cs]), unroll=2)
        st_re_ref[:, cs] = xr
        st_im_ref[:, cs] = xi


def _ssm_ends_kernel(u_ref, bmat_re_ref, bmat_im_ref, ab_re_ref, ab_im_ref, end_re_ref, end_im_ref,
                     lhs_ref, bu_re_ref, bu_im_ref, st_re_ref, st_im_ref, *, tb):
    tblk = pl.program_id(1)

    @pl.when(tblk == 0)
    def _():
        st_re_ref[...] = jnp.zeros_like(st_re_ref)
        st_im_ref[...] = jnp.zeros_like(st_im_ref)

    _ssm_input_drive(u_ref, lhs_ref, bmat_re_ref, bmat_im_ref, bu_re_ref, bu_im_ref, tb=tb)
    _ssm_recurrence(ab_re_ref, ab_im_ref, st_re_ref, st_im_ref, bu_re_ref, bu_im_ref, tb=tb)

    @pl.when(tblk == pl.num_programs(1) - 1)
    def _():
        end_re_ref[0] = st_re_ref[...]
        end_im_ref[0] = st_im_ref[...]


def _ssm_main_kernel(u_ref, end_re_ref, end_im_ref, bmat_re_ref, bmat_im_ref, ab_re_ref, ab_im_ref,
                     cmat_re_ref, cmat_im_ref, d_ref, wglu_ref, bglu_ref, o_ref,
                     lhs_ref, bu_re_ref, bu_im_ref, st_re_ref, st_im_ref, x_re_ref, x_im_ref, *, tb, seg_len):
    tblk = pl.program_id(1)
    n_seg = st_re_ref.shape[0]

    @pl.when(tblk == 0)
    def _():
        pr = ab_re_ref[...]
        pi = ab_im_ref[...]
        for _ in range(int(math.log2(seg_len))):
            pr, pi = pr * pr - pi * pi, 2.0 * pr * pi
        er = jnp.zeros_like(pr)
        ei = jnp.zeros_like(pi)
        st_re_ref[0:1, :] = er
        st_im_ref[0:1, :] = ei
        for j in range(1, n_seg):
            lr = end_re_ref[0, j - 1:j, :]
            li = end_im_ref[0, j - 1:j, :]
            er, ei = lr + pr * er - pi * ei, li + pr * ei + pi * er
            st_re_ref[j:j + 1, :] = er
            st_im_ref[j:j + 1, :] = ei

    _ssm_input_drive(u_ref, lhs_ref, bmat_re_ref, bmat_im_ref, bu_re_ref, bu_im_ref, tb=tb)
    _ssm_recurrence(ab_re_ref, ab_im_ref, st_re_ref, st_im_ref, bu_re_ref, bu_im_ref, tb=tb,
                    x_re_ref=x_re_ref, x_im_ref=x_im_ref)

    n_slabs = lhs_ref.shape[0]
    kst = bu_re_ref.shape[1] // SSM_BLOCKS
    ys = []
    for j in range(SSM_BLOCKS):
        xr = x_re_ref[:, j * kst:(j + 1) * kst]
        xi = x_im_ref[:, j * kst:(j + 1) * kst]
        nt = (((1,), (1,)), ((), ()))
        ys.append(lax.dot_general(xr, cmat_re_ref[j], nt, preferred_element_type=F32)
                  - lax.dot_general(xi, cmat_im_ref[j], nt, preferred_element_type=F32))
    u_rows = jnp.concatenate([lhs_ref[c] for c in range(n_slabs)], axis=-1)
    y = jnp.concatenate(ys, axis=-1) + d_ref[...] * u_rows
    z = jax.nn.gelu(y)
    gate = jnp.dot(z.astype(BF16), wglu_ref[...], preferred_element_type=F32) + bglu_ref[...]
    s = z * jax.nn.sigmoid(gate)
    for c in range(n_slabs):
        lhs_ref[c] = s[:, c * V7X_LANES:(c + 1) * V7X_LANES]
    for seg in range(n_seg):
        for c in range(n_slabs):
            o_ref[0, seg, :, c * V7X_LANES:(c + 1) * V7X_LANES] = lhs_ref[c, pl.ds(seg, tb, stride=n_seg), :]


def _ssm(u_seg, bmat_re, bmat_im, ab_re, ab_im, cmat_re, cmat_im, d_skip, w_glu, b_glu, *, tb):
    batch, n_seg, seg_len, d_ssm = u_seg.shape
    n_cols = ab_re.shape[1]
    rows = tb * n_seg
    grid = (batch, seg_len // tb)
    u_spec = pl.BlockSpec((1, n_seg, tb, d_ssm), lambda b, t: (b, 0, t, 0))
    end_spec = pl.BlockSpec((1, n_seg, n_cols), lambda b, t: (b, 0, 0))

    def whole(a):
        return pl.BlockSpec(a.shape, lambda b, t: (0,) * a.ndim, pipeline_mode=pl.Buffered(1))

    scratch = [pltpu.VMEM((d_ssm // V7X_LANES, rows, V7X_LANES), F32), pltpu.VMEM((rows, n_cols), F32),
               pltpu.VMEM((rows, n_cols), F32), pltpu.VMEM((n_seg, n_cols), F32),
               pltpu.VMEM((n_seg, n_cols), F32)]
    end_re, end_im = pl.pallas_call(
        functools.partial(_ssm_ends_kernel, tb=tb),
        grid=grid,
        in_specs=[u_spec, whole(bmat_re), whole(bmat_im), whole(ab_re), whole(ab_im)],
        out_specs=[end_spec, end_spec],
        out_shape=[jax.ShapeDtypeStruct((batch, n_seg, n_cols), F32)] * 2,
        scratch_shapes=scratch,
        compiler_params=_params(("parallel", "arbitrary")),
        name="ssm_ends",
    )(u_seg, bmat_re, bmat_im, ab_re, ab_im)
    return pl.pallas_call(
        functools.partial(_ssm_main_kernel, tb=tb, seg_len=seg_len),
        grid=grid,
        in_specs=[u_spec, end_spec, end_spec, whole(bmat_re), whole(bmat_im), whole(ab_re), whole(ab_im),
                  whole(cmat_re), whole(cmat_im), whole(d_skip), whole(w_glu), whole(b_glu)],
        out_specs=u_spec,
        out_shape=jax.ShapeDtypeStruct(u_seg.shape, F32),
        scratch_shapes=scratch + [pltpu.VMEM((rows, n_cols), BF16), pltpu.VMEM((rows, n_cols), BF16)],
        compiler_params=_params(("parallel", "arbitrary")),
        name="ssm_main",
    )(u_seg, end_re, end_im, bmat_re, bmat_im, ab_re, ab_im, cmat_re, cmat_im, d_skip, w_glu, b_glu)


def _out_proj_kernel(a_ref, s_ref, wa_ref, ws_ref, xh_ref, xt_ref, g_ref, o_ref, *, n_head):
    m = (jnp.dot(a_ref[...], wa_ref[...], preferred_element_type=F32)
         + jnp.dot(s_ref[...].astype(BF16), ws_ref[...], preferred_element_type=F32))
    o_ref[...] = _row_tile(xh_ref, xt_ref, n_head) + m * _rms_scale(m) * g_ref[...]


def _out_proj(a, s, w, x_parts, g, *, tm):
    t = a.shape[0]
    d = w.shape[1]
    da = a.shape[1]
    ds = s.shape[1]
    assert da == ds and w.shape[0] == da + ds
    x_specs, n_head = _row_tile_specs(x_parts, tm)
    return pl.pallas_call(
        functools.partial(_out_proj_kernel, n_head=n_head),
        grid=(t // tm,),
        in_specs=[
            pl.BlockSpec((tm, da), lambda i: (i, 0)),
            pl.BlockSpec((tm, ds), lambda i: (i, 0)),
            pl.BlockSpec((da, d), lambda i: (0, 0)),
            pl.BlockSpec((ds, d), lambda i: (1, 0)),
        ] + x_specs + [
            pl.BlockSpec((1, d), lambda i: (0, 0)),
        ],
        out_specs=pl.BlockSpec((tm, d), lambda i: (i, 0)),
        out_shape=jax.ShapeDtypeStruct((t, d), F32),
        compiler_params=_params(("parallel",)),
        name="out_proj",
    )(a, s, w, w, *x_parts, g)


def kernel(x, ffn1_pre_g, ffn1_w_gate, ffn1_w_up, ffn1_w_down, ffn1_post_g, mix_pre_g, w_in, lambda_q1, lambda_k1, lambda_q2, lambda_k2, subln_g, rel_bias, ssm_a_re, ssm_a_im, ssm_b_re, ssm_b_im, ssm_c_re, ssm_c_im, ssm_d, ssm_log_dt, w_glu, b_glu, w_out, mix_post_g, ffn2_pre_g, ffn2_w_gate, ffn2_w_up, ffn2_w_down, ffn2_post_g):
    batch, seq, d_model = x.shape
    depth = ffn1_pre_g.shape[0]
    assert depth == 1, "LAMBDA_INIT is specialised to a single layer"
    n_heads = rel_bias.shape[1]
    d_attn = n_heads * V_HEAD_DIM
    n_groups, n_state = ssm_a_re.shape[1:]
    d_ssm = n_groups * SSM_GROUP
    n_seg = V7X_SUBLANES
    seg_len = seq // n_seg
    tokens = batch * seq
    tm, tf, tq, tb = 512, 512, 256, 64
    tm_first = 2 * tm

    xt = x.reshape(tokens, d_model)
    row = lambda v: v.reshape(1, -1)
    l = 0

    x1_head, wg1, wu1, wd1, w_in_bf, w_glu_bf = _ffn_first(
        xt, row(ffn1_pre_g[l]), ffn1_w_gate[l], ffn1_w_up[l], ffn1_w_down[l], row(ffn1_post_g[l]),
        tm=tm_first, tf=tf // 2, cast=(w_in[l], w_glu[l]))
    x1_tail = _ffn(xt, row(ffn1_pre_g[l]), wg1, wu1, wd1, row(ffn1_post_g[l]), tm=tm, tf=tf,
                   rows_done=tm_first)
    x1 = (x1_head, x1_tail)

    qkv, u = _in_proj(x1, row(mix_pre_g[l]), w_in_bf, tm=tm, q_scale=ATTN_HEAD_DIM ** -0.5 * LOG2_E)

    band = _bias_band(rel_bias, tq=tq, scale=LOG2_E)
    a, wg2, wu2, wd2, wo = _attention(
        qkv, band, row(lambda_q1[l]), row(lambda_k1[l]), row(lambda_q2[l]), row(lambda_k2[l]),
        subln_g[l].reshape(-1, 1), batch=batch, seq=seq, n_heads=n_heads, tq=tq,
        cast=(ffn2_w_gate[l], ffn2_w_up[l], ffn2_w_down[l], w_out[l]))

    ab_re, ab_im, bmat_re, bmat_im, cmat_re, cmat_im = _ssm_params(
        ssm_a_re[l].reshape(n_groups, 1, n_state), ssm_a_im[l].reshape(n_groups, 1, n_state),
        ssm_log_dt[l].reshape(n_groups, 1, 1),
        jnp.swapaxes(ssm_b_re[l], 1, 2), jnp.swapaxes(ssm_b_im[l], 1, 2), ssm_c_re[l], ssm_c_im[l])
    u_seg = u.reshape(batch, n_seg, seg_len, d_ssm)
    s_seg = _ssm(u_seg, bmat_re, bmat_im, ab_re.reshape(1, -1), ab_im.reshape(1, -1), cmat_re, cmat_im,
                 row(ssm_d[l]), w_glu_bf, row(b_glu[l]), tb=tb)
    s = s_seg.reshape(tokens, d_ssm)

    x2 = _out_proj(a.reshape(tokens, d_attn), s, wo, x1, row(mix_post_g[l]), tm=tm)

    x3 = _ffn(x2, row(ffn2_pre_g[l]), wg2, wu2, wd2, row(ffn2_post_g[l]), tm=tm, tf=tf)
    return x3.reshape(batch, seq, d_model)
```

```python
import functools
import math

import jax
import jax.numpy as jnp
from jax import lax
from jax.experimental import pallas as pl
from jax.experimental.pallas import tpu as pltpu

V7X_LANES = 128
V7X_SUBLANES = 8
V7X_MXU_COLS = 256
V7X_VMEM_LIMIT_BYTES = 56 * 1024 * 1024

RMS_EPS = 1e-6
NEG_INF = -1e30
N_BUCKETS = 32
MAX_DISTANCE = 128
ATTN_HEAD_DIM = 64
V_HEAD_DIM = 128
SSM_GROUP = 16
SSM_STATE = 64
LAMBDA_INIT = 0.8 - 0.6 * math.exp(-0.3 * 0)

F32 = jnp.float32
BF16 = jnp.bfloat16
EPILOGUE_ROWS = 128


def _params(semantics):
    return pltpu.CompilerParams(dimension_semantics=semantics,
                                vmem_limit_bytes=V7X_VMEM_LIMIT_BYTES)


def _rms_scale(v):
    return lax.rsqrt(jnp.mean(v * v, axis=-1, keepdims=True) + RMS_EPS)


def _ffn_prologue(x_ref, pre_g_ref, xn_ref, acc_ref):
    xf = x_ref[...]
    xn_ref[...] = (xf * _rms_scale(xf) * pre_g_ref[...]).astype(BF16)
    acc_ref[...] = jnp.zeros_like(acc_ref)


def _ffn_chunk(xn_ref, acc_ref, wg_ref, wu_ref, wd_ref):
    xn = xn_ref[...]
    n_split = 2 if wg_ref.shape[1] >= 2 * V7X_MXU_COLS else 1
    half = wg_ref.shape[1] // n_split
    pre = []
    for c in range(n_split):
        cols = slice(c * half, (c + 1) * half)
        pre.append((jnp.dot(xn, wg_ref[:, cols], preferred_element_type=F32),
                    jnp.dot(xn, wu_ref[:, cols], preferred_element_type=F32)))
    out = acc_ref[...]
    for c, (gate, up) in enumerate(pre):
        hidden = (gate * jax.nn.sigmoid(gate) * up).astype(BF16)
        out = out + jnp.dot(hidden, wd_ref[c * half:(c + 1) * half, :], preferred_element_type=F32)
    acc_ref[...] = out


def _ffn_epilogue(x_ref, acc_ref, post_g_ref, o_ref):
    scale = _rms_scale(acc_ref[...])
    g_half = 0.5 * post_g_ref[...]
    for r in range(0, acc_ref.shape[0], EPILOGUE_ROWS):
        rows = slice(r, r + EPILOGUE_ROWS)
        o_ref[rows, :] = x_ref[rows, :] + acc_ref[rows, :] * scale[rows] * g_half


def _ffn_kernel(x_ref, pre_g_ref, wg_ref, wu_ref, wd_ref, post_g_ref, o_ref, xn_ref, acc_ref):
    k = pl.program_id(1)

    @pl.when(k == 0)
    def _():
        _ffn_prologue(x_ref, pre_g_ref, xn_ref, acc_ref)

    _ffn_chunk(xn_ref, acc_ref, wg_ref, wu_ref, wd_ref)

    @pl.when(k == pl.num_programs(1) - 1)
    def _():
        _ffn_epilogue(x_ref, acc_ref, post_g_ref, o_ref)


def _ffn_first_kernel(x_ref, pre_g_ref, wg32_ref, wu32_ref, wd32_ref, post_g_ref, *rest, cast_steps):
    n_cast = len(cast_steps)
    cast_in = rest[:n_cast]
    o_ref, wg_ref, wu_ref, wd_ref = rest[n_cast:n_cast + 4]
    cast_out = rest[n_cast + 4:2 * n_cast + 4]
    xn_ref, acc_ref = rest[2 * n_cast + 4:]
    k = pl.program_id(0)

    @pl.when(k == 0)
    def _():
        _ffn_prologue(x_ref, pre_g_ref, xn_ref, acc_ref)

    for src, dst, steps in zip(cast_in, cast_out, cast_steps):
        @pl.when(k < steps)
        def _(src=src, dst=dst):
            dst[...] = src[...].astype(BF16)

    wg = wg32_ref[...].astype(BF16)
    wu = wu32_ref[...].astype(BF16)
    wd = wd32_ref[...].astype(BF16)
    wg_ref[...] = wg
    wu_ref[...] = wu
    wd_ref[...] = wd
    _ffn_chunk(xn_ref, acc_ref, wg, wu, wd)

    @pl.when(k == pl.num_programs(0) - 1)
    def _():
        _ffn_epilogue(x_ref, acc_ref, post_g_ref, o_ref)


def _ffn_first(x, pre_g, wg32, wu32, wd32, post_g, *, tm, tf, cast=()):
    t, d = x.shape
    f = wg32.shape[1]
    n_chunks = f // tf
    cast_steps = []
    for w in cast:
        steps = 1
        while steps * 2 <= n_chunks and w.shape[0] % (16 * steps * 2) == 0:
            steps *= 2
        cast_steps.append(steps)
    cast_specs = [pl.BlockSpec((w.shape[0] // steps, w.shape[1]),
                               lambda k, steps=steps: (jnp.minimum(k, steps - 1), 0))
                  for w, steps in zip(cast, cast_steps)]
    w_specs = [pl.BlockSpec((d, tf), lambda k: (0, k)), pl.BlockSpec((d, tf), lambda k: (0, k)),
               pl.BlockSpec((tf, d), lambda k: (k, 0))]
    return pl.pallas_call(
        functools.partial(_ffn_first_kernel, cast_steps=tuple(cast_steps)),
        grid=(n_chunks,),
        in_specs=[pl.BlockSpec((tm, d), lambda k: (0, 0), pipeline_mode=pl.Buffered(1)),
                  pl.BlockSpec((1, d), lambda k: (0, 0))]
                 + w_specs + [pl.BlockSpec((1, d), lambda k: (0, 0))] + cast_specs,
        out_specs=[pl.BlockSpec((tm, d), lambda k: (0, 0), pipeline_mode=pl.Buffered(1))] + w_specs + cast_specs,
        out_shape=[jax.ShapeDtypeStruct((tm, d), F32)]
                  + [jax.ShapeDtypeStruct(w.shape, BF16) for w in (wg32, wu32, wd32) + tuple(cast)],
        scratch_shapes=[pltpu.VMEM((tm, d), BF16), pltpu.VMEM((tm, d), F32)],
        compiler_params=_params(("arbitrary",)),
        name="ffn_first",
    )(x, pre_g, wg32, wu32, wd32, post_g, *cast)


def _ffn(x, pre_g, wg, wu, wd, post_g, *, tm, tf, rows_done=0):
    t, d = x.shape
    f = wg.shape[1]
    assert rows_done % tm == 0
    skip = rows_done // tm
    return pl.pallas_call(
        _ffn_kernel,
        grid=(t // tm - skip, f // tf),
        in_specs=[
            pl.BlockSpec((tm, d), lambda i, k: (i + skip, 0)),
            pl.BlockSpec((1, d), lambda i, k: (0, 0)),
            pl.BlockSpec((d, tf), lambda i, k: (0, k)),
            pl.BlockSpec((d, tf), lambda i, k: (0, k)),
            pl.BlockSpec((tf, d), lambda i, k: (k, 0)),
            pl.BlockSpec((1, d), lambda i, k: (0, 0)),
        ],
        out_specs=pl.BlockSpec((tm, d), lambda i, k: (i, 0)),
        out_shape=jax.ShapeDtypeStruct((t - rows_done, d), F32),
        scratch_shapes=[pltpu.VMEM((tm, d), BF16), pltpu.VMEM((tm, d), F32)],
        compiler_params=_params(("parallel", "arbitrary")),
        name="ffn",
    )(x, pre_g, wg, wu, wd, post_g)


def _row_tile_specs(parts, tm):
    head, tail = parts
    n_head = head.shape[0] // tm
    assert head.shape[0] % tm == 0 and tail.shape[0] % tm == 0 and head.shape[1] == tail.shape[1]
    d = head.shape[1]
    specs = [pl.BlockSpec((tm, d), lambda i: (jnp.minimum(i, n_head - 1), 0)),
             pl.BlockSpec((tm, d), lambda i: (jnp.maximum(i - n_head, 0), 0))]
    return specs, n_head


def _row_tile(head_ref, tail_ref, n_head):
    return jnp.where(pl.program_id(0) < n_head, head_ref[...], tail_ref[...])


def _in_proj_kernel(xh_ref, xt_ref, g_ref, w_ref, qkv_ref, u_ref, *, q_scale, n_head):
    xf = _row_tile(xh_ref, xt_ref, n_head)
    hn = (xf * _rms_scale(xf) * g_ref[...]).astype(BF16)
    tn = u_ref.shape[1]
    for n in range(4):
        acc = jnp.dot(hn, w_ref[:, n * tn:(n + 1) * tn], preferred_element_type=F32)
        if n == 0:
            qkv_ref[:, :tn] = (acc * q_scale).astype(BF16)
        elif n < 3:
            qkv_ref[:, n * tn:(n + 1) * tn] = acc.astype(BF16)
        else:
            u_ref[...] = acc


def _in_proj(x_parts, g, w, *, tm, q_scale):
    t = x_parts[0].shape[0] + x_parts[1].shape[0]
    d = x_parts[0].shape[1]
    tn = w.shape[1] // 4
    x_specs, n_head = _row_tile_specs(x_parts, tm)
    return pl.pallas_call(
        functools.partial(_in_proj_kernel, q_scale=q_scale, n_head=n_head),
        grid=(t // tm,),
        in_specs=x_specs + [
            pl.BlockSpec((1, d), lambda i: (0, 0)),
            pl.BlockSpec(w.shape, lambda i: (0, 0), pipeline_mode=pl.Buffered(1)),
        ],
        out_specs=[
            pl.BlockSpec((tm, 3 * tn), lambda i: (i, 0)),
            pl.BlockSpec((tm, tn), lambda i: (i, 0)),
        ],
        out_shape=[jax.ShapeDtypeStruct((t, 3 * tn), BF16),
                   jax.ShapeDtypeStruct((t, tn), F32)],
        compiler_params=_params(("parallel",)),
        name="in_proj",
    )(*x_parts, g, w)


def _bias_band_kernel(rb_ref, o_ref, *, tq, scale):
    h = pl.program_id(0)
    key = lax.broadcasted_iota(jnp.int32, (3 * tq, tq), 0)
    qry = lax.broadcasted_iota(jnp.int32, (3 * tq, tq), 1)
    dist = 2 * tq + qry - key
    n = jnp.maximum(dist, 0)
    max_exact = N_BUCKETS // 2
    nf = jnp.maximum(n, 1).astype(F32)
    large = max_exact + (jnp.log(nf / max_exact) / math.log(MAX_DISTANCE / max_exact)
                         * (N_BUCKETS - max_exact)).astype(jnp.int32)
    large = jnp.minimum(large, N_BUCKETS - 1)
    bucket = jnp.where(n < max_exact, n, large)
    val = jnp.zeros((3 * tq, tq), F32)
    for b in range(N_BUCKETS):
        val = jnp.where(bucket == b, rb_ref[b, h], val)
    o_ref[0] = jnp.where(dist >= 0, val * scale, NEG_INF)


def _bias_band(rel_bias, *, tq, scale):
    assert tq >= MAX_DISTANCE
    n_heads = rel_bias.shape[1]
    return pl.pallas_call(
        functools.partial(_bias_band_kernel, tq=tq, scale=scale),
        grid=(n_heads,),
        in_specs=[pl.BlockSpec(memory_space=pltpu.SMEM)],
        out_specs=pl.BlockSpec((1, 3 * tq, tq), lambda h: (h, 0, 0)),
        out_shape=jax.ShapeDtypeStruct((n_heads, 3 * tq, tq), F32),
        compiler_params=_params(("arbitrary",)),
        name="bias_band",
    )(rel_bias)


N_STREAMS = 4
VT_ROWS = V_HEAD_DIM + 16
LOG2_E = math.log2(math.e)


def _attn_kernel(lq1_ref, lk1_ref, lq2_ref, lk2_ref, q1_ref, q2_ref, k1_ref, k2_ref,
                 v_ref, band_ref, g_ref, *rest, tq, cast_rows):
    n_cast = len(cast_rows)
    cast_in = rest[:n_cast]
    o_ref = rest[n_cast]
    cast_out = rest[n_cast + 1:2 * n_cast + 1]
    vt_ref, qm_ref, s_ref, m_ref, acc_ref = rest[2 * n_cast + 1:]
    n_blocks = v_ref.shape[1] // tq
    n_pairs_total = n_blocks * (n_blocks + 1) // 2
    assert n_pairs_total % 2 == 0
    q_refs = (q1_ref, q2_ref)
    k_refs = (k1_ref, k2_ref)
    streams = [(hh, mi) for hh in range(2) for mi in range(2)]

    ones_pad = (lax.broadcasted_iota(jnp.int32, (VT_ROWS - V_HEAD_DIM, tq), 0) == 0).astype(BF16)
    feat = lax.broadcasted_iota(jnp.int32, (2 * ATTN_HEAD_DIM, tq), 0)
    for c in range(n_blocks):
        rows = slice(c * tq, (c + 1) * tq)
        vt = v_ref[0, rows, :].astype(F32).T.astype(BF16)
        for hh in range(2):
            vt_ref[c, hh, :V_HEAD_DIM, :] = vt[hh * V_HEAD_DIM:(hh + 1) * V_HEAD_DIM]
            vt_ref[c, hh, V_HEAD_DIM:, :] = ones_pad
        for mi in range(2):
            qt = q_refs[mi][0, rows, :].astype(F32).T
            for hh in range(2):
                qm_ref[c, mi, :, hh * tq:(hh + 1) * tq] = jnp.where(
                    (feat >= ATTN_HEAD_DIM) == (hh == 1), qt, 0.0).astype(BF16)

    def reset_state():
        m_ref[...] = jnp.full(m_ref.shape, NEG_INF, F32)
        acc_ref[...] = jnp.zeros(acc_ref.shape, F32)

    reset_state()
    lam = (jnp.exp(jnp.sum(lq1_ref[...] * lk1_ref[...], axis=-1, keepdims=True))
           - jnp.exp(jnp.sum(lq2_ref[...] * lk2_ref[...], axis=-1, keepdims=True)) + LAMBDA_INIT)

    def score_phase(i, j, buf):
        rows = pl.ds(pl.multiple_of(j * tq, tq), tq)
        start = jnp.where(j == i, 2 * tq, jnp.where(j == i - 1, tq, 0))
        band_rows = pl.ds(pl.multiple_of(start, tq), tq)
        band2 = jnp.concatenate([band_ref[0, band_rows, :], band_ref[1, band_rows, :]], axis=1)
        for mi in range(2):
            s_ref[buf, mi] = jnp.dot(k_refs[mi][0, rows, :], qm_ref[i, mi],
                                     preferred_element_type=F32) + band2

    def value_phase(j, buf):
        alphas, ps = [], []
        for mi in range(2):
            s = s_ref[buf, mi]
            m_old = jnp.where(j == 0, NEG_INF, m_ref[mi])
            m_new = jnp.maximum(m_old, jnp.max(s, axis=0, keepdims=True))
            alphas.append(jnp.exp2(m_old - m_new))
            ps.append(jnp.exp2(s - m_new).astype(BF16))
            m_ref[mi] = m_new
        for hh in range(2):
            head = slice(hh * tq, (hh + 1) * tq)
            p2 = jnp.concatenate([ps[0][:, head], ps[1][:, head]], axis=1)
            alpha2 = jnp.concatenate([alphas[0][:, head], alphas[1][:, head]], axis=1)
            acc_ref[buf, hh] = alpha2 * acc_ref[1 - buf, hh] + jnp.dot(
                vt_ref[j, hh], p2, preferred_element_type=F32)

    def finish_block(i, buf):
        out_rows = pl.ds(pl.multiple_of(i * tq, tq), tq)
        den = V_HEAD_DIM
        for hh in range(2):
            a = (acc_ref[buf, hh, :den, :tq] / acc_ref[buf, hh, den:den + 1, :tq]
                 - lam * (acc_ref[buf, hh, :den, tq:] / acc_ref[buf, hh, den:den + 1, tq:]))
            scale = lax.rsqrt(jnp.mean(a * a, axis=0, keepdims=True) + RMS_EPS)
            a = a * scale * g_ref[...] * (1.0 - LAMBDA_INIT)
            o_ref[0, out_rows, hh * V_HEAD_DIM:(hh + 1) * V_HEAD_DIM] = a.T.astype(o_ref.dtype)

    def half_step(i, j, parity):
        last_chunk = j == i
        ni = jnp.where(last_chunk, i + 1, i)
        nj = jnp.where(last_chunk, 0, j + 1)
        past_end = ni >= n_blocks
        score_phase(jnp.where(past_end, i, ni), jnp.where(past_end, j, nj), 1 - parity)
        value_phase(j, parity)
        return ni, nj

    def sweep_body(it, carry):
        for src, dst, rows_per_it in zip(cast_in, cast_out, cast_rows):
            blk = jnp.minimum(it, src.shape[0] // rows_per_it - 1)
            rows = pl.ds(pl.multiple_of(blk * rows_per_it, rows_per_it), rows_per_it)
            dst[rows, :] = src[rows, :].astype(BF16)
        i0, j0 = carry
        i1, j1 = half_step(i0, j0, 0)
        i2, j2 = half_step(i1, j1, 1)
        for parity, (i, j) in enumerate(((i0, j0), (i1, j1))):
            @pl.when(j == i)
            def _(parity=parity, i=i):
                finish_block(i, parity)
        return i2, j2

    score_phase(0, 0, 0)
    lax.fori_loop(0, n_pairs_total // 2, sweep_body, (jnp.int32(0), jnp.int32(0)))


def _attention(qkv, band, lq1, lk1, lq2, lk2, subln_g, *, batch, seq, n_heads, tq, cast=()):
    d_attn = n_heads * V_HEAD_DIM
    pair = 2 * ATTN_HEAD_DIM
    vpair = 2 * V_HEAD_DIM
    n_pairs = d_attn // 2 // pair
    n_blocks = seq // tq
    n_steps = batch * n_pairs
    n_iters = n_blocks * (n_blocks + 1) // 4
    cast_rows = []
    for w in cast:
        block_rows = w.shape[0] // n_steps
        rows_per_it = 16
        while block_rows % rows_per_it or block_rows // rows_per_it > n_iters:
            rows_per_it += 16
        assert w.shape[0] % n_steps == 0 and rows_per_it <= block_rows
        cast_rows.append(rows_per_it)
    cast_specs = [pl.BlockSpec((w.shape[0] // n_steps, w.shape[1]), lambda b, p: (b * n_pairs + p, 0))
                  for w in cast]
    qkv3 = qkv.reshape(batch, seq, 3 * d_attn)
    small = pl.BlockSpec((1, ATTN_HEAD_DIM), lambda b, p: (0, 0))
    return pl.pallas_call(
        functools.partial(_attn_kernel, tq=tq, cast_rows=tuple(cast_rows)),
        grid=(batch, n_pairs),
        in_specs=[
            small, small, small, small,
            pl.BlockSpec((1, seq, pair), lambda b, p: (b, 0, p)),
            pl.BlockSpec((1, seq, pair), lambda b, p: (b, 0, n_pairs + p)),
            pl.BlockSpec((1, seq, pair), lambda b, p: (b, 0, 2 * n_pairs + p)),
            pl.BlockSpec((1, seq, pair), lambda b, p: (b, 0, 3 * n_pairs + p)),
            pl.BlockSpec((1, seq, vpair), lambda b, p: (b, 0, 2 * n_pairs + p)),
            pl.BlockSpec((2, 3 * tq, tq), lambda b, p: (p, 0, 0)),
            pl.BlockSpec((V_HEAD_DIM, 1), lambda b, p: (0, 0)),
        ] + cast_specs,
        out_specs=[pl.BlockSpec((1, seq, vpair), lambda b, p: (b, 0, p))] + cast_specs,
        out_shape=[jax.ShapeDtypeStruct((batch, seq, d_attn), BF16)]
                  + [jax.ShapeDtypeStruct(w.shape, BF16) for w in cast],
        scratch_shapes=[pltpu.VMEM((n_blocks, 2, VT_ROWS, tq), BF16),
                        pltpu.VMEM((n_blocks, 2, pair, 2 * tq), BF16),
                        pltpu.VMEM((2, 2, tq, 2 * tq), F32),
                        pltpu.VMEM((2, 1, 2 * tq), F32),
                        pltpu.VMEM((2, 2, VT_ROWS, 2 * tq), F32)],
        compiler_params=_params(("parallel", "parallel")),
        name="diff_attention",
    )(lq1, lk1, lq2, lk2, qkv3, qkv3, qkv3, qkv3, qkv3, band, subln_g, *cast)


def _block_diag_blocks(w, dst_ref):
    n_blocks, rows, cols = dst_ref.shape
    g, p, n = w.shape
    gl = g // n_blocks
    row_group = lax.broadcasted_iota(jnp.int32, (rows, cols), 0) // p
    col_group = lax.broadcasted_iota(jnp.int32, (rows, cols), 1) // n
    for j in range(n_blocks):
        stacked = w[j * gl:(j + 1) * gl].reshape(rows, n)
        tiled = jnp.concatenate([stacked] * gl, axis=1)
        dst_ref[j] = jnp.where(row_group == col_group, tiled, 0.0).astype(BF16)


def _ssm_params_kernel(a_re_ref, a_im_ref, log_dt_ref, b_re_ref, b_im_ref, c_re_ref, c_im_ref,
                       ab_re_ref, ab_im_ref, bd_re_ref, bd_im_ref, cd_re_ref, cd_im_ref):
    ar = a_re_ref[...]
    ai = a_im_ref[...]
    dt = jnp.exp(log_dt_ref[...])
    decay = jnp.exp(dt * ar)
    ab_re = decay * jnp.cos(dt * ai)
    ab_im = decay * jnp.sin(dt * ai)
    den = ar * ar + ai * ai
    nr = ab_re - 1.0
    ni = ab_im
    coef_re = (nr * ar + ni * ai) / den
    coef_im = (ni * ar - nr * ai) / den
    br = b_re_ref[...]
    bi = b_im_ref[...]
    ab_re_ref[...] = ab_re
    ab_im_ref[...] = ab_im
    _block_diag_blocks(coef_re * br - coef_im * bi, bd_re_ref)
    _block_diag_blocks(coef_re * bi + coef_im * br, bd_im_ref)
    _block_diag_blocks(c_re_ref[...], cd_re_ref)
    _block_diag_blocks(c_im_ref[...], cd_im_ref)


def _ssm_params(a_re, a_im, log_dt, b_re_t, b_im_t, c_re, c_im):
    g, _, n = a_re.shape
    p = b_re_t.shape[1]
    gl = g // SSM_BLOCKS
    dense = jax.ShapeDtypeStruct((SSM_BLOCKS, gl * p, gl * n), BF16)
    return pl.pallas_call(
        _ssm_params_kernel,
        out_shape=[jax.ShapeDtypeStruct((g, 1, n), F32), jax.ShapeDtypeStruct((g, 1, n), F32),
                   dense, dense, dense, dense],
        compiler_params=pltpu.CompilerParams(vmem_limit_bytes=V7X_VMEM_LIMIT_BYTES),
        name="ssm_params",
    )(a_re, a_im, log_dt, b_re_t, b_im_t, c_re, c_im)


SSM_COL_CHUNK = 512
SSM_BLOCKS = 4


def _ssm_input_drive(u_ref, lhs_ref, bmat_re_ref, bmat_im_ref, bu_re_ref, bu_im_ref, *, tb):
    n_seg = u_ref.shape[1]
    n_slabs = lhs_ref.shape[0]
    for seg in range(n_seg):
        for c in range(n_slabs):
            lhs_ref[c, pl.ds(seg, tb, stride=n_seg), :] = u_ref[0, seg, :, c * V7X_LANES:(c + 1) * V7X_LANES]
    per_block = n_slabs // SSM_BLOCKS
    kst = bu_re_ref.shape[1] // SSM_BLOCKS
    for j in range(SSM_BLOCKS):
        uj = jnp.concatenate([lhs_ref[c] for c in range(j * per_block, (j + 1) * per_block)],
                             axis=-1).astype(BF16)
        bu_re_ref[:, j * kst:(j + 1) * kst] = jnp.dot(uj, bmat_re_ref[j], preferred_element_type=F32)
        bu_im_ref[:, j * kst:(j + 1) * kst] = jnp.dot(uj, bmat_im_ref[j], preferred_element_type=F32)


def _ssm_recurrence(ab_re_ref, ab_im_ref, st_re_ref, st_im_ref, bu_re_ref, bu_im_ref, *, tb,
                    x_re_ref=None, x_im_ref=None):
    n_cols = st_re_ref.shape[1]
    two = 2 * V7X_SUBLANES
    for c in range(n_cols // SSM_COL_CHUNK):
        cs = slice(c * SSM_COL_CHUNK, (c + 1) * SSM_COL_CHUNK)
        ar = jnp.broadcast_to(ab_re_ref[:, cs], (V7X_SUBLANES, SSM_COL_CHUNK))
        ai = jnp.broadcast_to(ab_im_ref[:, cs], (V7X_SUBLANES, SSM_COL_CHUNK))

        def step(t2, carry, cs=cs, ar=ar, ai=ai):
            xr, xi = carry
            rows = pl.ds(pl.multiple_of(t2 * two, two), two)
            bur = bu_re_ref[rows, cs]
            bui = bu_im_ref[rows, cs]
            xr1 = ar * xr - ai * xi + bur[:V7X_SUBLANES]
            xi1 = ar * xi + ai * xr + bui[:V7X_SUBLANES]
            xr2 = ar * xr1 - ai * xi1 + bur[V7X_SUBLANES:]
            xi2 = ar * xi1 + ai * xr1 + bui[V7X_SUBLANES:]
            if x_re_ref is not None:
                x_re_ref[rows, cs] = jnp.concatenate([xr1, xr2], axis=0).astype(BF16)
                x_im_ref[rows, cs] = jnp.concatenate([xi1, xi2], axis=0).astype(BF16)
            return xr2, xi2

        xr, xi = lax.fori_loop(0, tb // 2, step, (st_re_ref[:, cs], st_im_ref[:, cs]), unroll=2)
        st_re_ref[:, cs] = xr
        st_im_ref[:, cs] = xi


def _ssm_ends_kernel(u_ref, bmat_re_ref, bmat_im_ref, ab_re_ref, ab_im_ref, end_re_ref, end_im_ref,
                     lhs_ref, bu_re_ref, bu_im_ref, st_re_ref, st_im_ref, *, tb):
    tblk = pl.program_id(1)

    @pl.when(tblk == 0)
    def _():
        st_re_ref[...] = jnp.zeros_like(st_re_ref)
        st_im_ref[...] = jnp.zeros_like(st_im_ref)

    _ssm_input_drive(u_ref, lhs_ref, bmat_re_ref, bmat_im_ref, bu_re_ref, bu_im_ref, tb=tb)
    _ssm_recurrence(ab_re_ref, ab_im_ref, st_re_ref, st_im_ref, bu_re_ref, bu_im_ref, tb=tb)

    @pl.when(tblk == pl.num_programs(1) - 1)
    def _():
        end_re_ref[0] = st_re_ref[...]
        end_im_ref[0] = st_im_ref[...]


def _ssm_main_kernel(u_ref, end_re_ref, end_im_ref, bmat_re_ref, bmat_im_ref, ab_re_ref, ab_im_ref,
                     cmat_re_ref, cmat_im_ref, d_ref, wglu_ref, bglu_ref, o_ref,
                     lhs_ref, bu_re_ref, bu_im_ref, st_re_ref, st_im_ref, x_re_ref, x_im_ref, *, tb, seg_len):
    tblk = pl.program_id(1)
    n_seg = st_re_ref.shape[0]

    @pl.when(tblk == 0)
    def _():
        pr = ab_re_ref[...]
        pi = ab_im_ref[...]
        for _ in range(int(math.log2(seg_len))):
            pr, pi = pr * pr - pi * pi, 2.0 * pr * pi
        er = jnp.zeros_like(pr)
        ei = jnp.zeros_like(pi)
        st_re_ref[0:1, :] = er
        st_im_ref[0:1, :] = ei
        for j in range(1, n_seg):
            lr = end_re_ref[0, j - 1:j, :]
            li = end_im_ref[0, j - 1:j, :]
            er, ei = lr + pr * er - pi * ei, li + pr * ei + pi * er
            st_re_ref[j:j + 1, :] = er
            st_im_ref[j:j + 1, :] = ei

    _ssm_input_drive(u_ref, lhs_ref, bmat_re_ref, bmat_im_ref, bu_re_ref, bu_im_ref, tb=tb)
    _ssm_recurrence(ab_re_ref, ab_im_ref, st_re_ref, st_im_ref, bu_re_ref, bu_im_ref, tb=tb,
                    x_re_ref=x_re_ref, x_im_ref=x_im_ref)

    n_slabs = lhs_ref.shape[0]
    kst = bu_re_ref.shape[1] // SSM_BLOCKS
    ys = []
    for j in range(SSM_BLOCKS):
        xr = x_re_ref[:, j * kst:(j + 1) * kst]
        xi = x_im_ref[:, j * kst:(j + 1) * kst]
        nt = (((1,), (1,)), ((), ()))
        ys.append(lax.dot_general(xr, cmat_re_ref[j], nt, preferred_element_type=F32)
                  - lax.dot_general(xi, cmat_im_ref[j], nt, preferred_element_type=F32))
    u_rows = jnp.concatenate([lhs_ref[c] for c in range(n_slabs)], axis=-1)
    y = jnp.concatenate(ys, axis=-1) + d_ref[...] * u_rows
    z = jax.nn.gelu(y)
    gate = jnp.dot(z.astype(BF16), wglu_ref[...], preferred_element_type=F32) + bglu_ref[...]
    s = z * jax.nn.sigmoid(gate)
    for c in range(n_slabs):
        lhs_ref[c] = s[:, c * V7X_LANES:(c + 1) * V7X_LANES]
    for seg in range(n_seg):
        for c in range(n_slabs):
            o_ref[0, seg, :, c * V7X_LANES:(c + 1) * V7X_LANES] = lhs_ref[c, pl.ds(seg, tb, stride=n_seg), :]


def _ssm(u_seg, bmat_re, bmat_im, ab_re, ab_im, cmat_re, cmat_im, d_skip, w_glu, b_glu, *, tb):
    batch, n_seg, seg_len, d_ssm = u_seg.shape
    n_cols = ab_re.shape[1]
    rows = tb * n_seg
    grid = (batch, seg_len // tb)
    u_spec = pl.BlockSpec((1, n_seg, tb, d_ssm), lambda b, t: (b, 0, t, 0))
    end_spec = pl.BlockSpec((1, n_seg, n_cols), lambda b, t: (b, 0, 0))

    def whole(a):
        return pl.BlockSpec(a.shape, lambda b, t: (0,) * a.ndim, pipeline_mode=pl.Buffered(1))

    scratch = [pltpu.VMEM((d_ssm // V7X_LANES, rows, V7X_LANES), F32), pltpu.VMEM((rows, n_cols), F32),
               pltpu.VMEM((rows, n_cols), F32), pltpu.VMEM((n_seg, n_cols), F32),
               pltpu.VMEM((n_seg, n_cols), F32)]
    end_re, end_im = pl.pallas_call(
        functools.partial(_ssm_ends_kernel, tb=tb),
        grid=grid,
        in_specs=[u_spec, whole(bmat_re), whole(bmat_im), whole(ab_re), whole(ab_im)],
        out_specs=[end_spec, end_spec],
        out_shape=[jax.ShapeDtypeStruct((batch, n_seg, n_cols), F32)] * 2,
        scratch_shapes=scratch,
        compiler_params=_params(("parallel", "arbitrary")),
        name="ssm_ends",
    )(u_seg, bmat_re, bmat_im, ab_re, ab_im)
    return pl.pallas_call(
        functools.partial(_ssm_main_kernel, tb=tb, seg_len=seg_len),
        grid=grid,
        in_specs=[u_spec, end_spec, end_spec, whole(bmat_re), whole(bmat_im), whole(ab_re), whole(ab_im),
                  whole(cmat_re), whole(cmat_im), whole(d_skip), whole(w_glu), whole(b_glu)],
        out_specs=u_spec,
        out_shape=jax.ShapeDtypeStruct(u_seg.shape, F32),
        scratch_shapes=scratch + [pltpu.VMEM((rows, n_cols), BF16), pltpu.VMEM((rows, n_cols), BF16)],
        compiler_params=_params(("parallel", "arbitrary")),
        name="ssm_main",
    )(u_seg, end_re, end_im, bmat_re, bmat_im, ab_re, ab_im, cmat_re, cmat_im, d_skip, w_glu, b_glu)


def _out_proj_kernel(a_ref, s_ref, wa_ref, ws_ref, xh_ref, xt_ref, g_ref, o_ref, *, n_head):
    m = (jnp.dot(a_ref[...], wa_ref[...], preferred_element_type=F32)
         + jnp.dot(s_ref[...].astype(BF16), ws_ref[...], preferred_element_type=F32))
    o_ref[...] = _row_tile(xh_ref, xt_ref, n_head) + m * _rms_scale(m) * g_ref[...]


def _out_proj(a, s, w, x_parts, g, *, tm):
    t = a.shape[0]
    d = w.shape[1]
    da = a.shape[1]
    ds = s.shape[1]
    assert da == ds and w.shape[0] == da + ds
    x_specs, n_head = _row_tile_specs(x_parts, tm)
    return pl.pallas_call(
        functools.partial(_out_proj_kernel, n_head=n_head),
        grid=(t // tm,),
        in_specs=[
            pl.BlockSpec((tm, da), lambda i: (i, 0)),
            pl.BlockSpec((tm, ds), lambda i: (i, 0)),
            pl.BlockSpec((da, d), lambda i: (0, 0)),
            pl.BlockSpec((ds, d), lambda i: (1, 0)),
        ] + x_specs + [
            pl.BlockSpec((1, d), lambda i: (0, 0)),
        ],
        out_specs=pl.BlockSpec((tm, d), lambda i: (i, 0)),
        out_shape=jax.ShapeDtypeStruct((t, d), F32),
        compiler_params=_params(("parallel",)),
        name="out_proj",
    )(a, s, w, w, *x_parts, g)


def kernel(x, ffn1_pre_g, ffn1_w_gate, ffn1_w_up, ffn1_w_down, ffn1_post_g, mix_pre_g, w_in, lambda_q1, lambda_k1, lambda_q2, lambda_k2, subln_g, rel_bias, ssm_a_re, ssm_a_im, ssm_b_re, ssm_b_im, ssm_c_re, ssm_c_im, ssm_d, ssm_log_dt, w_glu, b_glu, w_out, mix_post_g, ffn2_pre_g, ffn2_w_gate, ffn2_w_up, ffn2_w_down, ffn2_post_g):
    batch, seq, d_model = x.shape
    depth = ffn1_pre_g.shape[0]
    assert depth == 1, "LAMBDA_INIT is specialised to a single layer"
    n_heads = rel_bias.shape[1]
    d_attn = n_heads * V_HEAD_DIM
    n_groups, n_state = ssm_a_re.shape[1:]
    d_ssm = n_groups * SSM_GROUP
    n_seg = V7X_SUBLANES
    seg_len = seq // n_seg
    tokens = batch * seq
    tm, tf, tq, tb = 512, 512, 256, 64
    tm_first = 2 * tm

    xt = x.reshape(tokens, d_model)
    row = lambda v: v.reshape(1, -1)
    l = 0

    x1_head, wg1, wu1, wd1, w_in_bf, w_glu_bf = _ffn_first(
        xt, row(ffn1_pre_g[l]), ffn1_w_gate[l], ffn1_w_up[l], ffn1_w_down[l], row(ffn1_post_g[l]),
        tm=tm_first, tf=tf // 2, cast=(w_in[l], w_glu[l]))
    x1_tail = _ffn(xt, row(ffn1_pre_g[l]), wg1, wu1, wd1, row(ffn1_post_g[l]), tm=tm, tf=tf,
                   rows_done=tm_first)
    x1 = (x1_head, x1_tail)

    qkv, u = _in_proj(x1, row(mix_pre_g[l]), w_in_bf, tm=tm, q_scale=ATTN_HEAD_DIM ** -0.5 * LOG2_E)

    band = _bias_band(rel_bias, tq=tq, scale=LOG2_E)
    a, wg2, wu2, wd2, wo = _attention(
        qkv, band, row(lambda_q1[l]), row(lambda_k1[l]), row(lambda_q2[l]), row(lambda_k2[l]),
        subln_g[l].reshape(-1, 1), batch=batch, seq=seq, n_heads=n_heads, tq=tq,
        cast=(ffn2_w_gate[l], ffn2_w_up[l], ffn2_w_down[l], w_out[l]))

    ab_re, ab_im, bmat_re, bmat_im, cmat_re, cmat_im = _ssm_params(
        ssm_a_re[l].reshape(n_groups, 1, n_state), ssm_a_im[l].reshape(n_groups, 1, n_state),
        ssm_log_dt[l].reshape(n_groups, 1, 1),
        jnp.swapaxes(ssm_b_re[l], 1, 2), jnp.swapaxes(ssm_b_im[l], 1, 2), ssm_c_re[l], ssm_c_im[l])
    u_seg = u.reshape(batch, n_seg, seg_len, d_ssm)
    s_seg = _ssm(u_seg, bmat_re, bmat_im, ab_re.reshape(1, -1), ab_im.reshape(1, -1), cmat_re, cmat_im,
                 row(ssm_d[l]), w_glu_bf, row(b_glu[l]), tb=tb)
    s = s_seg.reshape(tokens, d_ssm)

    x2 = _out_proj(a.reshape(tokens, d_attn), s, wo, x1, row(mix_post_g[l]), tm=tm)

    x3 = _ffn(x2, row(ffn2_pre_g[l]), wg2, wu2, wd2, row(ffn2_post_g[l]), tm=tm, tf=tf)
    return x3.reshape(batch, seq, d_model)
```

```python
import functools
import math

import jax
import jax.numpy as jnp
from jax import lax
from jax.experimental import pallas as pl
from jax.experimental.pallas import tpu as pltpu

V7X_LANES = 128
V7X_SUBLANES = 8
V7X_MXU_COLS = 256
V7X_VMEM_LIMIT_BYTES = 56 * 1024 * 1024

RMS_EPS = 1e-6
NEG_INF = -1e30
N_BUCKETS = 32
MAX_DISTANCE = 128
ATTN_HEAD_DIM = 64
V_HEAD_DIM = 128
SSM_GROUP = 16
SSM_STATE = 64
LAMBDA_INIT = 0.8 - 0.6 * math.exp(-0.3 * 0)

F32 = jnp.float32
BF16 = jnp.bfloat16
EPILOGUE_ROWS = 128


def _params(semantics):
    return pltpu.CompilerParams(dimension_semantics=semantics,
                                vmem_limit_bytes=V7X_VMEM_LIMIT_BYTES)


def _rms_scale(v):
    return lax.rsqrt(jnp.mean(v * v, axis=-1, keepdims=True) + RMS_EPS)


def _ffn_prologue(x_ref, pre_g_ref, xn_ref, acc_ref):
    xf = x_ref[...]
    xn_ref[...] = (xf * _rms_scale(xf) * pre_g_ref[...]).astype(BF16)
    acc_ref[...] = jnp.zeros_like(acc_ref)


def _ffn_chunk(xn_ref, acc_ref, wg_ref, wu_ref, wd_ref):
    xn = xn_ref[...]
    n_split = 2 if wg_ref.shape[1] >= 2 * V7X_MXU_COLS else 1
    half = wg_ref.shape[1] // n_split
    pre = []
    for c in range(n_split):
        cols = slice(c * half, (c + 1) * half)
        pre.append((jnp.dot(xn, wg_ref[:, cols], preferred_element_type=F32),
                    jnp.dot(xn, wu_ref[:, cols], preferred_element_type=F32)))
    out = acc_ref[...]
    for c, (gate, up) in enumerate(pre):
        hidden = (gate * jax.nn.sigmoid(gate) * up).astype(BF16)
        out = out + jnp.dot(hidden, wd_ref[c * half:(c + 1) * half, :], preferred_element_type=F32)
    acc_ref[...] = out


def _ffn_epilogue(x_ref, acc_ref, post_g_ref, o_ref):
    scale = _rms_scale(acc_ref[...])
    g_half = 0.5 * post_g_ref[...]
    for r in range(0, acc_ref.shape[0], EPILOGUE_ROWS):
        rows = slice(r, r + EPILOGUE_ROWS)
        o_ref[rows, :] = x_ref[rows, :] + acc_ref[rows, :] * scale[rows] * g_half


def _ffn_kernel(x_ref, pre_g_ref, wg_ref, wu_ref, wd_ref, post_g_ref, o_ref, xn_ref, acc_ref):
    k = pl.program_id(1)

    @pl.when(k == 0)
    def _():
        _ffn_prologue(x_ref, pre_g_ref, xn_ref, acc_ref)

    _ffn_chunk(xn_ref, acc_ref, wg_ref, wu_ref, wd_ref)

    @pl.when(k == pl.num_programs(1) - 1)
    def _():
        _ffn_epilogue(x_ref, acc_ref, post_g_ref, o_ref)


def _ffn_first_kernel(x_ref, pre_g_ref, wg32_ref, wu32_ref, wd32_ref, post_g_ref, *rest, cast_steps):
    n_cast = len(cast_steps)
    cast_in = rest[:n_cast]
    o_ref, wg_ref, wu_ref, wd_ref = rest[n_cast:n_cast + 4]
    cast_out = rest[n_cast + 4:2 * n_cast + 4]
    xn_ref, acc_ref = rest[2 * n_cast + 4:]
    k = pl.program_id(0)

    @pl.when(k == 0)
    def _():
        _ffn_prologue(x_ref, pre_g_ref, xn_ref, acc_ref)

    for src, dst, steps in zip(cast_in, cast_out, cast_steps):
        @pl.when(k < steps)
        def _(src=src, dst=dst):
            dst[...] = src[...].astype(BF16)

    wg = wg32_ref[...].astype(BF16)
    wu = wu32_ref[...].astype(BF16)
    wd = wd32_ref[...].astype(BF16)
    wg_ref[...] = wg
    wu_ref[...] = wu
    wd_ref[...] = wd
    _ffn_chunk(xn_ref, acc_ref, wg, wu, wd)

    @pl.when(k == pl.num_programs(0) - 1)
    def _():
        _ffn_epilogue(x_ref, acc_ref, post_g_ref, o_ref)


def _ffn_first(x, pre_g, wg32, wu32, wd32, post_g, *, tm, tf, cast=()):
    t, d = x.shape
    f = wg32.shape[1]
    n_chunks = f // tf
    cast_steps = []
    for w in cast:
        steps = 1
        while steps * 2 <= n_chunks and w.shape[0] % (16 * steps * 2) == 0:
            steps *= 2
        cast_steps.append(steps)
    cast_specs = [pl.BlockSpec((w.shape[0] // steps, w.shape[1]),
                               lambda k, steps=steps: (jnp.minimum(k, steps - 1), 0))
                  for w, steps in zip(cast, cast_steps)]
    w_specs = [pl.BlockSpec((d, tf), lambda k: (0, k)), pl.BlockSpec((d, tf), lambda k: (0, k)),
               pl.BlockSpec((tf, d), lambda k: (k, 0))]
    return pl.pallas_call(
        functools.partial(_ffn_first_kernel, cast_steps=tuple(cast_steps)),
        grid=(n_chunks,),
        in_specs=[pl.BlockSpec((tm, d), lambda k: (0, 0), pipeline_mode=pl.Buffered(1)),
                  pl.BlockSpec((1, d), lambda k: (0, 0))]
                 + w_specs + [pl.BlockSpec((1, d), lambda k: (0, 0))] + cast_specs,
        out_specs=[pl.BlockSpec((tm, d), lambda k: (0, 0), pipeline_mode=pl.Buffered(1))] + w_specs + cast_specs,
        out_shape=[jax.ShapeDtypeStruct((tm, d), F32)]
                  + [jax.ShapeDtypeStruct(w.shape, BF16) for w in (wg32, wu32, wd32) + tuple(cast)],
        scratch_shapes=[pltpu.VMEM((tm, d), BF16), pltpu.VMEM((tm, d), F32)],
        compiler_params=_params(("arbitrary",)),
        name="ffn_first",
    )(x, pre_g, wg32, wu32, wd32, post_g, *cast)


def _ffn(x, pre_g, wg, wu, wd, post_g, *, tm, tf, rows_done=0):
    t, d = x.shape
    f = wg.shape[1]
    assert rows_done % tm == 0
    skip = rows_done // tm
    return pl.pallas_call(
        _ffn_kernel,
        grid=(t // tm - skip, f // tf),
        in_specs=[
            pl.BlockSpec((tm, d), lambda i, k: (i + skip, 0)),
            pl.BlockSpec((1, d), lambda i, k: (0, 0)),
            pl.BlockSpec((d, tf), lambda i, k: (0, k)),
            pl.BlockSpec((d, tf), lambda i, k: (0, k)),
            pl.BlockSpec((tf, d), lambda i, k: (k, 0)),
            pl.BlockSpec((1, d), lambda i, k: (0, 0)),
        ],
        out_specs=pl.BlockSpec((tm, d), lambda i, k: (i, 0)),
        out_shape=jax.ShapeDtypeStruct((t - rows_done, d), F32),
        scratch_shapes=[pltpu.VMEM((tm, d), BF16), pltpu.VMEM((tm, d), F32)],
        compiler_params=_params(("parallel", "arbitrary")),
        name="ffn",
    )(x, pre_g, wg, wu, wd, post_g)


def _row_tile_specs(parts, tm):
    head, tail = parts
    n_head = head.shape[0] // tm
    assert head.shape[0] % tm == 0 and tail.shape[0] % tm == 0 and head.shape[1] == tail.shape[1]
    d = head.shape[1]
    specs = [pl.BlockSpec((tm, d), lambda i: (jnp.minimum(i, n_head - 1), 0)),
             pl.BlockSpec((tm, d), lambda i: (jnp.maximum(i - n_head, 0), 0))]
    return specs, n_head


def _row_tile(head_ref, tail_ref, n_head):
    return jnp.where(pl.program_id(0) < n_head, head_ref[...], tail_ref[...])


def _in_proj_kernel(xh_ref, xt_ref, g_ref, w_ref, qkv_ref, u_ref, *, q_scale, n_head):
    xf = _row_tile(xh_ref, xt_ref, n_head)
    hn = (xf * _rms_scale(xf) * g_ref[...]).astype(BF16)
    tn = u_ref.shape[1]
    for n in range(4):
        acc = jnp.dot(hn, w_ref[:, n * tn:(n + 1) * tn], preferred_element_type=F32)
        if n == 0:
            qkv_ref[:, :tn] = (acc * q_scale).astype(BF16)
        elif n < 3:
            qkv_ref[:, n * tn:(n + 1) * tn] = acc.astype(BF16)
        else:
            u_ref[...] = acc


def _in_proj(x_parts, g, w, *, tm, q_scale):
    t = x_parts[0].shape[0] + x_parts[1].shape[0]
    d = x_parts[0].shape[1]
    tn = w.shape[1] // 4
    x_specs, n_head = _row_tile_specs(x_parts, tm)
    return pl.pallas_call(
        functools.partial(_in_proj_kernel, q_scale=q_scale, n_head=n_head),
        grid=(t // tm,),
        in_specs=x_specs + [
            pl.BlockSpec((1, d), lambda i: (0, 0)),
            pl.BlockSpec(w.shape, lambda i: (0, 0), pipeline_mode=pl.Buffered(1)),
        ],
        out_specs=[
            pl.BlockSpec((tm, 3 * tn), lambda i: (i, 0)),
            pl.BlockSpec((tm, tn), lambda i: (i, 0)),
        ],
        out_shape=[jax.ShapeDtypeStruct((t, 3 * tn), BF16),
                   jax.ShapeDtypeStruct((t, tn), F32)],
        compiler_params=_params(("parallel",)),
        name="in_proj",
    )(*x_parts, g, w)


def _bias_band_kernel(rb_ref, o_ref, *, tq, scale):
    h = pl.program_id(0)
    key = lax.broadcasted_iota(jnp.int32, (3 * tq, tq), 0)
    qry = lax.broadcasted_iota(jnp.int32, (3 * tq, tq), 1)
    dist = 2 * tq + qry - key
    n = jnp.maximum(dist, 0)
    max_exact = N_BUCKETS // 2
    nf = jnp.maximum(n, 1).astype(F32)
    large = max_exact + (jnp.log(nf / max_exact) / math.log(MAX_DISTANCE / max_exact)
                         * (N_BUCKETS - max_exact)).astype(jnp.int32)
    large = jnp.minimum(large, N_BUCKETS - 1)
    bucket = jnp.where(n < max_exact, n, large)
    val = jnp.zeros((3 * tq, tq), F32)
    for b in range(N_BUCKETS):
        val = jnp.where(bucket == b, rb_ref[b, h], val)
    o_ref[0] = jnp.where(dist >= 0, val * scale, NEG_INF)


def _bias_band(rel_bias, *, tq, scale):
    assert tq >= MAX_DISTANCE
    n_heads = rel_bias.shape[1]
    return pl.pallas_call(
        functools.partial(_bias_band_kernel, tq=tq, scale=scale),
        grid=(n_heads,),
        in_specs=[pl.BlockSpec(memory_space=pltpu.SMEM)],
        out_specs=pl.BlockSpec((1, 3 * tq, tq), lambda h: (h, 0, 0)),
        out_shape=jax.ShapeDtypeStruct((n_heads, 3 * tq, tq), F32),
        compiler_params=_params(("arbitrary",)),
        name="bias_band",
    )(rel_bias)


N_STREAMS = 4
VT_ROWS = V_HEAD_DIM + 16
LOG2_E = math.log2(math.e)


def _attn_kernel(lq1_ref, lk1_ref, lq2_ref, lk2_ref, q1_ref, q2_ref, k1_ref, k2_ref,
                 v_ref, band_ref, g_ref, *rest, tq, cast_rows):
    n_cast = len(cast_rows)
    cast_in = rest[:n_cast]
    o_ref = rest[n_cast]
    cast_out = rest[n_cast + 1:2 * n_cast + 1]
    vt_ref, qm_ref, s_ref, m_ref, acc_ref = rest[2 * n_cast + 1:]
    n_blocks = v_ref.shape[1] // tq
    n_pairs_total = n_blocks * (n_blocks + 1) // 2
    assert n_pairs_total % 2 == 0
    q_refs = (q1_ref, q2_ref)
    k_refs = (k1_ref, k2_ref)
    streams = [(hh, mi) for hh in range(2) for mi in range(2)]

    ones_pad = (lax.broadcasted_iota(jnp.int32, (VT_ROWS - V_HEAD_DIM, tq), 0) == 0).astype(BF16)
    feat = lax.broadcasted_iota(jnp.int32, (2 * ATTN_HEAD_DIM, tq), 0)
    for c in range(n_blocks):
        rows = slice(c * tq, (c + 1) * tq)
        vt = v_ref[0, rows, :].astype(F32).T.astype(BF16)
        for hh in range(2):
            vt_ref[c, hh, :V_HEAD_DIM, :] = vt[hh * V_HEAD_DIM:(hh + 1) * V_HEAD_DIM]
            vt_ref[c, hh, V_HEAD_DIM:, :] = ones_pad
        for mi in range(2):
            qt = q_refs[mi][0, rows, :].astype(F32).T
            for hh in range(2):
                qm_ref[c, mi, :, hh * tq:(hh + 1) * tq] = jnp.where(
                    (feat >= ATTN_HEAD_DIM) == (hh == 1), qt, 0.0).astype(BF16)

    def reset_state():
        m_ref[...] = jnp.full(m_ref.shape, NEG_INF, F32)
        acc_ref[...] = jnp.zeros(acc_ref.shape, F32)

    reset_state()
    lam = (jnp.exp(jnp.sum(lq1_ref[...] * lk1_ref[...], axis=-1, keepdims=True))
           - jnp.exp(jnp.sum(lq2_ref[...] * lk2_ref[...], axis=-1, keepdims=True)) + LAMBDA_INIT)

    def score_phase(i, j, buf):
        rows = pl.ds(pl.multiple_of(j * tq, tq), tq)
        start = jnp.where(j == i, 2 * tq, jnp.where(j == i - 1, tq, 0))
        band_rows = pl.ds(pl.multiple_of(start, tq), tq)
        band2 = jnp.concatenate([band_ref[0, band_rows, :], band_ref[1, band_rows, :]], axis=1)
        for mi in range(2):
            s_ref[buf, mi] = jnp.dot(k_refs[mi][0, rows, :], qm_ref[i, mi],
                                     preferred_element_type=F32) + band2

    def value_phase(j, buf):
        alphas, ps = [], []
        for mi in range(2):
            s = s_ref[buf, mi]
            m_old = jnp.where(j == 0, NEG_INF, m_ref[mi])
            m_new = jnp.maximum(m_old, jnp.max(s, axis=0, keepdims=True))
            alphas.append(jnp.exp2(m_old - m_new))
            ps.append(jnp.exp2(s - m_new).astype(BF16))
            m_ref[mi] = m_new
        for hh in range(2):
            head = slice(hh * tq, (hh + 1) * tq)
            p2 = jnp.concatenate([ps[0][:, head], ps[1][:, head]], axis=1)
            alpha2 = jnp.concatenate([alphas[0][:, head], alphas[1][:, head]], axis=1)
            acc_ref[buf, hh] = alpha2 * acc_ref[1 - buf, hh] + jnp.dot(
                vt_ref[j, hh], p2, preferred_element_type=F32)

    def finish_block(i, buf):
        out_rows = pl.ds(pl.multiple_of(i * tq, tq), tq)
        den = V_HEAD_DIM
        for hh in range(2):
            a = (acc_ref[buf, hh, :den, :tq] / acc_ref[buf, hh, den:den + 1, :tq]
                 - lam * (acc_ref[buf, hh, :den, tq:] / acc_ref[buf, hh, den:den + 1, tq:]))
            scale = lax.rsqrt(jnp.mean(a * a, axis=0, keepdims=True) + RMS_EPS)
            a = a * scale * g_ref[...] * (1.0 - LAMBDA_INIT)
            o_ref[0, out_rows, hh * V_HEAD_DIM:(hh + 1) * V_HEAD_DIM] = a.T.astype(o_ref.dtype)

    def half_step(i, j, parity):
        last_chunk = j == i
        ni = jnp.where(last_chunk, i + 1, i)
        nj = jnp.where(last_chunk, 0, j + 1)
        past_end = ni >= n_blocks
        score_phase(jnp.where(past_end, i, ni), jnp.where(past_end, j, nj), 1 - parity)
        value_phase(j, parity)
        return ni, nj

    def sweep_body(it, carry):
        for src, dst, rows_per_it in zip(cast_in, cast_out, cast_rows):
            blk = jnp.minimum(it, src.shape[0] // rows_per_it - 1)
            rows = pl.ds(pl.multiple_of(blk * rows_per_it, rows_per_it), rows_per_it)
            dst[rows, :] = src[rows, :].astype(BF16)
        i0, j0 = carry
        i1, j1 = half_step(i0, j0, 0)
        i2, j2 = half_step(i1, j1, 1)
        for parity, (i, j) in enumerate(((i0, j0), (i1, j1))):
            @pl.when(j == i)
            def _(parity=parity, i=i):
                finish_block(i, parity)
        return i2, j2

    score_phase(0, 0, 0)
    lax.fori_loop(0, n_pairs_total // 2, sweep_body, (jnp.int32(0), jnp.int32(0)))


def _attention(qkv, band, lq1, lk1, lq2, lk2, subln_g, *, batch, seq, n_heads, tq, cast=()):
    d_attn = n_heads * V_HEAD_DIM
    pair = 2 * ATTN_HEAD_DIM
    vpair = 2 * V_HEAD_DIM
    n_pairs = d_attn // 2 // pair
    n_blocks = seq // tq
    n_steps = batch * n_pairs
    n_iters = n_blocks * (n_blocks + 1) // 4
    cast_rows = []
    for w in cast:
        block_rows = w.shape[0] // n_steps
        rows_per_it = 16
        while block_rows % rows_per_it or block_rows // rows_per_it > n_iters:
            rows_per_it += 16
        assert w.shape[0] % n_steps == 0 and rows_per_it <= block_rows
        cast_rows.append(rows_per_it)
    cast_specs = [pl.BlockSpec((w.shape[0] // n_steps, w.shape[1]), lambda b, p: (b * n_pairs + p, 0))
                  for w in cast]
    qkv3 = qkv.reshape(batch, seq, 3 * d_attn)
    small = pl.BlockSpec((1, ATTN_HEAD_DIM), lambda b, p: (0, 0))
    return pl.pallas_call(
        functools.partial(_attn_kernel, tq=tq, cast_rows=tuple(cast_rows)),
        grid=(batch, n_pairs),
        in_specs=[
            small, small, small, small,
            pl.BlockSpec((1, seq, pair), lambda b, p: (b, 0, p)),
            pl.BlockSpec((1, seq, pair), lambda b, p: (b, 0, n_pairs + p)),
            pl.BlockSpec((1, seq, pair), lambda b, p: (b, 0, 2 * n_pairs + p)),
            pl.BlockSpec((1, seq, pair), lambda b, p: (b, 0, 3 * n_pairs + p)),
            pl.BlockSpec((1, seq, vpair), lambda b, p: (b, 0, 2 * n_pairs + p)),
            pl.BlockSpec((2, 3 * tq, tq), lambda b, p: (p, 0, 0)),
            pl.BlockSpec((V_HEAD_DIM, 1), lambda b, p: (0, 0)),
        ] + cast_specs,
        out_specs=[pl.BlockSpec((1, seq, vpair), lambda b, p: (b, 0, p))] + cast_specs,
        out_shape=[jax.ShapeDtypeStruct((batch, seq, d_attn), BF16)]
                  + [jax.ShapeDtypeStruct(w.shape, BF16) for w in cast],
        scratch_shapes=[pltpu.VMEM((n_blocks, 2, VT_ROWS, tq), BF16),
                        pltpu.VMEM((n_blocks, 2, pair, 2 * tq), BF16),
                        pltpu.VMEM((2, 2, tq, 2 * tq), F32),
                        pltpu.VMEM((2, 1, 2 * tq), F32),
                        pltpu.VMEM((2, 2, VT_ROWS, 2 * tq), F32)],
        compiler_params=_params(("parallel", "parallel")),
        name="diff_attention",
    )(lq1, lk1, lq2, lk2, qkv3, qkv3, qkv3, qkv3, qkv3, band, subln_g, *cast)


def _block_diag_blocks(w, dst_ref):
    n_blocks, rows, cols = dst_ref.shape
    g, p, n = w.shape
    gl = g // n_blocks
    row_group = lax.broadcasted_iota(jnp.int32, (rows, cols), 0) // p
    col_group = lax.broadcasted_iota(jnp.int32, (rows, cols), 1) // n
    for j in range(n_blocks):
        stacked = w[j * gl:(j + 1) * gl].reshape(rows, n)
        tiled = jnp.concatenate([stacked] * gl, axis=1)
        dst_ref[j] = jnp.where(row_group == col_group, tiled, 0.0).astype(BF16)


def _ssm_params_kernel(a_re_ref, a_im_ref, log_dt_ref, b_re_ref, b_im_ref, c_re_ref, c_im_ref,
                       ab_re_ref, ab_im_ref, bd_re_ref, bd_im_ref, cd_re_ref, cd_im_ref):
    ar = a_re_ref[...]
    ai = a_im_ref[...]
    dt = jnp.exp(log_dt_ref[...])
    decay = jnp.exp(dt * ar)
    ab_re = decay * jnp.cos(dt * ai)
    ab_im = decay * jnp.sin(dt * ai)
    den = ar * ar + ai * ai
    nr = ab_re - 1.0
    ni = ab_im
    coef_re = (nr * ar + ni * ai) / den
    coef_im = (ni * ar - nr * ai) / den
    br = b_re_ref[...]
    bi = b_im_ref[...]
    ab_re_ref[...] = ab_re
    ab_im_ref[...] = ab_im
    _block_diag_blocks(coef_re * br - coef_im * bi, bd_re_ref)
    _block_diag_blocks(coef_re * bi + coef_im * br, bd_im_ref)
    _block_diag_blocks(c_re_ref[...], cd_re_ref)
    _block_diag_blocks(c_im_ref[...], cd_im_ref)


def _ssm_params(a_re, a_im, log_dt, b_re_t, b_im_t, c_re, c_im):
    g, _, n = a_re.shape
    p = b_re_t.shape[1]
    gl = g // SSM_BLOCKS
    dense = jax.ShapeDtypeStruct((SSM_BLOCKS, gl * p, gl * n), BF16)
    return pl.pallas_call(
        _ssm_params_kernel,
        out_shape=[jax.ShapeDtypeStruct((g, 1, n), F32), jax.ShapeDtypeStruct((g, 1, n), F32),
                   dense, dense, dense, dense],
        compiler_params=pltpu.CompilerParams(vmem_limit_bytes=V7X_VMEM_LIMIT_BYTES),
        name="ssm_params",
    )(a_re, a_im, log_dt, b_re_t, b_im_t, c_re, c_im)


SSM_COL_CHUNK = 512
SSM_BLOCKS = 4


def _ssm_fill_lhs(u_ref, lhs_ref, *, tb):
    n_seg = u_ref.shape[1]
    for seg in range(n_seg):
        for c in range(lhs_ref.shape[0]):
            lhs_ref[c, pl.ds(seg, tb, stride=n_seg), :] = u_ref[0, seg, :, c * V7X_LANES:(c + 1) * V7X_LANES]


def _ssm_drive_block(j, lhs_ref, bmat_re_ref, bmat_im_ref, bu_re_ref, bu_im_ref):
    per_block = lhs_ref.shape[0] // SSM_BLOCKS
    kst = bu_re_ref.shape[1] // SSM_BLOCKS
    uj = jnp.concatenate([lhs_ref[c] for c in range(j * per_block, (j + 1) * per_block)],
                         axis=-1).astype(BF16)
    bu_re_ref[:, j * kst:(j + 1) * kst] = jnp.dot(uj, bmat_re_ref[j], preferred_element_type=F32)
    bu_im_ref[:, j * kst:(j + 1) * kst] = jnp.dot(uj, bmat_im_ref[j], preferred_element_type=F32)


def _ssm_scan_block(j, ab_re_ref, ab_im_ref, st_re_ref, st_im_ref, bu_re_ref, bu_im_ref, *, tb,
                    x_re_ref=None, x_im_ref=None):
    kst = bu_re_ref.shape[1] // SSM_BLOCKS
    two = 2 * V7X_SUBLANES
    chunks = [slice(j * kst + c * SSM_COL_CHUNK, j * kst + (c + 1) * SSM_COL_CHUNK)
              for c in range(kst // SSM_COL_CHUNK)]
    coef = [(jnp.broadcast_to(ab_re_ref[:, cs], (V7X_SUBLANES, SSM_COL_CHUNK)),
             jnp.broadcast_to(ab_im_ref[:, cs], (V7X_SUBLANES, SSM_COL_CHUNK))) for cs in chunks]
    state = [(st_re_ref[:, cs], st_im_ref[:, cs]) for cs in chunks]
    for t2 in range(tb // 2):
        rows = slice(t2 * two, (t2 + 1) * two)
        for n, cs in enumerate(chunks):
            ar, ai = coef[n]
            xr, xi = state[n]
            bur = bu_re_ref[rows, cs]
            bui = bu_im_ref[rows, cs]
            xr1 = ar * xr - ai * xi + bur[:V7X_SUBLANES]
            xi1 = ar * xi + ai * xr + bui[:V7X_SUBLANES]
            xr2 = ar * xr1 - ai * xi1 + bur[V7X_SUBLANES:]
            xi2 = ar * xi1 + ai * xr1 + bui[V7X_SUBLANES:]
            if x_re_ref is not None:
                x_re_ref[rows, cs] = jnp.concatenate([xr1, xr2], axis=0).astype(BF16)
                x_im_ref[rows, cs] = jnp.concatenate([xi1, xi2], axis=0).astype(BF16)
            state[n] = (xr2, xi2)
    for (xr, xi), cs in zip(state, chunks):
        st_re_ref[:, cs] = xr
        st_im_ref[:, cs] = xi


def _ssm_ends_kernel(u_ref, bmat_re_ref, bmat_im_ref, ab_re_ref, ab_im_ref, end_re_ref, end_im_ref,
                     lhs_ref, bu_re_ref, bu_im_ref, st_re_ref, st_im_ref, *, tb):
    tblk = pl.program_id(1)

    @pl.when(tblk == 0)
    def _():
        st_re_ref[...] = jnp.zeros_like(st_re_ref)
        st_im_ref[...] = jnp.zeros_like(st_im_ref)

    _ssm_fill_lhs(u_ref, lhs_ref, tb=tb)
    drive = functools.partial(_ssm_drive_block, lhs_ref=lhs_ref, bmat_re_ref=bmat_re_ref,
                              bmat_im_ref=bmat_im_ref, bu_re_ref=bu_re_ref, bu_im_ref=bu_im_ref)
    drive(0)
    for j in range(SSM_BLOCKS):
        if j + 1 < SSM_BLOCKS:
            drive(j + 1)
        _ssm_scan_block(j, ab_re_ref, ab_im_ref, st_re_ref, st_im_ref, bu_re_ref, bu_im_ref, tb=tb)

    @pl.when(tblk == pl.num_programs(1) - 1)
    def _():
        end_re_ref[0] = st_re_ref[...]
        end_im_ref[0] = st_im_ref[...]


def _ssm_main_kernel(u_ref, end_re_ref, end_im_ref, bmat_re_ref, bmat_im_ref, ab_re_ref, ab_im_ref,
                     cmat_re_ref, cmat_im_ref, d_ref, wglu_ref, bglu_ref, o_ref,
                     lhs_ref, bu_re_ref, bu_im_ref, st_re_ref, st_im_ref, x_re_ref, x_im_ref, *, tb, seg_len):
    tblk = pl.program_id(1)
    n_seg = st_re_ref.shape[0]

    @pl.when(tblk == 0)
    def _():
        pr = ab_re_ref[...]
        pi = ab_im_ref[...]
        for _ in range(int(math.log2(seg_len))):
            pr, pi = pr * pr - pi * pi, 2.0 * pr * pi
        er = jnp.zeros_like(pr)
        ei = jnp.zeros_like(pi)
        st_re_ref[0:1, :] = er
        st_im_ref[0:1, :] = ei
        for j in range(1, n_seg):
            lr = end_re_ref[0, j - 1:j, :]
            li = end_im_ref[0, j - 1:j, :]
            er, ei = lr + pr * er - pi * ei, li + pr * ei + pi * er
            st_re_ref[j:j + 1, :] = er
            st_im_ref[j:j + 1, :] = ei

    _ssm_fill_lhs(u_ref, lhs_ref, tb=tb)
    drive = functools.partial(_ssm_drive_block, lhs_ref=lhs_ref, bmat_re_ref=bmat_re_ref,
                              bmat_im_ref=bmat_im_ref, bu_re_ref=bu_re_ref, bu_im_ref=bu_im_ref)
    n_slabs = lhs_ref.shape[0]
    kst = bu_re_ref.shape[1] // SSM_BLOCKS
    nt = (((1,), (1,)), ((), ()))
    ys = []
    drive(0)
    for j in range(SSM_BLOCKS):
        if j + 1 < SSM_BLOCKS:
            drive(j + 1)
        _ssm_scan_block(j, ab_re_ref, ab_im_ref, st_re_ref, st_im_ref, bu_re_ref, bu_im_ref, tb=tb,
                        x_re_ref=x_re_ref, x_im_ref=x_im_ref)
        xr = x_re_ref[:, j * kst:(j + 1) * kst]
        xi = x_im_ref[:, j * kst:(j + 1) * kst]
        ys.append(lax.dot_general(xr, cmat_re_ref[j], nt, preferred_element_type=F32)
                  - lax.dot_general(xi, cmat_im_ref[j], nt, preferred_element_type=F32))
    u_rows = jnp.concatenate([lhs_ref[c] for c in range(n_slabs)], axis=-1)
    y = jnp.concatenate(ys, axis=-1) + d_ref[...] * u_rows
    z = jax.nn.gelu(y)
    gate = jnp.dot(z.astype(BF16), wglu_ref[...], preferred_element_type=F32) + bglu_ref[...]
    s = z * jax.nn.sigmoid(gate)
    for c in range(n_slabs):
        lhs_ref[c] = s[:, c * V7X_LANES:(c + 1) * V7X_LANES]
    for seg in range(n_seg):
        for c in range(n_slabs):
            o_ref[0, seg, :, c * V7X_LANES:(c + 1) * V7X_LANES] = lhs_ref[c, pl.ds(seg, tb, stride=n_seg), :]


def _ssm(u_seg, bmat_re, bmat_im, ab_re, ab_im, cmat_re, cmat_im, d_skip, w_glu, b_glu, *, tb):
    batch, n_seg, seg_len, d_ssm = u_seg.shape
    n_cols = ab_re.shape[1]
    rows = tb * n_seg
    grid = (batch, seg_len // tb)
    u_spec = pl.BlockSpec((1, n_seg, tb, d_ssm), lambda b, t: (b, 0, t, 0))
    end_spec = pl.BlockSpec((1, n_seg, n_cols), lambda b, t: (b, 0, 0))

    def whole(a):
        return pl.BlockSpec(a.shape, lambda b, t: (0,) * a.ndim, pipeline_mode=pl.Buffered(1))

    scratch = [pltpu.VMEM((d_ssm // V7X_LANES, rows, V7X_LANES), F32), pltpu.VMEM((rows, n_cols), F32),
               pltpu.VMEM((rows, n_cols), F32), pltpu.VMEM((n_seg, n_cols), F32),
               pltpu.VMEM((n_seg, n_cols), F32)]
    end_re, end_im = pl.pallas_call(
        functools.partial(_ssm_ends_kernel, tb=tb),
        grid=grid,
        in_specs=[u_spec, whole(bmat_re), whole(bmat_im), whole(ab_re), whole(ab_im)],
        out_specs=[end_spec, end_spec],
        out_shape=[jax.ShapeDtypeStruct((batch, n_seg, n_cols), F32)] * 2,
        scratch_shapes=scratch,
        compiler_params=_params(("parallel", "arbitrary")),
        name="ssm_ends",
    )(u_seg, bmat_re, bmat_im, ab_re, ab_im)
    return pl.pallas_call(
        functools.partial(_ssm_main_kernel, tb=tb, seg_len=seg_len),
        grid=grid,
        in_specs=[u_spec, end_spec, end_spec, whole(bmat_re), whole(bmat_im), whole(ab_re), whole(ab_im),
                  whole(cmat_re), whole(cmat_im), whole(d_skip), whole(w_glu), whole(b_glu)],
        out_specs=u_spec,
        out_shape=jax.ShapeDtypeStruct(u_seg.shape, F32),
        scratch_shapes=scratch + [pltpu.VMEM((rows, n_cols), BF16), pltpu.VMEM((rows, n_cols), BF16)],
        compiler_params=_params(("parallel", "arbitrary")),
        name="ssm_main",
    )(u_seg, end_re, end_im, bmat_re, bmat_im, ab_re, ab_im, cmat_re, cmat_im, d_skip, w_glu, b_glu)


def _out_proj_kernel(a_ref, s_ref, wa_ref, ws_ref, xh_ref, xt_ref, g_ref, o_ref, *, n_head):
    m = (jnp.dot(a_ref[...], wa_ref[...], preferred_element_type=F32)
         + jnp.dot(s_ref[...].astype(BF16), ws_ref[...], preferred_element_type=F32))
    o_ref[...] = _row_tile(xh_ref, xt_ref, n_head) + m * _rms_scale(m) * g_ref[...]


def _out_proj(a, s, w, x_parts, g, *, tm):
    t = a.shape[0]
    d = w.shape[1]
    da = a.shape[1]
    ds = s.shape[1]
    assert da == ds and w.shape[0] == da + ds
    x_specs, n_head = _row_tile_specs(x_parts, tm)
    return pl.pallas_call(
        functools.partial(_out_proj_kernel, n_head=n_head),
        grid=(t // tm,),
        in_specs=[
            pl.BlockSpec((tm, da), lambda i: (i, 0)),
            pl.BlockSpec((tm, ds), lambda i: (i, 0)),
            pl.BlockSpec((da, d), lambda i: (0, 0)),
            pl.BlockSpec((ds, d), lambda i: (1, 0)),
        ] + x_specs + [
            pl.BlockSpec((1, d), lambda i: (0, 0)),
        ],
        out_specs=pl.BlockSpec((tm, d), lambda i: (i, 0)),
        out_shape=jax.ShapeDtypeStruct((t, d), F32),
        compiler_params=_params(("parallel",)),
        name="out_proj",
    )(a, s, w, w, *x_parts, g)


def kernel(x, ffn1_pre_g, ffn1_w_gate, ffn1_w_up, ffn1_w_down, ffn1_post_g, mix_pre_g, w_in, lambda_q1, lambda_k1, lambda_q2, lambda_k2, subln_g, rel_bias, ssm_a_re, ssm_a_im, ssm_b_re, ssm_b_im, ssm_c_re, ssm_c_im, ssm_d, ssm_log_dt, w_glu, b_glu, w_out, mix_post_g, ffn2_pre_g, ffn2_w_gate, ffn2_w_up, ffn2_w_down, ffn2_post_g):
    batch, seq, d_model = x.shape
    depth = ffn1_pre_g.shape[0]
    assert depth == 1, "LAMBDA_INIT is specialised to a single layer"
    n_heads = rel_bias.shape[1]
    d_attn = n_heads * V_HEAD_DIM
    n_groups, n_state = ssm_a_re.shape[1:]
    d_ssm = n_groups * SSM_GROUP
    n_seg = V7X_SUBLANES
    seg_len = seq // n_seg
    tokens = batch * seq
    tm, tf, tq, tb = 512, 512, 256, 64
    tm_first = 2 * tm

    xt = x.reshape(tokens, d_model)
    row = lambda v: v.reshape(1, -1)
    l = 0

    x1_head, wg1, wu1, wd1, w_in_bf, w_glu_bf = _ffn_first(
        xt, row(ffn1_pre_g[l]), ffn1_w_gate[l], ffn1_w_up[l], ffn1_w_down[l], row(ffn1_post_g[l]),
        tm=tm_first, tf=tf // 2, cast=(w_in[l], w_glu[l]))
    x1_tail = _ffn(xt, row(ffn1_pre_g[l]), wg1, wu1, wd1, row(ffn1_post_g[l]), tm=tm, tf=tf,
                   rows_done=tm_first)
    x1 = (x1_head, x1_tail)

    qkv, u = _in_proj(x1, row(mix_pre_g[l]), w_in_bf, tm=tm, q_scale=ATTN_HEAD_DIM ** -0.5 * LOG2_E)

    band = _bias_band(rel_bias, tq=tq, scale=LOG2_E)
    a, wg2, wu2, wd2, wo = _attention(
        qkv, band, row(lambda_q1[l]), row(lambda_k1[l]), row(lambda_q2[l]), row(lambda_k2[l]),
        subln_g[l].reshape(-1, 1), batch=batch, seq=seq, n_heads=n_heads, tq=tq,
        cast=(ffn2_w_gate[l], ffn2_w_up[l], ffn2_w_down[l], w_out[l]))

    ab_re, ab_im, bmat_re, bmat_im, cmat_re, cmat_im = _ssm_params(
        ssm_a_re[l].reshape(n_groups, 1, n_state), ssm_a_im[l].reshape(n_groups, 1, n_state),
        ssm_log_dt[l].reshape(n_groups, 1, 1),
        jnp.swapaxes(ssm_b_re[l], 1, 2), jnp.swapaxes(ssm_b_im[l], 1, 2), ssm_c_re[l], ssm_c_im[l])
    u_seg = u.reshape(batch, n_seg, seg_len, d_ssm)
    s_seg = _ssm(u_seg, bmat_re, bmat_im, ab_re.reshape(1, -1), ab_im.reshape(1, -1), cmat_re, cmat_im,
                 row(ssm_d[l]), w_glu_bf, row(b_glu[l]), tb=tb)
    s = s_seg.reshape(tokens, d_ssm)

    x2 = _out_proj(a.reshape(tokens, d_attn), s, wo, x1, row(mix_post_g[l]), tm=tm)

    x3 = _ffn(x2, row(ffn2_pre_g[l]), wg2, wu2, wd2, row(ffn2_post_g[l]), tm=tm, tf=tf)
    return x3.reshape(batch, seq, d_model)
```

```python
import functools
import math

import jax
import jax.numpy as jnp
from jax import lax
from jax.experimental import pallas as pl
from jax.experimental.pallas import tpu as pltpu

V7X_LANES = 128
V7X_SUBLANES = 8
V7X_MXU_COLS = 256
V7X_VMEM_LIMIT_BYTES = 56 * 1024 * 1024

RMS_EPS = 1e-6
NEG_INF = -1e30
N_BUCKETS = 32
MAX_DISTANCE = 128
ATTN_HEAD_DIM = 64
V_HEAD_DIM = 128
SSM_GROUP = 16
SSM_STATE = 64
LAMBDA_INIT = 0.8 - 0.6 * math.exp(-0.3 * 0)

F32 = jnp.float32
BF16 = jnp.bfloat16
EPILOGUE_ROWS = 128


def _params(semantics):
    return pltpu.CompilerParams(dimension_semantics=semantics,
                                vmem_limit_bytes=V7X_VMEM_LIMIT_BYTES)


def _rms_scale(v):
    return lax.rsqrt(jnp.mean(v * v, axis=-1, keepdims=True) + RMS_EPS)


def _ffn_prologue(x_ref, pre_g_ref, xn_ref, acc_ref):
    xf = x_ref[...]
    xn_ref[...] = (xf * _rms_scale(xf) * pre_g_ref[...]).astype(BF16)
    acc_ref[...] = jnp.zeros_like(acc_ref)


def _ffn_chunk(xn_ref, acc_ref, wg_ref, wu_ref, wd_ref):
    xn = xn_ref[...]
    n_split = 2 if wg_ref.shape[1] >= 2 * V7X_MXU_COLS else 1
    half = wg_ref.shape[1] // n_split
    pre = []
    for c in range(n_split):
        cols = slice(c * half, (c + 1) * half)
        pre.append((jnp.dot(xn, wg_ref[:, cols], preferred_element_type=F32),
                    jnp.dot(xn, wu_ref[:, cols], preferred_element_type=F32)))
    out = acc_ref[...]
    for c, (gate, up) in enumerate(pre):
        hidden = (gate * jax.nn.sigmoid(gate) * up).astype(BF16)
        out = out + jnp.dot(hidden, wd_ref[c * half:(c + 1) * half, :], preferred_element_type=F32)
    acc_ref[...] = out


def _ffn_epilogue(x_ref, acc_ref, post_g_ref, o_ref):
    scale = _rms_scale(acc_ref[...])
    g_half = 0.5 * post_g_ref[...]
    for r in range(0, acc_ref.shape[0], EPILOGUE_ROWS):
        rows = slice(r, r + EPILOGUE_ROWS)
        o_ref[rows, :] = x_ref[rows, :] + acc_ref[rows, :] * scale[rows] * g_half


def _ffn_kernel(x_ref, pre_g_ref, wg_ref, wu_ref, wd_ref, post_g_ref, *rest, cast_steps):
    n_cast = len(cast_steps)
    cast_in = rest[:n_cast]
    o_ref = rest[n_cast]
    cast_out = rest[n_cast + 1:2 * n_cast + 1]
    xn_ref, acc_ref = rest[2 * n_cast + 1:]
    k = pl.program_id(1)
    step = pl.program_id(0) * pl.num_programs(1) + k
    for src, dst, steps in zip(cast_in, cast_out, cast_steps):
        @pl.when(step < steps)
        def _(src=src, dst=dst):
            dst[...] = src[...].astype(BF16)

    @pl.when(k == 0)
    def _():
        _ffn_prologue(x_ref, pre_g_ref, xn_ref, acc_ref)

    _ffn_chunk(xn_ref, acc_ref, wg_ref, wu_ref, wd_ref)

    @pl.when(k == pl.num_programs(1) - 1)
    def _():
        _ffn_epilogue(x_ref, acc_ref, post_g_ref, o_ref)


def _ffn_first_kernel(x_ref, pre_g_ref, wg32_ref, wu32_ref, wd32_ref, post_g_ref, *rest, cast_steps):
    n_cast = len(cast_steps)
    cast_in = rest[:n_cast]
    o_ref, wg_ref, wu_ref, wd_ref = rest[n_cast:n_cast + 4]
    cast_out = rest[n_cast + 4:2 * n_cast + 4]
    xn_ref, acc_ref = rest[2 * n_cast + 4:]
    k = pl.program_id(0)

    @pl.when(k == 0)
    def _():
        _ffn_prologue(x_ref, pre_g_ref, xn_ref, acc_ref)

    for src, dst, steps in zip(cast_in, cast_out, cast_steps):
        @pl.when(k < steps)
        def _(src=src, dst=dst):
            dst[...] = src[...].astype(BF16)

    wg = wg32_ref[...].astype(BF16)
    wu = wu32_ref[...].astype(BF16)
    wd = wd32_ref[...].astype(BF16)
    wg_ref[...] = wg
    wu_ref[...] = wu
    wd_ref[...] = wd
    _ffn_chunk(xn_ref, acc_ref, wg, wu, wd)

    @pl.when(k == pl.num_programs(0) - 1)
    def _():
        _ffn_epilogue(x_ref, acc_ref, post_g_ref, o_ref)


def _ffn_first(x, pre_g, wg32, wu32, wd32, post_g, *, tm, tf, cast=()):
    t, d = x.shape
    f = wg32.shape[1]
    n_chunks = f // tf
    cast_steps = []
    for w in cast:
        steps = 1
        while steps * 2 <= n_chunks and w.shape[0] % (16 * steps * 2) == 0:
            steps *= 2
        cast_steps.append(steps)
    cast_specs = [pl.BlockSpec((w.shape[0] // steps, w.shape[1]),
                               lambda k, steps=steps: (jnp.minimum(k, steps - 1), 0))
                  for w, steps in zip(cast, cast_steps)]
    w_specs = [pl.BlockSpec((d, tf), lambda k: (0, k)), pl.BlockSpec((d, tf), lambda k: (0, k)),
               pl.BlockSpec((tf, d), lambda k: (k, 0))]
    return pl.pallas_call(
        functools.partial(_ffn_first_kernel, cast_steps=tuple(cast_steps)),
        grid=(n_chunks,),
        in_specs=[pl.BlockSpec((tm, d), lambda k: (0, 0), pipeline_mode=pl.Buffered(1)),
                  pl.BlockSpec((1, d), lambda k: (0, 0))]
                 + w_specs + [pl.BlockSpec((1, d), lambda k: (0, 0))] + cast_specs,
        out_specs=[pl.BlockSpec((tm, d), lambda k: (0, 0), pipeline_mode=pl.Buffered(1))] + w_specs + cast_specs,
        out_shape=[jax.ShapeDtypeStruct((tm, d), F32)]
                  + [jax.ShapeDtypeStruct(w.shape, BF16) for w in (wg32, wu32, wd32) + tuple(cast)],
        scratch_shapes=[pltpu.VMEM((tm, d), BF16), pltpu.VMEM((tm, d), F32)],
        compiler_params=_params(("arbitrary",)),
        name="ffn_first",
    )(x, pre_g, wg32, wu32, wd32, post_g, *cast)


def _ffn(x, pre_g, wg, wu, wd, post_g, *, tm, tf, rows_done=0, cast=()):
    t, d = x.shape
    f = wg.shape[1]
    assert rows_done % tm == 0
    skip = rows_done // tm
    n_tiles, n_chunks = t // tm - skip, f // tf
    cast_steps, cast_specs = [], []
    for w in cast:
        rows = 16
        while w.shape[0] % rows or w.shape[0] // rows > n_tiles * n_chunks:
            rows += 16
        steps = w.shape[0] // rows
        cast_steps.append(steps)
        cast_specs.append(pl.BlockSpec(
            (rows, w.shape[1]), lambda i, k, steps=steps: (jnp.minimum(i * n_chunks + k, steps - 1), 0)))
    outs = pl.pallas_call(
        functools.partial(_ffn_kernel, cast_steps=tuple(cast_steps)),
        grid=(n_tiles, n_chunks),
        in_specs=[
            pl.BlockSpec((tm, d), lambda i, k: (i + skip, 0)),
            pl.BlockSpec((1, d), lambda i, k: (0, 0)),
            pl.BlockSpec((d, tf), lambda i, k: (0, k)),
            pl.BlockSpec((d, tf), lambda i, k: (0, k)),
            pl.BlockSpec((tf, d), lambda i, k: (k, 0)),
            pl.BlockSpec((1, d), lambda i, k: (0, 0)),
        ] + cast_specs,
        out_specs=[pl.BlockSpec((tm, d), lambda i, k: (i, 0))] + cast_specs,
        out_shape=[jax.ShapeDtypeStruct((t - rows_done, d), F32)]
                  + [jax.ShapeDtypeStruct(w.shape, BF16) for w in cast],
        scratch_shapes=[pltpu.VMEM((tm, d), BF16), pltpu.VMEM((tm, d), F32)],
        compiler_params=_params(("arbitrary" if cast else "parallel", "arbitrary")),
        name="ffn",
    )(x, pre_g, wg, wu, wd, post_g, *cast)
    return outs


def _row_tile_specs(parts, tm):
    head, tail = parts
    n_head = head.shape[0] // tm
    assert head.shape[0] % tm == 0 and tail.shape[0] % tm == 0 and head.shape[1] == tail.shape[1]
    d = head.shape[1]
    specs = [pl.BlockSpec((tm, d), lambda i: (jnp.minimum(i, n_head - 1), 0)),
             pl.BlockSpec((tm, d), lambda i: (jnp.maximum(i - n_head, 0), 0))]
    return specs, n_head


def _row_tile(head_ref, tail_ref, n_head):
    return jnp.where(pl.program_id(0) < n_head, head_ref[...], tail_ref[...])


def _in_proj_kernel(xh_ref, xt_ref, g_ref, w_ref, qkv_ref, u_ref, *, q_scale, n_head):
    xf = _row_tile(xh_ref, xt_ref, n_head)
    hn = (xf * _rms_scale(xf) * g_ref[...]).astype(BF16)
    tn = u_ref.shape[1]
    for n in range(4):
        acc = jnp.dot(hn, w_ref[:, n * tn:(n + 1) * tn], preferred_element_type=F32)
        if n == 0:
            qkv_ref[:, :tn] = (acc * q_scale).astype(BF16)
        elif n < 3:
            qkv_ref[:, n * tn:(n + 1) * tn] = acc.astype(BF16)
        else:
            u_ref[...] = acc


def _in_proj(x_parts, g, w, *, tm, q_scale):
    t = x_parts[0].shape[0] + x_parts[1].shape[0]
    d = x_parts[0].shape[1]
    tn = w.shape[1] // 4
    x_specs, n_head = _row_tile_specs(x_parts, tm)
    return pl.pallas_call(
        functools.partial(_in_proj_kernel, q_scale=q_scale, n_head=n_head),
        grid=(t // tm,),
        in_specs=x_specs + [
            pl.BlockSpec((1, d), lambda i: (0, 0)),
            pl.BlockSpec(w.shape, lambda i: (0, 0), pipeline_mode=pl.Buffered(1)),
        ],
        out_specs=[
            pl.BlockSpec((tm, 3 * tn), lambda i: (i, 0)),
            pl.BlockSpec((tm, tn), lambda i: (i, 0)),
        ],
        out_shape=[jax.ShapeDtypeStruct((t, 3 * tn), BF16),
                   jax.ShapeDtypeStruct((t, tn), F32)],
        compiler_params=_params(("parallel",)),
        name="in_proj",
    )(*x_parts, g, w)


def _bias_band_kernel(rb_ref, o_ref, *, tq, scale):
    h = pl.program_id(0)
    key = lax.broadcasted_iota(jnp.int32, (3 * tq, tq), 0)
    qry = lax.broadcasted_iota(jnp.int32, (3 * tq, tq), 1)
    dist = 2 * tq + qry - key
    n = jnp.maximum(dist, 0)
    max_exact = N_BUCKETS // 2
    nf = jnp.maximum(n, 1).astype(F32)
    large = max_exact + (jnp.log(nf / max_exact) / math.log(MAX_DISTANCE / max_exact)
                         * (N_BUCKETS - max_exact)).astype(jnp.int32)
    large = jnp.minimum(large, N_BUCKETS - 1)
    bucket = jnp.where(n < max_exact, n, large)
    val = jnp.zeros((3 * tq, tq), F32)
    for b in range(N_BUCKETS):
        val = jnp.where(bucket == b, rb_ref[b, h], val)
    o_ref[0] = jnp.where(dist >= 0, val * scale, NEG_INF)


def _bias_band(rel_bias, *, tq, scale):
    assert tq >= MAX_DISTANCE
    n_heads = rel_bias.shape[1]
    return pl.pallas_call(
        functools.partial(_bias_band_kernel, tq=tq, scale=scale),
        grid=(n_heads,),
        in_specs=[pl.BlockSpec(memory_space=pltpu.SMEM)],
        out_specs=pl.BlockSpec((1, 3 * tq, tq), lambda h: (h, 0, 0)),
        out_shape=jax.ShapeDtypeStruct((n_heads, 3 * tq, tq), F32),
        compiler_params=_params(("arbitrary",)),
        name="bias_band",
    )(rel_bias)


N_STREAMS = 4
VT_ROWS = V_HEAD_DIM + 16
LOG2_E = math.log2(math.e)


def _attn_kernel(lq1_ref, lk1_ref, lq2_ref, lk2_ref, q1_ref, q2_ref, k1_ref, k2_ref,
                 v_ref, band_ref, g_ref, *rest, tq, cast_rows):
    n_cast = len(cast_rows)
    cast_in = rest[:n_cast]
    o_ref = rest[n_cast]
    cast_out = rest[n_cast + 1:2 * n_cast + 1]
    vt_ref, qm_ref, s_ref, m_ref, acc_ref = rest[2 * n_cast + 1:]
    n_blocks = v_ref.shape[1] // tq
    n_pairs_total = n_blocks * (n_blocks + 1) // 2
    assert n_pairs_total % 2 == 0
    q_refs = (q1_ref, q2_ref)
    k_refs = (k1_ref, k2_ref)
    streams = [(hh, mi) for hh in range(2) for mi in range(2)]

    ones_pad = (lax.broadcasted_iota(jnp.int32, (VT_ROWS - V_HEAD_DIM, tq), 0) == 0).astype(BF16)
    feat = lax.broadcasted_iota(jnp.int32, (2 * ATTN_HEAD_DIM, tq), 0)
    for c in range(n_blocks):
        rows = slice(c * tq, (c + 1) * tq)
        vt = v_ref[0, rows, :].astype(F32).T.astype(BF16)
        for hh in range(2):
            vt_ref[c, hh, :V_HEAD_DIM, :] = vt[hh * V_HEAD_DIM:(hh + 1) * V_HEAD_DIM]
            vt_ref[c, hh, V_HEAD_DIM:, :] = ones_pad
        for mi in range(2):
            qt = q_refs[mi][0, rows, :].astype(F32).T
            for hh in range(2):
                qm_ref[c, mi, :, hh * tq:(hh + 1) * tq] = jnp.where(
                    (feat >= ATTN_HEAD_DIM) == (hh == 1), qt, 0.0).astype(BF16)

    def reset_state():
        m_ref[...] = jnp.full(m_ref.shape, NEG_INF, F32)
        acc_ref[...] = jnp.zeros(acc_ref.shape, F32)

    reset_state()
    lam = (jnp.exp(jnp.sum(lq1_ref[...] * lk1_ref[...], axis=-1, keepdims=True))
           - jnp.exp(jnp.sum(lq2_ref[...] * lk2_ref[...], axis=-1, keepdims=True)) + LAMBDA_INIT)

    def score_phase(i, j, buf):
        rows = pl.ds(pl.multiple_of(j * tq, tq), tq)
        start = jnp.where(j == i, 2 * tq, jnp.where(j == i - 1, tq, 0))
        band_rows = pl.ds(pl.multiple_of(start, tq), tq)
        band2 = jnp.concatenate([band_ref[0, band_rows, :], band_ref[1, band_rows, :]], axis=1)
        for mi in range(2):
            s_ref[buf, mi] = jnp.dot(k_refs[mi][0, rows, :], qm_ref[i, mi],
                                     preferred_element_type=F32) + band2

    def value_phase(j, buf):
        alphas, ps = [], []
        for mi in range(2):
            s = s_ref[buf, mi]
            m_old = jnp.where(j == 0, NEG_INF, m_ref[mi])
            m_new = jnp.maximum(m_old, jnp.max(s, axis=0, keepdims=True))
            alphas.append(jnp.exp2(m_old - m_new))
            ps.append(jnp.exp2(s - m_new).astype(BF16))
            m_ref[mi] = m_new
        for hh in range(2):
            head = slice(hh * tq, (hh + 1) * tq)
            p2 = jnp.concatenate([ps[0][:, head], ps[1][:, head]], axis=1)
            alpha2 = jnp.concatenate([alphas[0][:, head], alphas[1][:, head]], axis=1)
            acc_ref[buf, hh] = alpha2 * acc_ref[1 - buf, hh] + jnp.dot(
                vt_ref[j, hh], p2, preferred_element_type=F32)

    def finish_block(i, buf):
        out_rows = pl.ds(pl.multiple_of(i * tq, tq), tq)
        den = V_HEAD_DIM
        for hh in range(2):
            a = (acc_ref[buf, hh, :den, :tq] / acc_ref[buf, hh, den:den + 1, :tq]
                 - lam * (acc_ref[buf, hh, :den, tq:] / acc_ref[buf, hh, den:den + 1, tq:]))
            scale = lax.rsqrt(jnp.mean(a * a, axis=0, keepdims=True) + RMS_EPS)
            a = a * scale * g_ref[...] * (1.0 - LAMBDA_INIT)
            o_ref[0, out_rows, hh * V_HEAD_DIM:(hh + 1) * V_HEAD_DIM] = a.T.astype(o_ref.dtype)

    def half_step(i, j, parity):
        last_chunk = j == i
        ni = jnp.where(last_chunk, i + 1, i)
        nj = jnp.where(last_chunk, 0, j + 1)
        past_end = ni >= n_blocks
        score_phase(jnp.where(past_end, i, ni), jnp.where(past_end, j, nj), 1 - parity)
        value_phase(j, parity)
        return ni, nj

    def sweep_body(it, carry):
        for src, dst, rows_per_it in zip(cast_in, cast_out, cast_rows):
            blk = jnp.minimum(it, src.shape[0] // rows_per_it - 1)
            rows = pl.ds(pl.multiple_of(blk * rows_per_it, rows_per_it), rows_per_it)
            dst[rows, :] = src[rows, :].astype(BF16)
        i0, j0 = carry
        i1, j1 = half_step(i0, j0, 0)
        i2, j2 = half_step(i1, j1, 1)
        for parity, (i, j) in enumerate(((i0, j0), (i1, j1))):
            @pl.when(j == i)
            def _(parity=parity, i=i):
                finish_block(i, parity)
        return i2, j2

    score_phase(0, 0, 0)
    lax.fori_loop(0, n_pairs_total // 2, sweep_body, (jnp.int32(0), jnp.int32(0)))


def _attention(qkv, band, lq1, lk1, lq2, lk2, subln_g, *, batch, seq, n_heads, tq, cast=()):
    d_attn = n_heads * V_HEAD_DIM
    pair = 2 * ATTN_HEAD_DIM
    vpair = 2 * V_HEAD_DIM
    n_pairs = d_attn // 2 // pair
    n_blocks = seq // tq
    n_steps = batch * n_pairs
    n_iters = n_blocks * (n_blocks + 1) // 4
    cast_rows = []
    for w in cast:
        block_rows = w.shape[0] // n_steps
        rows_per_it = 16
        while block_rows % rows_per_it or block_rows // rows_per_it > n_iters:
            rows_per_it += 16
        assert w.shape[0] % n_steps == 0 and rows_per_it <= block_rows
        cast_rows.append(rows_per_it)
    cast_specs = [pl.BlockSpec((w.shape[0] // n_steps, w.shape[1]), lambda b, p: (b * n_pairs + p, 0))
                  for w in cast]
    qkv3 = qkv.reshape(batch, seq, 3 * d_attn)
    small = pl.BlockSpec((1, ATTN_HEAD_DIM), lambda b, p: (0, 0))
    return pl.pallas_call(
        functools.partial(_attn_kernel, tq=tq, cast_rows=tuple(cast_rows)),
        grid=(batch, n_pairs),
        in_specs=[
            small, small, small, small,
            pl.BlockSpec((1, seq, pair), lambda b, p: (b, 0, p)),
            pl.BlockSpec((1, seq, pair), lambda b, p: (b, 0, n_pairs + p)),
            pl.BlockSpec((1, seq, pair), lambda b, p: (b, 0, 2 * n_pairs + p)),
            pl.BlockSpec((1, seq, pair), lambda b, p: (b, 0, 3 * n_pairs + p)),
            pl.BlockSpec((1, seq, vpair), lambda b, p: (b, 0, 2 * n_pairs + p)),
            pl.BlockSpec((2, 3 * tq, tq), lambda b, p: (p, 0, 0)),
            pl.BlockSpec((V_HEAD_DIM, 1), lambda b, p: (0, 0)),
        ] + cast_specs,
        out_specs=[pl.BlockSpec((1, seq, vpair), lambda b, p: (b, 0, p))] + cast_specs,
        out_shape=[jax.ShapeDtypeStruct((batch, seq, d_attn), BF16)]
                  + [jax.ShapeDtypeStruct(w.shape, BF16) for w in cast],
        scratch_shapes=[pltpu.VMEM((n_blocks, 2, VT_ROWS, tq), BF16),
                        pltpu.VMEM((n_blocks, 2, pair, 2 * tq), BF16),
                        pltpu.VMEM((2, 2, tq, 2 * tq), F32),
                        pltpu.VMEM((2, 1, 2 * tq), F32),
                        pltpu.VMEM((2, 2, VT_ROWS, 2 * tq), F32)],
        compiler_params=_params(("parallel", "parallel")),
        name="diff_attention",
    )(lq1, lk1, lq2, lk2, qkv3, qkv3, qkv3, qkv3, qkv3, band, subln_g, *cast)


def _block_diag_blocks(w, dst_ref):
    n_blocks, rows, cols = dst_ref.shape
    g, p, n = w.shape
    gl = g // n_blocks
    row_group = lax.broadcasted_iota(jnp.int32, (rows, cols), 0) // p
    col_group = lax.broadcasted_iota(jnp.int32, (rows, cols), 1) // n
    for j in range(n_blocks):
        stacked = w[j * gl:(j + 1) * gl].reshape(rows, n)
        tiled = jnp.concatenate([stacked] * gl, axis=1)
        dst_ref[j] = jnp.where(row_group == col_group, tiled, 0.0).astype(BF16)


def _ssm_params_kernel(a_re_ref, a_im_ref, log_dt_ref, b_re_ref, b_im_ref, c_re_ref, c_im_ref,
                       ab_re_ref, ab_im_ref, bd_re_ref, bd_im_ref, cd_re_ref, cd_im_ref):
    ar = a_re_ref[...]
    ai = a_im_ref[...]
    dt = jnp.exp(log_dt_ref[...])
    decay = jnp.exp(dt * ar)
    ab_re = decay * jnp.cos(dt * ai)
    ab_im = decay * jnp.sin(dt * ai)
    den = ar * ar + ai * ai
    nr = ab_re - 1.0
    ni = ab_im
    coef_re = (nr * ar + ni * ai) / den
    coef_im = (ni * ar - nr * ai) / den
    br = b_re_ref[...]
    bi = b_im_ref[...]
    ab_re_ref[...] = ab_re
    ab_im_ref[...] = ab_im
    _block_diag_blocks(coef_re * br - coef_im * bi, bd_re_ref)
    _block_diag_blocks(coef_re * bi + coef_im * br, bd_im_ref)
    _block_diag_blocks(c_re_ref[...], cd_re_ref)
    _block_diag_blocks(c_im_ref[...], cd_im_ref)


def _ssm_params(a_re, a_im, log_dt, b_re_t, b_im_t, c_re, c_im):
    g, _, n = a_re.shape
    p = b_re_t.shape[1]
    gl = g // SSM_BLOCKS
    dense = jax.ShapeDtypeStruct((SSM_BLOCKS, gl * p, gl * n), BF16)
    return pl.pallas_call(
        _ssm_params_kernel,
        out_shape=[jax.ShapeDtypeStruct((g, 1, n), F32), jax.ShapeDtypeStruct((g, 1, n), F32),
                   dense, dense, dense, dense],
        compiler_params=pltpu.CompilerParams(vmem_limit_bytes=V7X_VMEM_LIMIT_BYTES),
        name="ssm_params",
    )(a_re, a_im, log_dt, b_re_t, b_im_t, c_re, c_im)


SSM_COL_CHUNK = 512
SSM_BLOCKS = 4


def _ssm_fill_lhs(u_ref, lhs_ref, *, tb):
    n_seg = u_ref.shape[1]
    for seg in range(n_seg):
        for c in range(lhs_ref.shape[0]):
            lhs_ref[c, pl.ds(seg, tb, stride=n_seg), :] = u_ref[0, seg, :, c * V7X_LANES:(c + 1) * V7X_LANES]


def _ssm_drive_block(j, lhs_ref, bmat_re_ref, bmat_im_ref, bu_re_ref, bu_im_ref):
    per_block = lhs_ref.shape[0] // SSM_BLOCKS
    kst = bu_re_ref.shape[1] // SSM_BLOCKS
    uj = jnp.concatenate([lhs_ref[c] for c in range(j * per_block, (j + 1) * per_block)],
                         axis=-1).astype(BF16)
    bu_re_ref[:, j * kst:(j + 1) * kst] = jnp.dot(uj, bmat_re_ref[j], preferred_element_type=F32)
    bu_im_ref[:, j * kst:(j + 1) * kst] = jnp.dot(uj, bmat_im_ref[j], preferred_element_type=F32)


def _ssm_scan_block(j, ab_re_ref, ab_im_ref, st_re_ref, st_im_ref, bu_re_ref, bu_im_ref, *, tb,
                    x_re_ref=None, x_im_ref=None):
    kst = bu_re_ref.shape[1] // SSM_BLOCKS
    two = 2 * V7X_SUBLANES
    chunks = [slice(j * kst + c * SSM_COL_CHUNK, j * kst + (c + 1) * SSM_COL_CHUNK)
              for c in range(kst // SSM_COL_CHUNK)]
    coef = [(jnp.broadcast_to(ab_re_ref[:, cs], (V7X_SUBLANES, SSM_COL_CHUNK)),
             jnp.broadcast_to(ab_im_ref[:, cs], (V7X_SUBLANES, SSM_COL_CHUNK))) for cs in chunks]
    state = [(st_re_ref[:, cs], st_im_ref[:, cs]) for cs in chunks]
    for t2 in range(tb // 2):
        rows = slice(t2 * two, (t2 + 1) * two)
        for n, cs in enumerate(chunks):
            ar, ai = coef[n]
            xr, xi = state[n]
            bur = bu_re_ref[rows, cs]
            bui = bu_im_ref[rows, cs]
            xr1 = ar * xr - ai * xi + bur[:V7X_SUBLANES]
            xi1 = ar * xi + ai * xr + bui[:V7X_SUBLANES]
            xr2 = ar * xr1 - ai * xi1 + bur[V7X_SUBLANES:]
            xi2 = ar * xi1 + ai * xr1 + bui[V7X_SUBLANES:]
            if x_re_ref is not None:
                x_re_ref[rows, cs] = jnp.concatenate([xr1, xr2], axis=0).astype(BF16)
                x_im_ref[rows, cs] = jnp.concatenate([xi1, xi2], axis=0).astype(BF16)
            state[n] = (xr2, xi2)
    for (xr, xi), cs in zip(state, chunks):
        st_re_ref[:, cs] = xr
        st_im_ref[:, cs] = xi


def _ssm_ends_kernel(u_ref, bmat_re_ref, bmat_im_ref, ab_re_ref, ab_im_ref, end_re_ref, end_im_ref,
                     lhs_ref, bu_re_ref, bu_im_ref, st_re_ref, st_im_ref, *, tb):
    tblk = pl.program_id(1)

    @pl.when(tblk == 0)
    def _():
        st_re_ref[...] = jnp.zeros_like(st_re_ref)
        st_im_ref[...] = jnp.zeros_like(st_im_ref)

    _ssm_fill_lhs(u_ref, lhs_ref, tb=tb)
    drive = functools.partial(_ssm_drive_block, lhs_ref=lhs_ref, bmat_re_ref=bmat_re_ref,
                              bmat_im_ref=bmat_im_ref, bu_re_ref=bu_re_ref, bu_im_ref=bu_im_ref)
    drive(0)
    for j in range(SSM_BLOCKS):
        if j + 1 < SSM_BLOCKS:
            drive(j + 1)
        _ssm_scan_block(j, ab_re_ref, ab_im_ref, st_re_ref, st_im_ref, bu_re_ref, bu_im_ref, tb=tb)

    @pl.when(tblk == pl.num_programs(1) - 1)
    def _():
        end_re_ref[0] = st_re_ref[...]
        end_im_ref[0] = st_im_ref[...]


def _ssm_main_kernel(u_ref, end_re_ref, end_im_ref, bmat_re_ref, bmat_im_ref, ab_re_ref, ab_im_ref,
                     cmat_re_ref, cmat_im_ref, d_ref, wglu_ref, bglu_ref, o_ref,
                     lhs_ref, bu_re_ref, bu_im_ref, st_re_ref, st_im_ref, x_re_ref, x_im_ref, *, tb, seg_len):
    tblk = pl.program_id(1)
    n_seg = st_re_ref.shape[0]

    @pl.when(tblk == 0)
    def _():
        pr = ab_re_ref[...]
        pi = ab_im_ref[...]
        for _ in range(int(math.log2(seg_len))):
            pr, pi = pr * pr - pi * pi, 2.0 * pr * pi
        er = jnp.zeros_like(pr)
        ei = jnp.zeros_like(pi)
        st_re_ref[0:1, :] = er
        st_im_ref[0:1, :] = ei
        for j in range(1, n_seg):
            lr = end_re_ref[0, j - 1:j, :]
            li = end_im_ref[0, j - 1:j, :]
            er, ei = lr + pr * er - pi * ei, li + pr * ei + pi * er
            st_re_ref[j:j + 1, :] = er
            st_im_ref[j:j + 1, :] = ei

    _ssm_fill_lhs(u_ref, lhs_ref, tb=tb)
    drive = functools.partial(_ssm_drive_block, lhs_ref=lhs_ref, bmat_re_ref=bmat_re_ref,
                              bmat_im_ref=bmat_im_ref, bu_re_ref=bu_re_ref, bu_im_ref=bu_im_ref)
    n_slabs = lhs_ref.shape[0]
    kst = bu_re_ref.shape[1] // SSM_BLOCKS
    nt = (((1,), (1,)), ((), ()))
    ys = []
    drive(0)
    for j in range(SSM_BLOCKS):
        if j + 1 < SSM_BLOCKS:
            drive(j + 1)
        _ssm_scan_block(j, ab_re_ref, ab_im_ref, st_re_ref, st_im_ref, bu_re_ref, bu_im_ref, tb=tb,
                        x_re_ref=x_re_ref, x_im_ref=x_im_ref)
        xr = x_re_ref[:, j * kst:(j + 1) * kst]
        xi = x_im_ref[:, j * kst:(j + 1) * kst]
        ys.append(lax.dot_general(xr, cmat_re_ref[j], nt, preferred_element_type=F32)
                  - lax.dot_general(xi, cmat_im_ref[j], nt, preferred_element_type=F32))
    u_rows = jnp.concatenate([lhs_ref[c] for c in range(n_slabs)], axis=-1)
    y = jnp.concatenate(ys, axis=-1) + d_ref[...] * u_rows
    z = jax.nn.gelu(y)
    gate = jnp.dot(z.astype(BF16), wglu_ref[...], preferred_element_type=F32) + bglu_ref[...]
    s = z * jax.nn.sigmoid(gate)
    for c in range(n_slabs):
        lhs_ref[c] = s[:, c * V7X_LANES:(c + 1) * V7X_LANES]
    for seg in range(n_seg):
        for c in range(n_slabs):
            o_ref[0, seg, :, c * V7X_LANES:(c + 1) * V7X_LANES] = lhs_ref[c, pl.ds(seg, tb, stride=n_seg), :]


def _ssm(u_seg, bmat_re, bmat_im, ab_re, ab_im, cmat_re, cmat_im, d_skip, w_glu, b_glu, *, tb):
    batch, n_seg, seg_len, d_ssm = u_seg.shape
    n_cols = ab_re.shape[1]
    rows = tb * n_seg
    grid = (batch, seg_len // tb)
    u_spec = pl.BlockSpec((1, n_seg, tb, d_ssm), lambda b, t: (b, 0, t, 0))
    end_spec = pl.BlockSpec((1, n_seg, n_cols), lambda b, t: (b, 0, 0))

    def whole(a):
        return pl.BlockSpec(a.shape, lambda b, t: (0,) * a.ndim, pipeline_mode=pl.Buffered(1))

    scratch = [pltpu.VMEM((d_ssm // V7X_LANES, rows, V7X_LANES), F32), pltpu.VMEM((rows, n_cols), F32),
               pltpu.VMEM((rows, n_cols), F32), pltpu.VMEM((n_seg, n_cols), F32),
               pltpu.VMEM((n_seg, n_cols), F32)]
    end_re, end_im = pl.pallas_call(
        functools.partial(_ssm_ends_kernel, tb=tb),
        grid=grid,
        in_specs=[u_spec, whole(bmat_re), whole(bmat_im), whole(ab_re), whole(ab_im)],
        out_specs=[end_spec, end_spec],
        out_shape=[jax.ShapeDtypeStruct((batch, n_seg, n_cols), F32)] * 2,
        scratch_shapes=scratch,
        compiler_params=_params(("parallel", "arbitrary")),
        name="ssm_ends",
    )(u_seg, bmat_re, bmat_im, ab_re, ab_im)
    return pl.pallas_call(
        functools.partial(_ssm_main_kernel, tb=tb, seg_len=seg_len),
        grid=grid,
        in_specs=[u_spec, end_spec, end_spec, whole(bmat_re), whole(bmat_im), whole(ab_re), whole(ab_im),
                  whole(cmat_re), whole(cmat_im), whole(d_skip), whole(w_glu), whole(b_glu)],
        out_specs=u_spec,
        out_shape=jax.ShapeDtypeStruct(u_seg.shape, F32),
        scratch_shapes=scratch + [pltpu.VMEM((rows, n_cols), BF16), pltpu.VMEM((rows, n_cols), BF16)],
        compiler_params=_params(("parallel", "arbitrary")),
        name="ssm_main",
    )(u_seg, end_re, end_im, bmat_re, bmat_im, ab_re, ab_im, cmat_re, cmat_im, d_skip, w_glu, b_glu)


def _out_proj_kernel(a_ref, s_ref, wa_ref, ws_ref, xh_ref, xt_ref, g_ref, o_ref, *, n_head):
    m = (jnp.dot(a_ref[...], wa_ref[...], preferred_element_type=F32)
         + jnp.dot(s_ref[...].astype(BF16), ws_ref[...], preferred_element_type=F32))
    o_ref[...] = _row_tile(xh_ref, xt_ref, n_head) + m * _rms_scale(m) * g_ref[...]


def _out_proj(a, s, w, x_parts, g, *, tm):
    t = a.shape[0]
    d = w.shape[1]
    da = a.shape[1]
    ds = s.shape[1]
    assert da == ds and w.shape[0] == da + ds
    x_specs, n_head = _row_tile_specs(x_parts, tm)
    return pl.pallas_call(
        functools.partial(_out_proj_kernel, n_head=n_head),
        grid=(t // tm,),
        in_specs=[
            pl.BlockSpec((tm, da), lambda i: (i, 0)),
            pl.BlockSpec((tm, ds), lambda i: (i, 0)),
            pl.BlockSpec((da, d), lambda i: (0, 0)),
            pl.BlockSpec((ds, d), lambda i: (1, 0)),
        ] + x_specs + [
            pl.BlockSpec((1, d), lambda i: (0, 0)),
        ],
        out_specs=pl.BlockSpec((tm, d), lambda i: (i, 0)),
        out_shape=jax.ShapeDtypeStruct((t, d), F32),
        compiler_params=_params(("parallel",)),
        name="out_proj",
    )(a, s, w, w, *x_parts, g)


def kernel(x, ffn1_pre_g, ffn1_w_gate, ffn1_w_up, ffn1_w_down, ffn1_post_g, mix_pre_g, w_in, lambda_q1, lambda_k1, lambda_q2, lambda_k2, subln_g, rel_bias, ssm_a_re, ssm_a_im, ssm_b_re, ssm_b_im, ssm_c_re, ssm_c_im, ssm_d, ssm_log_dt, w_glu, b_glu, w_out, mix_post_g, ffn2_pre_g, ffn2_w_gate, ffn2_w_up, ffn2_w_down, ffn2_post_g):
    batch, seq, d_model = x.shape
    depth = ffn1_pre_g.shape[0]
    assert depth == 1, "LAMBDA_INIT is specialised to a single layer"
    n_heads = rel_bias.shape[1]
    d_attn = n_heads * V_HEAD_DIM
    n_groups, n_state = ssm_a_re.shape[1:]
    d_ssm = n_groups * SSM_GROUP
    n_seg = V7X_SUBLANES
    seg_len = seq // n_seg
    tokens = batch * seq
    tm, tf, tq, tb = 512, 512, 256, 64
    tm_first = 2 * tm

    xt = x.reshape(tokens, d_model)
    row = lambda v: v.reshape(1, -1)
    l = 0

    x1_head, wg1, wu1, wd1 = _ffn_first(
        xt, row(ffn1_pre_g[l]), ffn1_w_gate[l], ffn1_w_up[l], ffn1_w_down[l], row(ffn1_post_g[l]),
        tm=tm_first, tf=tf // 2)
    x1_tail, w_in_bf, w_glu_bf = _ffn(xt, row(ffn1_pre_g[l]), wg1, wu1, wd1, row(ffn1_post_g[l]), tm=tm, tf=tf,
                                      rows_done=tm_first, cast=(w_in[l], w_glu[l]))
    x1 = (x1_head, x1_tail)

    qkv, u = _in_proj(x1, row(mix_pre_g[l]), w_in_bf, tm=tm, q_scale=ATTN_HEAD_DIM ** -0.5 * LOG2_E)

    band = _bias_band(rel_bias, tq=tq, scale=LOG2_E)
    a, wg2, wu2, wd2, wo = _attention(
        qkv, band, row(lambda_q1[l]), row(lambda_k1[l]), row(lambda_q2[l]), row(lambda_k2[l]),
        subln_g[l].reshape(-1, 1), batch=batch, seq=seq, n_heads=n_heads, tq=tq,
        cast=(ffn2_w_gate[l], ffn2_w_up[l], ffn2_w_down[l], w_out[l]))

    ab_re, ab_im, bmat_re, bmat_im, cmat_re, cmat_im = _ssm_params(
        ssm_a_re[l].reshape(n_groups, 1, n_state), ssm_a_im[l].reshape(n_groups, 1, n_state),
        ssm_log_dt[l].reshape(n_groups, 1, 1),
        jnp.swapaxes(ssm_b_re[l], 1, 2), jnp.swapaxes(ssm_b_im[l], 1, 2), ssm_c_re[l], ssm_c_im[l])
    u_seg = u.reshape(batch, n_seg, seg_len, d_ssm)
    s_seg = _ssm(u_seg, bmat_re, bmat_im, ab_re.reshape(1, -1), ab_im.reshape(1, -1), cmat_re, cmat_im,
                 row(ssm_d[l]), w_glu_bf, row(b_glu[l]), tb=tb)
    s = s_seg.reshape(tokens, d_ssm)

    x2 = _out_proj(a.reshape(tokens, d_attn), s, wo, x1, row(mix_post_g[l]), tm=tm)

    (x3,) = _ffn(x2, row(ffn2_pre_g[l]), wg2, wu2, wd2, row(ffn2_post_g[l]), tm=tm, tf=tf)
    return x3.reshape(batch, seq, d_model)
```

```python
import functools
import math

import jax
import jax.numpy as jnp
from jax import lax
from jax.experimental import pallas as pl
from jax.experimental.pallas import tpu as pltpu

V7X_LANES = 128
V7X_SUBLANES = 8
V7X_MXU_COLS = 256
V7X_VMEM_LIMIT_BYTES = 56 * 1024 * 1024

RMS_EPS = 1e-6
NEG_INF = -1e30
N_BUCKETS = 32
MAX_DISTANCE = 128
ATTN_HEAD_DIM = 64
V_HEAD_DIM = 128
SSM_GROUP = 16
LAMBDA_INIT = 0.8 - 0.6 * math.exp(-0.3 * 0)

F32 = jnp.float32
BF16 = jnp.bfloat16
EPILOGUE_ROWS = 128


def _params(semantics):
    return pltpu.CompilerParams(dimension_semantics=semantics,
                                vmem_limit_bytes=V7X_VMEM_LIMIT_BYTES)


def _rms_scale(v):
    return lax.rsqrt(jnp.mean(v * v, axis=-1, keepdims=True) + RMS_EPS)


def _ffn_prologue(x_ref, pre_g_ref, xn_ref, acc_ref):
    xf = x_ref[...]
    xn_ref[...] = (xf * _rms_scale(xf) * pre_g_ref[...]).astype(BF16)
    acc_ref[...] = jnp.zeros_like(acc_ref)


def _ffn_chunk(xn_ref, acc_ref, wg_ref, wu_ref, wd_ref):
    xn = xn_ref[...]
    n_split = 2 if wg_ref.shape[1] >= 2 * V7X_MXU_COLS else 1
    half = wg_ref.shape[1] // n_split
    pre = []
    for c in range(n_split):
        cols = slice(c * half, (c + 1) * half)
        pre.append((jnp.dot(xn, wg_ref[:, cols], preferred_element_type=F32),
                    jnp.dot(xn, wu_ref[:, cols], preferred_element_type=F32)))
    out = acc_ref[...]
    for c, (gate, up) in enumerate(pre):
        hidden = (gate * jax.nn.sigmoid(gate) * up).astype(BF16)
        out = out + jnp.dot(hidden, wd_ref[c * half:(c + 1) * half, :], preferred_element_type=F32)
    acc_ref[...] = out


def _ffn_epilogue(x_ref, acc_ref, post_g_ref, o_ref):
    scale = _rms_scale(acc_ref[...])
    g_half = 0.5 * post_g_ref[...]
    for r in range(0, acc_ref.shape[0], EPILOGUE_ROWS):
        rows = slice(r, r + EPILOGUE_ROWS)
        o_ref[rows, :] = x_ref[rows, :] + acc_ref[rows, :] * scale[rows] * g_half


def _ffn_kernel(x_ref, pre_g_ref, wg_ref, wu_ref, wd_ref, post_g_ref, o_ref, xn_ref, acc_ref):
    k = pl.program_id(1)

    @pl.when(k == 0)
    def _():
        _ffn_prologue(x_ref, pre_g_ref, xn_ref, acc_ref)

    _ffn_chunk(xn_ref, acc_ref, wg_ref, wu_ref, wd_ref)

    @pl.when(k == pl.num_programs(1) - 1)
    def _():
        _ffn_epilogue(x_ref, acc_ref, post_g_ref, o_ref)


def _ffn_first_kernel(x_ref, pre_g_ref, wg32_ref, wu32_ref, wd32_ref, post_g_ref, *rest, cast_steps):
    n_cast = len(cast_steps)
    cast_in = rest[:n_cast]
    o_ref, wg_ref, wu_ref, wd_ref = rest[n_cast:n_cast + 4]
    cast_out = rest[n_cast + 4:2 * n_cast + 4]
    xn_ref, acc_ref = rest[2 * n_cast + 4:]
    k = pl.program_id(0)

    @pl.when(k == 0)
    def _():
        _ffn_prologue(x_ref, pre_g_ref, xn_ref, acc_ref)

    for src, dst, steps in zip(cast_in, cast_out, cast_steps):
        @pl.when(k < steps)
        def _(src=src, dst=dst):
            dst[...] = src[...].astype(BF16)

    wg = wg32_ref[...].astype(BF16)
    wu = wu32_ref[...].astype(BF16)
    wd = wd32_ref[...].astype(BF16)
    wg_ref[...] = wg
    wu_ref[...] = wu
    wd_ref[...] = wd
    _ffn_chunk(xn_ref, acc_ref, wg, wu, wd)

    @pl.when(k == pl.num_programs(0) - 1)
    def _():
        _ffn_epilogue(x_ref, acc_ref, post_g_ref, o_ref)


def _ffn_first(x, pre_g, wg32, wu32, wd32, post_g, *, tm, tf, cast=()):
    t, d = x.shape
    f = wg32.shape[1]
    n_chunks = f // tf
    cast_steps = []
    for w in cast:
        steps = 1
        while steps * 2 <= n_chunks and w.shape[0] % (16 * steps * 2) == 0:
            steps *= 2
        cast_steps.append(steps)
    cast_specs = [pl.BlockSpec((w.shape[0] // steps, w.shape[1]),
                               lambda k, steps=steps: (jnp.minimum(k, steps - 1), 0))
                  for w, steps in zip(cast, cast_steps)]
    w_specs = [pl.BlockSpec((d, tf), lambda k: (0, k)), pl.BlockSpec((d, tf), lambda k: (0, k)),
               pl.BlockSpec((tf, d), lambda k: (k, 0))]
    return pl.pallas_call(
        functools.partial(_ffn_first_kernel, cast_steps=tuple(cast_steps)),
        grid=(n_chunks,),
        in_specs=[pl.BlockSpec((tm, d), lambda k: (0, 0), pipeline_mode=pl.Buffered(1)),
                  pl.BlockSpec((1, d), lambda k: (0, 0))]
                 + w_specs + [pl.BlockSpec((1, d), lambda k: (0, 0))] + cast_specs,
        out_specs=[pl.BlockSpec((tm, d), lambda k: (0, 0), pipeline_mode=pl.Buffered(1))] + w_specs + cast_specs,
        out_shape=[jax.ShapeDtypeStruct((tm, d), F32)]
                  + [jax.ShapeDtypeStruct(w.shape, BF16) for w in (wg32, wu32, wd32) + tuple(cast)],
        scratch_shapes=[pltpu.VMEM((tm, d), BF16), pltpu.VMEM((tm, d), F32)],
        compiler_params=_params(("arbitrary",)),
        name="ffn_first",
    )(x, pre_g, wg32, wu32, wd32, post_g, *cast)


def _ffn(x, pre_g, wg, wu, wd, post_g, *, tm, tf, rows_done=0):
    t, d = x.shape
    f = wg.shape[1]
    assert rows_done % tm == 0
    skip = rows_done // tm
    return pl.pallas_call(
        _ffn_kernel,
        grid=(t // tm - skip, f // tf),
        in_specs=[
            pl.BlockSpec((tm, d), lambda i, k: (i + skip, 0)),
            pl.BlockSpec((1, d), lambda i, k: (0, 0)),
            pl.BlockSpec((d, tf), lambda i, k: (0, k)),
            pl.BlockSpec((d, tf), lambda i, k: (0, k)),
            pl.BlockSpec((tf, d), lambda i, k: (k, 0)),
            pl.BlockSpec((1, d), lambda i, k: (0, 0)),
        ],
        out_specs=pl.BlockSpec((tm, d), lambda i, k: (i, 0)),
        out_shape=jax.ShapeDtypeStruct((t - rows_done, d), F32),
        scratch_shapes=[pltpu.VMEM((tm, d), BF16), pltpu.VMEM((tm, d), F32)],
        compiler_params=_params(("parallel", "arbitrary")),
        name="ffn",
    )(x, pre_g, wg, wu, wd, post_g)


def _row_tile_specs(parts, tm):
    head, tail = parts
    n_head = head.shape[0] // tm
    assert head.shape[0] % tm == 0 and tail.shape[0] % tm == 0 and head.shape[1] == tail.shape[1]
    d = head.shape[1]
    specs = [pl.BlockSpec((tm, d), lambda i: (jnp.minimum(i, n_head - 1), 0)),
             pl.BlockSpec((tm, d), lambda i: (jnp.maximum(i - n_head, 0), 0))]
    return specs, n_head


def _row_tile(head_ref, tail_ref, n_head):
    return jnp.where(pl.program_id(0) < n_head, head_ref[...], tail_ref[...])


def _in_proj_kernel(xh_ref, xt_ref, g_ref, w_ref, qkv_ref, u_ref, *, q_scale, n_head):
    xf = _row_tile(xh_ref, xt_ref, n_head)
    hn = (xf * _rms_scale(xf) * g_ref[...]).astype(BF16)
    tn = u_ref.shape[1]
    for n in range(4):
        acc = jnp.dot(hn, w_ref[:, n * tn:(n + 1) * tn], preferred_element_type=F32)
        if n == 0:
            qkv_ref[:, :tn] = (acc * q_scale).astype(BF16)
        elif n < 3:
            qkv_ref[:, n * tn:(n + 1) * tn] = acc.astype(BF16)
        else:
            u_ref[...] = acc


def _in_proj(x_parts, g, w, *, tm, q_scale):
    t = x_parts[0].shape[0] + x_parts[1].shape[0]
    d = x_parts[0].shape[1]
    tn = w.shape[1] // 4
    x_specs, n_head = _row_tile_specs(x_parts, tm)
    return pl.pallas_call(
        functools.partial(_in_proj_kernel, q_scale=q_scale, n_head=n_head),
        grid=(t // tm,),
        in_specs=x_specs + [
            pl.BlockSpec((1, d), lambda i: (0, 0)),
            pl.BlockSpec(w.shape, lambda i: (0, 0), pipeline_mode=pl.Buffered(1)),
        ],
        out_specs=[
            pl.BlockSpec((tm, 3 * tn), lambda i: (i, 0)),
            pl.BlockSpec((tm, tn), lambda i: (i, 0)),
        ],
        out_shape=[jax.ShapeDtypeStruct((t, 3 * tn), BF16),
                   jax.ShapeDtypeStruct((t, tn), F32)],
        compiler_params=_params(("parallel",)),
        name="in_proj",
    )(*x_parts, g, w)


def _bias_band_kernel(rb_ref, o_ref, *, tq, scale):
    h = pl.program_id(0)
    key = lax.broadcasted_iota(jnp.int32, (3 * tq, tq), 0)
    qry = lax.broadcasted_iota(jnp.int32, (3 * tq, tq), 1)
    dist = 2 * tq + qry - key
    n = jnp.maximum(dist, 0)
    max_exact = N_BUCKETS // 2
    nf = jnp.maximum(n, 1).astype(F32)
    large = max_exact + (jnp.log(nf / max_exact) / math.log(MAX_DISTANCE / max_exact)
                         * (N_BUCKETS - max_exact)).astype(jnp.int32)
    large = jnp.minimum(large, N_BUCKETS - 1)
    bucket = jnp.where(n < max_exact, n, large)
    val = jnp.zeros((3 * tq, tq), F32)
    for b in range(N_BUCKETS):
        val = jnp.where(bucket == b, rb_ref[b, h], val)
    o_ref[0] = jnp.where(dist >= 0, val * scale, NEG_INF)


def _bias_band(rel_bias, *, tq, scale):
    assert tq >= MAX_DISTANCE
    n_heads = rel_bias.shape[1]
    return pl.pallas_call(
        functools.partial(_bias_band_kernel, tq=tq, scale=scale),
        grid=(n_heads,),
        in_specs=[pl.BlockSpec(memory_space=pltpu.SMEM)],
        out_specs=pl.BlockSpec((1, 3 * tq, tq), lambda h: (h, 0, 0)),
        out_shape=jax.ShapeDtypeStruct((n_heads, 3 * tq, tq), F32),
        compiler_params=_params(("arbitrary",)),
        name="bias_band",
    )(rel_bias)


VT_ROWS = V_HEAD_DIM + 16
LOG2_E = math.log2(math.e)


def _attn_kernel(lq1_ref, lk1_ref, lq2_ref, lk2_ref, q1_ref, q2_ref, k1_ref, k2_ref,
                 v_ref, band_ref, g_ref, *rest, tq, cast_rows):
    n_cast = len(cast_rows)
    cast_in = rest[:n_cast]
    o_ref = rest[n_cast]
    cast_out = rest[n_cast + 1:2 * n_cast + 1]
    vt_ref, qm_ref, s_ref, m_ref, acc_ref = rest[2 * n_cast + 1:]
    n_blocks = v_ref.shape[1] // tq
    q_refs = (q1_ref, q2_ref)
    k_refs = (k1_ref, k2_ref)

    ones_pad = (lax.broadcasted_iota(jnp.int32, (VT_ROWS - V_HEAD_DIM, tq), 0) == 0).astype(BF16)
    feat = lax.broadcasted_iota(jnp.int32, (2 * ATTN_HEAD_DIM, tq), 0)
    for c in range(n_blocks):
        rows = slice(c * tq, (c + 1) * tq)
        vt = v_ref[0, rows, :].astype(F32).T.astype(BF16)
        for hh in range(2):
            vt_ref[c, hh, :V_HEAD_DIM, :] = vt[hh * V_HEAD_DIM:(hh + 1) * V_HEAD_DIM]
            vt_ref[c, hh, V_HEAD_DIM:, :] = ones_pad
        for mi in range(2):
            qt = q_refs[mi][0, rows, :].astype(F32).T
            for hh in range(2):
                qm_ref[c, mi, :, hh * tq:(hh + 1) * tq] = jnp.where(
                    (feat >= ATTN_HEAD_DIM) == (hh == 1), qt, 0.0).astype(BF16)

    lam = (jnp.exp(jnp.sum(lq1_ref[...] * lk1_ref[...], axis=-1, keepdims=True))
           - jnp.exp(jnp.sum(lq2_ref[...] * lk2_ref[...], axis=-1, keepdims=True)) + LAMBDA_INIT)

    def score_phase(i, j, buf):
        rows = slice(j * tq, (j + 1) * tq)
        start = 2 * tq if j == i else tq if j == i - 1 else 0
        band_rows = slice(start, start + tq)
        band2 = jnp.concatenate([band_ref[0, band_rows, :], band_ref[1, band_rows, :]], axis=1)
        for mi in range(2):
            s_ref[buf, mi] = jnp.dot(k_refs[mi][0, rows, :], qm_ref[i, mi],
                                     preferred_element_type=F32) + band2

    def value_phase(j, buf):
        alphas, ps = [], []
        for mi in range(2):
            s = s_ref[buf, mi]
            m_new = jnp.max(s, axis=0, keepdims=True)
            if j > 0:
                m_old = m_ref[mi]
                m_new = jnp.maximum(m_old, m_new)
                alphas.append(jnp.exp2(m_old - m_new))
            ps.append(jnp.exp2(s - m_new).astype(BF16))
            m_ref[mi] = m_new
        for hh in range(2):
            head = slice(hh * tq, (hh + 1) * tq)
            p2 = jnp.concatenate([ps[0][:, head], ps[1][:, head]], axis=1)
            pv = jnp.dot(vt_ref[j, hh], p2, preferred_element_type=F32)
            if j > 0:
                alpha2 = jnp.concatenate([alphas[0][:, head], alphas[1][:, head]], axis=1)
                pv = alpha2 * acc_ref[1 - buf, hh] + pv
            acc_ref[buf, hh] = pv

    def finish_block(i, buf):
        out_rows = slice(i * tq, (i + 1) * tq)
        den = V_HEAD_DIM
        for hh in range(2):
            a = (acc_ref[buf, hh, :den, :tq] / acc_ref[buf, hh, den:den + 1, :tq]
                 - lam * (acc_ref[buf, hh, :den, tq:] / acc_ref[buf, hh, den:den + 1, tq:]))
            scale = lax.rsqrt(jnp.mean(a * a, axis=0, keepdims=True) + RMS_EPS)
            a = a * scale * g_ref[...] * (1.0 - LAMBDA_INIT)
            o_ref[0, out_rows, hh * V_HEAD_DIM:(hh + 1) * V_HEAD_DIM] = a.T.astype(o_ref.dtype)

    pairs = [(i, j) for i in range(n_blocks) for j in range(i + 1)]
    score_phase(0, 0, 0)
    for n, (i, j) in enumerate(pairs):
        for src, dst, rows_per_it in zip(cast_in, cast_out, cast_rows):
            if n < src.shape[0] // rows_per_it:
                rows = slice(n * rows_per_it, (n + 1) * rows_per_it)
                dst[rows, :] = src[rows, :].astype(BF16)
        if n + 1 < len(pairs):
            score_phase(*pairs[n + 1], (n + 1) % 2)
        value_phase(j, n % 2)
        if j == i:
            finish_block(i, n % 2)


def _attention(qkv, band, lq1, lk1, lq2, lk2, subln_g, *, batch, seq, n_heads, tq, cast=()):
    d_attn = n_heads * V_HEAD_DIM
    pair = 2 * ATTN_HEAD_DIM
    vpair = 2 * V_HEAD_DIM
    n_pairs = d_attn // 2 // pair
    n_blocks = seq // tq
    n_steps = batch * n_pairs
    n_iters = n_blocks * (n_blocks + 1) // 2
    cast_rows = []
    for w in cast:
        block_rows = w.shape[0] // n_steps
        rows_per_it = 16
        while block_rows % rows_per_it or block_rows // rows_per_it > n_iters:
            rows_per_it += 16
        assert w.shape[0] % n_steps == 0 and rows_per_it <= block_rows
        cast_rows.append(rows_per_it)
    cast_specs = [pl.BlockSpec((w.shape[0] // n_steps, w.shape[1]), lambda b, p: (b * n_pairs + p, 0))
                  for w in cast]
    qkv3 = qkv.reshape(batch, seq, 3 * d_attn)
    small = pl.BlockSpec((1, ATTN_HEAD_DIM), lambda b, p: (0, 0))
    return pl.pallas_call(
        functools.partial(_attn_kernel, tq=tq, cast_rows=tuple(cast_rows)),
        grid=(batch, n_pairs),
        in_specs=[
            small, small, small, small,
            pl.BlockSpec((1, seq, pair), lambda b, p: (b, 0, p)),
            pl.BlockSpec((1, seq, pair), lambda b, p: (b, 0, n_pairs + p)),
            pl.BlockSpec((1, seq, pair), lambda b, p: (b, 0, 2 * n_pairs + p)),
            pl.BlockSpec((1, seq, pair), lambda b, p: (b, 0, 3 * n_pairs + p)),
            pl.BlockSpec((1, seq, vpair), lambda b, p: (b, 0, 2 * n_pairs + p)),
            pl.BlockSpec((2, 3 * tq, tq), lambda b, p: (p, 0, 0)),
            pl.BlockSpec((V_HEAD_DIM, 1), lambda b, p: (0, 0)),
        ] + cast_specs,
        out_specs=[pl.BlockSpec((1, seq, vpair), lambda b, p: (b, 0, p))] + cast_specs,
        out_shape=[jax.ShapeDtypeStruct((batch, seq, d_attn), BF16)]
                  + [jax.ShapeDtypeStruct(w.shape, BF16) for w in cast],
        scratch_shapes=[pltpu.VMEM((n_blocks, 2, VT_ROWS, tq), BF16),
                        pltpu.VMEM((n_blocks, 2, pair, 2 * tq), BF16),
                        pltpu.VMEM((2, 2, tq, 2 * tq), F32),
                        pltpu.VMEM((2, 1, 2 * tq), F32),
                        pltpu.VMEM((2, 2, VT_ROWS, 2 * tq), F32)],
        compiler_params=_params(("parallel", "parallel")),
        name="diff_attention",
    )(lq1, lk1, lq2, lk2, qkv3, qkv3, qkv3, qkv3, qkv3, band, subln_g, *cast)


def _block_diag_blocks(w, dst_ref):
    n_blocks, rows, cols = dst_ref.shape
    g, p, n = w.shape
    gl = g // n_blocks
    row_group = lax.broadcasted_iota(jnp.int32, (rows, cols), 0) // p
    col_group = lax.broadcasted_iota(jnp.int32, (rows, cols), 1) // n
    for j in range(n_blocks):
        stacked = w[j * gl:(j + 1) * gl].reshape(rows, n)
        tiled = jnp.concatenate([stacked] * gl, axis=1)
        dst_ref[j] = jnp.where(row_group == col_group, tiled, 0.0).astype(BF16)


def _ssm_params_kernel(a_re_ref, a_im_ref, log_dt_ref, b_re_ref, b_im_ref, c_re_ref, c_im_ref,
                       ab_re_ref, ab_im_ref, bd_re_ref, bd_im_ref, cd_re_ref, cd_im_ref):
    ar = a_re_ref[...]
    ai = a_im_ref[...]
    dt = jnp.exp(log_dt_ref[...])
    decay = jnp.exp(dt * ar)
    ab_re = decay * jnp.cos(dt * ai)
    ab_im = decay * jnp.sin(dt * ai)
    den = ar * ar + ai * ai
    nr = ab_re - 1.0
    ni = ab_im
    coef_re = (nr * ar + ni * ai) / den
    coef_im = (ni * ar - nr * ai) / den
    br = b_re_ref[...]
    bi = b_im_ref[...]
    ab_re_ref[...] = ab_re
    ab_im_ref[...] = ab_im
    _block_diag_blocks(coef_re * br - coef_im * bi, bd_re_ref)
    _block_diag_blocks(coef_re * bi + coef_im * br, bd_im_ref)
    _block_diag_blocks(c_re_ref[...], cd_re_ref)
    _block_diag_blocks(c_im_ref[...], cd_im_ref)


def _ssm_params(a_re, a_im, log_dt, b_re_t, b_im_t, c_re, c_im):
    g, _, n = a_re.shape
    p = b_re_t.shape[1]
    gl = g // SSM_BLOCKS
    dense = jax.ShapeDtypeStruct((SSM_BLOCKS, gl * p, gl * n), BF16)
    return pl.pallas_call(
        _ssm_params_kernel,
        out_shape=[jax.ShapeDtypeStruct((g, 1, n), F32), jax.ShapeDtypeStruct((g, 1, n), F32),
                   dense, dense, dense, dense],
        compiler_params=pltpu.CompilerParams(vmem_limit_bytes=V7X_VMEM_LIMIT_BYTES),
        name="ssm_params",
    )(a_re, a_im, log_dt, b_re_t, b_im_t, c_re, c_im)


SSM_COL_CHUNK = 512
SSM_BLOCKS = 4


def _ssm_fill_lhs(u_ref, lhs_ref, *, tb):
    n_seg = u_ref.shape[1]
    for seg in range(n_seg):
        for c in range(lhs_ref.shape[0]):
            lhs_ref[c, pl.ds(seg, tb, stride=n_seg), :] = u_ref[0, seg, :, c * V7X_LANES:(c + 1) * V7X_LANES]


def _ssm_drive_block(j, lhs_ref, bmat_re_ref, bmat_im_ref, bu_re_ref, bu_im_ref):
    per_block = lhs_ref.shape[0] // SSM_BLOCKS
    kst = bu_re_ref.shape[1] // SSM_BLOCKS
    uj = jnp.concatenate([lhs_ref[c] for c in range(j * per_block, (j + 1) * per_block)],
                         axis=-1).astype(BF16)
    bu_re_ref[:, j * kst:(j + 1) * kst] = jnp.dot(uj, bmat_re_ref[j], preferred_element_type=F32)
    bu_im_ref[:, j * kst:(j + 1) * kst] = jnp.dot(uj, bmat_im_ref[j], preferred_element_type=F32)


def _ssm_scan_block(j, ab_re_ref, ab_im_ref, st_re_ref, st_im_ref, bu_re_ref, bu_im_ref, *, tb,
                    x_re_ref=None, x_im_ref=None):
    kst = bu_re_ref.shape[1] // SSM_BLOCKS
    two = 2 * V7X_SUBLANES
    chunks = [slice(j * kst + c * SSM_COL_CHUNK, j * kst + (c + 1) * SSM_COL_CHUNK)
              for c in range(kst // SSM_COL_CHUNK)]
    coef = [(jnp.broadcast_to(ab_re_ref[:, cs], (V7X_SUBLANES, SSM_COL_CHUNK)),
             jnp.broadcast_to(ab_im_ref[:, cs], (V7X_SUBLANES, SSM_COL_CHUNK))) for cs in chunks]
    state = [(st_re_ref[:, cs], st_im_ref[:, cs]) for cs in chunks]
    for t2 in range(tb // 2):
        rows = slice(t2 * two, (t2 + 1) * two)
        for n, cs in enumerate(chunks):
            ar, ai = coef[n]
            xr, xi = state[n]
            bur = bu_re_ref[rows, cs]
            bui = bu_im_ref[rows, cs]
            xr1 = ar * xr - ai * xi + bur[:V7X_SUBLANES]
            xi1 = ar * xi + ai * xr + bui[:V7X_SUBLANES]
            xr2 = ar * xr1 - ai * xi1 + bur[V7X_SUBLANES:]
            xi2 = ar * xi1 + ai * xr1 + bui[V7X_SUBLANES:]
            if x_re_ref is not None:
                x_re_ref[rows, cs] = jnp.concatenate([xr1, xr2], axis=0).astype(BF16)
                x_im_ref[rows, cs] = jnp.concatenate([xi1, xi2], axis=0).astype(BF16)
            state[n] = (xr2, xi2)
    for (xr, xi), cs in zip(state, chunks):
        st_re_ref[:, cs] = xr
        st_im_ref[:, cs] = xi


def _ssm_ends_kernel(u_ref, bmat_re_ref, bmat_im_ref, ab_re_ref, ab_im_ref, end_re_ref, end_im_ref,
                     lhs_ref, bu_re_ref, bu_im_ref, st_re_ref, st_im_ref, *, tb):
    tblk = pl.program_id(1)

    @pl.when(tblk == 0)
    def _():
        st_re_ref[...] = jnp.zeros_like(st_re_ref)
        st_im_ref[...] = jnp.zeros_like(st_im_ref)

    _ssm_fill_lhs(u_ref, lhs_ref, tb=tb)
    drive = functools.partial(_ssm_drive_block, lhs_ref=lhs_ref, bmat_re_ref=bmat_re_ref,
                              bmat_im_ref=bmat_im_ref, bu_re_ref=bu_re_ref, bu_im_ref=bu_im_ref)
    drive(0)
    for j in range(SSM_BLOCKS):
        if j + 1 < SSM_BLOCKS:
            drive(j + 1)
        _ssm_scan_block(j, ab_re_ref, ab_im_ref, st_re_ref, st_im_ref, bu_re_ref, bu_im_ref, tb=tb)

    @pl.when(tblk == pl.num_programs(1) - 1)
    def _():
        end_re_ref[0] = st_re_ref[...]
        end_im_ref[0] = st_im_ref[...]


def _ssm_main_kernel(u_ref, end_re_ref, end_im_ref, bmat_re_ref, bmat_im_ref, ab_re_ref, ab_im_ref,
                     cmat_re_ref, cmat_im_ref, d_ref, wglu_ref, bglu_ref, o_ref,
                     lhs_ref, bu_re_ref, bu_im_ref, st_re_ref, st_im_ref, x_re_ref, x_im_ref, *, tb, seg_len):
    tblk = pl.program_id(1)
    n_seg = st_re_ref.shape[0]

    @pl.when(tblk == 0)
    def _():
        pr = ab_re_ref[...]
        pi = ab_im_ref[...]
        for _ in range(int(math.log2(seg_len))):
            pr, pi = pr * pr - pi * pi, 2.0 * pr * pi
        er = jnp.zeros_like(pr)
        ei = jnp.zeros_like(pi)
        st_re_ref[0:1, :] = er
        st_im_ref[0:1, :] = ei
        for j in range(1, n_seg):
            lr = end_re_ref[0, j - 1:j, :]
            li = end_im_ref[0, j - 1:j, :]
            er, ei = lr + pr * er - pi * ei, li + pr * ei + pi * er
            st_re_ref[j:j + 1, :] = er
            st_im_ref[j:j + 1, :] = ei

    _ssm_fill_lhs(u_ref, lhs_ref, tb=tb)
    drive = functools.partial(_ssm_drive_block, lhs_ref=lhs_ref, bmat_re_ref=bmat_re_ref,
                              bmat_im_ref=bmat_im_ref, bu_re_ref=bu_re_ref, bu_im_ref=bu_im_ref)
    n_slabs = lhs_ref.shape[0]
    kst = bu_re_ref.shape[1] // SSM_BLOCKS
    nt = (((1,), (1,)), ((), ()))
    ys = []
    drive(0)
    for j in range(SSM_BLOCKS):
        if j + 1 < SSM_BLOCKS:
            drive(j + 1)
        _ssm_scan_block(j, ab_re_ref, ab_im_ref, st_re_ref, st_im_ref, bu_re_ref, bu_im_ref, tb=tb,
                        x_re_ref=x_re_ref, x_im_ref=x_im_ref)
        xr = x_re_ref[:, j * kst:(j + 1) * kst]
        xi = x_im_ref[:, j * kst:(j + 1) * kst]
        ys.append(lax.dot_general(xr, cmat_re_ref[j], nt, preferred_element_type=F32)
                  - lax.dot_general(xi, cmat_im_ref[j], nt, preferred_element_type=F32))
    u_rows = jnp.concatenate([lhs_ref[c] for c in range(n_slabs)], axis=-1)
    y = jnp.concatenate(ys, axis=-1) + d_ref[...] * u_rows
    z = jax.nn.gelu(y)
    gate = jnp.dot(z.astype(BF16), wglu_ref[...], preferred_element_type=F32) + bglu_ref[...]
    s = z * jax.nn.sigmoid(gate)
    for c in range(n_slabs):
        lhs_ref[c] = s[:, c * V7X_LANES:(c + 1) * V7X_LANES]
    for seg in range(n_seg):
        for c in range(n_slabs):
            o_ref[0, seg, :, c * V7X_LANES:(c + 1) * V7X_LANES] = lhs_ref[c, pl.ds(seg, tb, stride=n_seg), :]


def _ssm(u_seg, bmat_re, bmat_im, ab_re, ab_im, cmat_re, cmat_im, d_skip, w_glu, b_glu, *, tb):
    batch, n_seg, seg_len, d_ssm = u_seg.shape
    n_cols = ab_re.shape[1]
    rows = tb * n_seg
    grid = (batch, seg_len // tb)
    u_spec = pl.BlockSpec((1, n_seg, tb, d_ssm), lambda b, t: (b, 0, t, 0))
    end_spec = pl.BlockSpec((1, n_seg, n_cols), lambda b, t: (b, 0, 0))

    def whole(a):
        return pl.BlockSpec(a.shape, lambda b, t: (0,) * a.ndim, pipeline_mode=pl.Buffered(1))

    scratch = [pltpu.VMEM((d_ssm // V7X_LANES, rows, V7X_LANES), F32), pltpu.VMEM((rows, n_cols), F32),
               pltpu.VMEM((rows, n_cols), F32), pltpu.VMEM((n_seg, n_cols), F32),
               pltpu.VMEM((n_seg, n_cols), F32)]
    end_re, end_im = pl.pallas_call(
        functools.partial(_ssm_ends_kernel, tb=tb),
        grid=grid,
        in_specs=[u_spec, whole(bmat_re), whole(bmat_im), whole(ab_re), whole(ab_im)],
        out_specs=[end_spec, end_spec],
        out_shape=[jax.ShapeDtypeStruct((batch, n_seg, n_cols), F32)] * 2,
        scratch_shapes=scratch,
        compiler_params=_params(("parallel", "arbitrary")),
        name="ssm_ends",
    )(u_seg, bmat_re, bmat_im, ab_re, ab_im)
    return pl.pallas_call(
        functools.partial(_ssm_main_kernel, tb=tb, seg_len=seg_len),
        grid=grid,
        in_specs=[u_spec, end_spec, end_spec, whole(bmat_re), whole(bmat_im), whole(ab_re), whole(ab_im),
                  whole(cmat_re), whole(cmat_im), whole(d_skip), whole(w_glu), whole(b_glu)],
        out_specs=u_spec,
        out_shape=jax.ShapeDtypeStruct(u_seg.shape, F32),
        scratch_shapes=scratch + [pltpu.VMEM((rows, n_cols), BF16), pltpu.VMEM((rows, n_cols), BF16)],
        compiler_params=_params(("parallel", "arbitrary")),
        name="ssm_main",
    )(u_seg, end_re, end_im, bmat_re, bmat_im, ab_re, ab_im, cmat_re, cmat_im, d_skip, w_glu, b_glu)


def _out_proj_kernel(a_ref, s_ref, wa_ref, ws_ref, xh_ref, xt_ref, g_ref, o_ref, *, n_head):
    m = (jnp.dot(a_ref[...], wa_ref[...], preferred_element_type=F32)
         + jnp.dot(s_ref[...].astype(BF16), ws_ref[...], preferred_element_type=F32))
    o_ref[...] = _row_tile(xh_ref, xt_ref, n_head) + m * _rms_scale(m) * g_ref[...]


def _out_proj(a, s, w, x_parts, g, *, tm):
    t = a.shape[0]
    d = w.shape[1]
    da = a.shape[1]
    ds = s.shape[1]
    assert da == ds and w.shape[0] == da + ds
    x_specs, n_head = _row_tile_specs(x_parts, tm)
    return pl.pallas_call(
        functools.partial(_out_proj_kernel, n_head=n_head),
        grid=(t // tm,),
        in_specs=[
            pl.BlockSpec((tm, da), lambda i: (i, 0)),
            pl.BlockSpec((tm, ds), lambda i: (i, 0)),
            pl.BlockSpec((da, d), lambda i: (0, 0)),
            pl.BlockSpec((ds, d), lambda i: (1, 0)),
        ] + x_specs + [
            pl.BlockSpec((1, d), lambda i: (0, 0)),
        ],
        out_specs=pl.BlockSpec((tm, d), lambda i: (i, 0)),
        out_shape=jax.ShapeDtypeStruct((t, d), F32),
        compiler_params=_params(("parallel",)),
        name="out_proj",
    )(a, s, w, w, *x_parts, g)


def _tiles(d_model, d_ff, d_ssm, n_states):
    tm, tf, tq, tb = 512, 512, 256, 64
    tm_first = 2 * tm
    f32, bf16 = 4, 2
    ffn_bytes = tm * d_model * (2 * f32 + 2 * f32 + f32 + bf16) + 2 * 3 * d_model * tf * bf16
    first_bytes = (tm_first * d_model * (f32 + f32 + f32 + bf16)
                   + 2 * 3 * d_model * (tf // 2) * (f32 + bf16))
    ssm_bytes = (V7X_SUBLANES * tb) * (d_ssm * f32 * (1 + 2 + 2) + 2 * n_states * (f32 + bf16))
    assert max(ffn_bytes, first_bytes, ssm_bytes) < V7X_VMEM_LIMIT_BYTES and d_ff % tf == 0
    return tm, tf, tm_first, tq, tb


def kernel(x, ffn1_pre_g, ffn1_w_gate, ffn1_w_up, ffn1_w_down, ffn1_post_g, mix_pre_g, w_in, lambda_q1, lambda_k1, lambda_q2, lambda_k2, subln_g, rel_bias, ssm_a_re, ssm_a_im, ssm_b_re, ssm_b_im, ssm_c_re, ssm_c_im, ssm_d, ssm_log_dt, w_glu, b_glu, w_out, mix_post_g, ffn2_pre_g, ffn2_w_gate, ffn2_w_up, ffn2_w_down, ffn2_post_g):
    batch, seq, d_model = x.shape
    depth = ffn1_pre_g.shape[0]
    assert depth == 1, "LAMBDA_INIT is specialised to a single layer"
    n_heads = rel_bias.shape[1]
    d_attn = n_heads * V_HEAD_DIM
    n_groups, n_state = ssm_a_re.shape[1:]
    d_ssm = n_groups * SSM_GROUP
    n_seg = V7X_SUBLANES
    seg_len = seq // n_seg
    tokens = batch * seq
    tm, tf, tm_first, tq, tb = _tiles(d_model, ffn1_w_gate.shape[2], d_ssm, n_groups * n_state)

    xt = x.reshape(tokens, d_model)
    row = lambda v: v.reshape(1, -1)
    l = 0

    x1_head, wg1, wu1, wd1, w_in_bf, w_glu_bf = _ffn_first(
        xt, row(ffn1_pre_g[l]), ffn1_w_gate[l], ffn1_w_up[l], ffn1_w_down[l], row(ffn1_post_g[l]),
        tm=tm_first, tf=tf // 2, cast=(w_in[l], w_glu[l]))
    x1_tail = _ffn(xt, row(ffn1_pre_g[l]), wg1, wu1, wd1, row(ffn1_post_g[l]), tm=tm, tf=tf,
                   rows_done=tm_first)
    x1 = (x1_head, x1_tail)

    qkv, u = _in_proj(x1, row(mix_pre_g[l]), w_in_bf, tm=tm, q_scale=ATTN_HEAD_DIM ** -0.5 * LOG2_E)

    band = _bias_band(rel_bias, tq=tq, scale=LOG2_E)
    a, wg2, wu2, wd2, wo = _attention(
        qkv, band, row(lambda_q1[l]), row(lambda_k1[l]), row(lambda_q2[l]), row(lambda_k2[l]),
        subln_g[l].reshape(-1, 1), batch=batch, seq=seq, n_heads=n_heads, tq=tq,
        cast=(ffn2_w_gate[l], ffn2_w_up[l], ffn2_w_down[l], w_out[l]))

    ab_re, ab_im, bmat_re, bmat_im, cmat_re, cmat_im = _ssm_params(
        ssm_a_re[l].reshape(n_groups, 1, n_state), ssm_a_im[l].reshape(n_groups, 1, n_state),
        ssm_log_dt[l].reshape(n_groups, 1, 1),
        jnp.swapaxes(ssm_b_re[l], 1, 2), jnp.swapaxes(ssm_b_im[l], 1, 2), ssm_c_re[l], ssm_c_im[l])
    u_seg = u.reshape(batch, n_seg, seg_len, d_ssm)
    s_seg = _ssm(u_seg, bmat_re, bmat_im, ab_re.reshape(1, -1), ab_im.reshape(1, -1), cmat_re, cmat_im,
                 row(ssm_d[l]), w_glu_bf, row(b_glu[l]), tb=tb)
    s = s_seg.reshape(tokens, d_ssm)

    x2 = _out_proj(a.reshape(tokens, d_attn), s, wo, x1, row(mix_post_g[l]), tm=tm)

    x3 = _ffn(x2, row(ffn2_pre_g[l]), wg2, wu2, wd2, row(ffn2_post_g[l]), tm=tm, tf=tf)
    return x3.reshape(batch, seq, d_model)
```

```python
import functools
import math

import jax
import jax.numpy as jnp
from jax import lax
from jax.experimental import pallas as pl
from jax.experimental.pallas import tpu as pltpu

V7X_LANES = 128
V7X_SUBLANES = 8
V7X_MXU_COLS = 256
V7X_VMEM_LIMIT_BYTES = 56 * 1024 * 1024

RMS_EPS = 1e-6
NEG_INF = -1e30
N_BUCKETS = 32
MAX_DISTANCE = 128
ATTN_HEAD_DIM = 64
V_HEAD_DIM = 128
SSM_GROUP = 16
LAMBDA_INIT = 0.8 - 0.6 * math.exp(-0.3 * 0)

F32 = jnp.float32
BF16 = jnp.bfloat16
EPILOGUE_ROWS = 128


def _params(semantics):
    return pltpu.CompilerParams(dimension_semantics=semantics,
                                vmem_limit_bytes=V7X_VMEM_LIMIT_BYTES)


def _rms_scale(v):
    return lax.rsqrt(jnp.mean(v * v, axis=-1, keepdims=True) + RMS_EPS)


def _ffn_prologue(x_ref, pre_g_ref, xn_ref, acc_ref):
    xf = x_ref[...]
    xn_ref[...] = (xf * _rms_scale(xf) * pre_g_ref[...]).astype(BF16)
    acc_ref[...] = jnp.zeros_like(acc_ref)


def _ffn_chunk(xn_ref, acc_ref, wg_ref, wu_ref, wd_ref):
    xn = xn_ref[...]
    n_split = 2 if wg_ref.shape[1] >= 2 * V7X_MXU_COLS else 1
    half = wg_ref.shape[1] // n_split
    pre = []
    for c in range(n_split):
        cols = slice(c * half, (c + 1) * half)
        pre.append((jnp.dot(xn, wg_ref[:, cols], preferred_element_type=F32),
                    jnp.dot(xn, wu_ref[:, cols], preferred_element_type=F32)))
    out = acc_ref[...]
    for c, (gate, up) in enumerate(pre):
        hidden = (gate * jax.nn.sigmoid(gate) * up).astype(BF16)
        out = out + jnp.dot(hidden, wd_ref[c * half:(c + 1) * half, :], preferred_element_type=F32)
    acc_ref[...] = out


def _ffn_epilogue(x_ref, acc_ref, post_g_ref, o_ref):
    scale = _rms_scale(acc_ref[...])
    g_half = 0.5 * post_g_ref[...]
    for r in range(0, acc_ref.shape[0], EPILOGUE_ROWS):
        rows = slice(r, r + EPILOGUE_ROWS)
        o_ref[rows, :] = x_ref[rows, :] + acc_ref[rows, :] * scale[rows] * g_half


def _ffn_kernel(x_ref, pre_g_ref, wg_ref, wu_ref, wd_ref, post_g_ref, o_ref, xn_ref, acc_ref):
    k = pl.program_id(1)

    @pl.when(k == 0)
    def _():
        _ffn_prologue(x_ref, pre_g_ref, xn_ref, acc_ref)

    _ffn_chunk(xn_ref, acc_ref, wg_ref, wu_ref, wd_ref)

    @pl.when(k == pl.num_programs(1) - 1)
    def _():
        _ffn_epilogue(x_ref, acc_ref, post_g_ref, o_ref)


def _ffn_first_kernel(x_ref, pre_g_ref, wg32_ref, wu32_ref, wd32_ref, post_g_ref, *rest, cast_steps):
    n_cast = len(cast_steps)
    cast_in = rest[:n_cast]
    o_ref, wg_ref, wu_ref, wd_ref = rest[n_cast:n_cast + 4]
    cast_out = rest[n_cast + 4:2 * n_cast + 4]
    xn_ref, acc_ref = rest[2 * n_cast + 4:]
    k = pl.program_id(0)

    @pl.when(k == 0)
    def _():
        _ffn_prologue(x_ref, pre_g_ref, xn_ref, acc_ref)

    for src, dst, steps in zip(cast_in, cast_out, cast_steps):
        @pl.when(k < steps)
        def _(src=src, dst=dst):
            dst[...] = src[...].astype(BF16)

    wg = wg32_ref[...].astype(BF16)
    wu = wu32_ref[...].astype(BF16)
    wd = wd32_ref[...].astype(BF16)
    wg_ref[...] = wg
    wu_ref[...] = wu
    wd_ref[...] = wd
    _ffn_chunk(xn_ref, acc_ref, wg, wu, wd)

    @pl.when(k == pl.num_programs(0) - 1)
    def _():
        _ffn_epilogue(x_ref, acc_ref, post_g_ref, o_ref)


def _ffn_first(x, pre_g, wg32, wu32, wd32, post_g, *, tm, tf, cast=()):
    t, d = x.shape
    f = wg32.shape[1]
    n_chunks = f // tf
    cast_steps = []
    for w in cast:
        steps = 1
        while steps * 2 <= n_chunks and w.shape[0] % (16 * steps * 2) == 0:
            steps *= 2
        cast_steps.append(steps)
    cast_specs = [pl.BlockSpec((w.shape[0] // steps, w.shape[1]),
                               lambda k, steps=steps: (jnp.minimum(k, steps - 1), 0))
                  for w, steps in zip(cast, cast_steps)]
    w_specs = [pl.BlockSpec((d, tf), lambda k: (0, k)), pl.BlockSpec((d, tf), lambda k: (0, k)),
               pl.BlockSpec((tf, d), lambda k: (k, 0))]
    return pl.pallas_call(
        functools.partial(_ffn_first_kernel, cast_steps=tuple(cast_steps)),
        grid=(n_chunks,),
        in_specs=[pl.BlockSpec((tm, d), lambda k: (0, 0), pipeline_mode=pl.Buffered(1)),
                  pl.BlockSpec((1, d), lambda k: (0, 0))]
                 + w_specs + [pl.BlockSpec((1, d), lambda k: (0, 0))] + cast_specs,
        out_specs=[pl.BlockSpec((tm, d), lambda k: (0, 0), pipeline_mode=pl.Buffered(1))] + w_specs + cast_specs,
        out_shape=[jax.ShapeDtypeStruct((tm, d), F32)]
                  + [jax.ShapeDtypeStruct(w.shape, BF16) for w in (wg32, wu32, wd32) + tuple(cast)],
        scratch_shapes=[pltpu.VMEM((tm, d), BF16), pltpu.VMEM((tm, d), F32)],
        compiler_params=_params(("arbitrary",)),
        name="ffn_first",
    )(x, pre_g, wg32, wu32, wd32, post_g, *cast)


def _ffn(x, pre_g, wg, wu, wd, post_g, *, tm, tf, rows_done=0):
    t, d = x.shape
    f = wg.shape[1]
    assert rows_done % tm == 0
    skip = rows_done // tm
    return pl.pallas_call(
        _ffn_kernel,
        grid=(t // tm - skip, f // tf),
        in_specs=[
            pl.BlockSpec((tm, d), lambda i, k: (i + skip, 0)),
            pl.BlockSpec((1, d), lambda i, k: (0, 0)),
            pl.BlockSpec((d, tf), lambda i, k: (0, k)),
            pl.BlockSpec((d, tf), lambda i, k: (0, k)),
            pl.BlockSpec((tf, d), lambda i, k: (k, 0)),
            pl.BlockSpec((1, d), lambda i, k: (0, 0)),
        ],
        out_specs=pl.BlockSpec((tm, d), lambda i, k: (i, 0)),
        out_shape=jax.ShapeDtypeStruct((t - rows_done, d), F32),
        scratch_shapes=[pltpu.VMEM((tm, d), BF16), pltpu.VMEM((tm, d), F32)],
        compiler_params=_params(("parallel", "arbitrary")),
        name="ffn",
    )(x, pre_g, wg, wu, wd, post_g)


def _row_tile_specs(parts, tm):
    head, tail = parts
    n_head = head.shape[0] // tm
    assert head.shape[0] % tm == 0 and tail.shape[0] % tm == 0 and head.shape[1] == tail.shape[1]
    d = head.shape[1]
    specs = [pl.BlockSpec((tm, d), lambda i: (jnp.minimum(i, n_head - 1), 0)),
             pl.BlockSpec((tm, d), lambda i: (jnp.maximum(i - n_head, 0), 0))]
    return specs, n_head


def _row_tile(head_ref, tail_ref, n_head):
    return jnp.where(pl.program_id(0) < n_head, head_ref[...], tail_ref[...])


def _in_proj_kernel(xh_ref, xt_ref, g_ref, w_ref, qkv_ref, u_ref, *, q_scale, n_head):
    xf = _row_tile(xh_ref, xt_ref, n_head)
    hn = (xf * _rms_scale(xf) * g_ref[...]).astype(BF16)
    tn = u_ref.shape[1]
    for n in range(4):
        acc = jnp.dot(hn, w_ref[:, n * tn:(n + 1) * tn], preferred_element_type=F32)
        if n == 0:
            qkv_ref[:, :tn] = (acc * q_scale).astype(BF16)
        elif n < 3:
            qkv_ref[:, n * tn:(n + 1) * tn] = acc.astype(BF16)
        else:
            u_ref[...] = acc


def _in_proj(x_parts, g, w, *, tm, q_scale):
    t = x_parts[0].shape[0] + x_parts[1].shape[0]
    d = x_parts[0].shape[1]
    tn = w.shape[1] // 4
    x_specs, n_head = _row_tile_specs(x_parts, tm)
    return pl.pallas_call(
        functools.partial(_in_proj_kernel, q_scale=q_scale, n_head=n_head),
        grid=(t // tm,),
        in_specs=x_specs + [
            pl.BlockSpec((1, d), lambda i: (0, 0)),
            pl.BlockSpec(w.shape, lambda i: (0, 0), pipeline_mode=pl.Buffered(1)),
        ],
        out_specs=[
            pl.BlockSpec((tm, 3 * tn), lambda i: (i, 0)),
            pl.BlockSpec((tm, tn), lambda i: (i, 0)),
        ],
        out_shape=[jax.ShapeDtypeStruct((t, 3 * tn), BF16),
                   jax.ShapeDtypeStruct((t, tn), F32)],
        compiler_params=_params(("parallel",)),
        name="in_proj",
    )(*x_parts, g, w)


def _bias_band_kernel(rb_ref, o_ref, *, tq, scale):
    h = pl.program_id(0)
    key = lax.broadcasted_iota(jnp.int32, (2 * tq, tq), 0)
    qry = lax.broadcasted_iota(jnp.int32, (2 * tq, tq), 1)
    dist = tq + qry - key
    n = jnp.maximum(dist, 0)
    max_exact = N_BUCKETS // 2
    nf = jnp.maximum(n, 1).astype(F32)
    large = max_exact + (jnp.log(nf / max_exact) / math.log(MAX_DISTANCE / max_exact)
                         * (N_BUCKETS - max_exact)).astype(jnp.int32)
    large = jnp.minimum(large, N_BUCKETS - 1)
    bucket = jnp.where(n < max_exact, n, large)
    val = jnp.zeros((2 * tq, tq), F32)
    for b in range(N_BUCKETS):
        val = jnp.where(bucket == b, rb_ref[b, h], val)
    o_ref[0] = jnp.where(dist >= 0, (val - rb_ref[N_BUCKETS - 1, h]) * scale, NEG_INF)


def _bias_band(rel_bias, *, tq, scale):
    assert tq >= MAX_DISTANCE
    n_heads = rel_bias.shape[1]
    return pl.pallas_call(
        functools.partial(_bias_band_kernel, tq=tq, scale=scale),
        grid=(n_heads,),
        in_specs=[pl.BlockSpec(memory_space=pltpu.SMEM)],
        out_specs=pl.BlockSpec((1, 2 * tq, tq), lambda h: (h, 0, 0)),
        out_shape=jax.ShapeDtypeStruct((n_heads, 2 * tq, tq), F32),
        compiler_params=_params(("arbitrary",)),
        name="bias_band",
    )(rel_bias)


VT_ROWS = V_HEAD_DIM + 16
LOG2_E = math.log2(math.e)


def _attn_kernel(lq1_ref, lk1_ref, lq2_ref, lk2_ref, q1_ref, q2_ref, k1_ref, k2_ref,
                 v_ref, band_ref, g_ref, *rest, tq, cast_rows):
    n_cast = len(cast_rows)
    cast_in = rest[:n_cast]
    o_ref = rest[n_cast]
    cast_out = rest[n_cast + 1:2 * n_cast + 1]
    vt_ref, qm_ref, s_ref, m_ref, acc_ref = rest[2 * n_cast + 1:]
    n_blocks = v_ref.shape[1] // tq
    q_refs = (q1_ref, q2_ref)
    k_refs = (k1_ref, k2_ref)

    ones_pad = (lax.broadcasted_iota(jnp.int32, (VT_ROWS - V_HEAD_DIM, tq), 0) == 0).astype(BF16)
    feat = lax.broadcasted_iota(jnp.int32, (2 * ATTN_HEAD_DIM, tq), 0)
    for c in range(n_blocks):
        rows = slice(c * tq, (c + 1) * tq)
        vt = v_ref[0, rows, :].astype(F32).T.astype(BF16)
        for hh in range(2):
            vt_ref[c, hh, :V_HEAD_DIM, :] = vt[hh * V_HEAD_DIM:(hh + 1) * V_HEAD_DIM]
            vt_ref[c, hh, V_HEAD_DIM:, :] = ones_pad
        for mi in range(2):
            qt = q_refs[mi][0, rows, :].astype(F32).T
            for hh in range(2):
                qm_ref[c, mi, :, hh * tq:(hh + 1) * tq] = jnp.where(
                    (feat >= ATTN_HEAD_DIM) == (hh == 1), qt, 0.0).astype(BF16)

    lam = (jnp.exp(jnp.sum(lq1_ref[...] * lk1_ref[...], axis=-1, keepdims=True))
           - jnp.exp(jnp.sum(lq2_ref[...] * lk2_ref[...], axis=-1, keepdims=True)) + LAMBDA_INIT)

    def score_phase(i, j, buf):
        rows = slice(j * tq, (j + 1) * tq)
        band_rows = slice(tq, 2 * tq) if j == i else slice(0, tq)
        for mi in range(2):
            s = jnp.dot(k_refs[mi][0, rows, :], qm_ref[i, mi], preferred_element_type=F32)
            if j >= i - 1:
                s = s + jnp.concatenate([band_ref[0, band_rows, :], band_ref[1, band_rows, :]], axis=1)
            s_ref[buf, mi] = s

    def value_phase(j, buf):
        alphas, ps = [], []
        for mi in range(2):
            s = s_ref[buf, mi]
            m_new = jnp.max(s, axis=0, keepdims=True)
            if j > 0:
                m_old = m_ref[mi]
                m_new = jnp.maximum(m_old, m_new)
                alphas.append(jnp.exp2(m_old - m_new))
            ps.append(jnp.exp2(s - m_new).astype(BF16))
            m_ref[mi] = m_new
        for hh in range(2):
            head = slice(hh * tq, (hh + 1) * tq)
            p2 = jnp.concatenate([ps[0][:, head], ps[1][:, head]], axis=1)
            pv = jnp.dot(vt_ref[j, hh], p2, preferred_element_type=F32)
            if j > 0:
                alpha2 = jnp.concatenate([alphas[0][:, head], alphas[1][:, head]], axis=1)
                pv = alpha2 * acc_ref[1 - buf, hh] + pv
            acc_ref[buf, hh] = pv

    def finish_block(i, buf):
        out_rows = slice(i * tq, (i + 1) * tq)
        den = V_HEAD_DIM
        for hh in range(2):
            a = (acc_ref[buf, hh, :den, :tq] / acc_ref[buf, hh, den:den + 1, :tq]
                 - lam * (acc_ref[buf, hh, :den, tq:] / acc_ref[buf, hh, den:den + 1, tq:]))
            scale = lax.rsqrt(jnp.mean(a * a, axis=0, keepdims=True) + RMS_EPS)
            a = a * scale * g_ref[...] * (1.0 - LAMBDA_INIT)
            o_ref[0, out_rows, hh * V_HEAD_DIM:(hh + 1) * V_HEAD_DIM] = a.T.astype(o_ref.dtype)

    pairs = [(i, j) for i in range(n_blocks) for j in range(i + 1)]
    score_phase(0, 0, 0)
    for n, (i, j) in enumerate(pairs):
        for src, dst, rows_per_it in zip(cast_in, cast_out, cast_rows):
            if n < src.shape[0] // rows_per_it:
                rows = slice(n * rows_per_it, (n + 1) * rows_per_it)
                dst[rows, :] = src[rows, :].astype(BF16)
        if n + 1 < len(pairs):
            score_phase(*pairs[n + 1], (n + 1) % 2)
        value_phase(j, n % 2)
        if j == i:
            finish_block(i, n % 2)


def _attention(qkv, band, lq1, lk1, lq2, lk2, subln_g, *, batch, seq, n_heads, tq, cast=()):
    d_attn = n_heads * V_HEAD_DIM
    pair = 2 * ATTN_HEAD_DIM
    vpair = 2 * V_HEAD_DIM
    n_pairs = d_attn // 2 // pair
    n_blocks = seq // tq
    n_steps = batch * n_pairs
    n_iters = n_blocks * (n_blocks + 1) // 2
    cast_rows = []
    for w in cast:
        block_rows = w.shape[0] // n_steps
        rows_per_it = 16
        while block_rows % rows_per_it or block_rows // rows_per_it > n_iters:
            rows_per_it += 16
        assert w.shape[0] % n_steps == 0 and rows_per_it <= block_rows
        cast_rows.append(rows_per_it)
    cast_specs = [pl.BlockSpec((w.shape[0] // n_steps, w.shape[1]), lambda b, p: (b * n_pairs + p, 0))
                  for w in cast]
    qkv3 = qkv.reshape(batch, seq, 3 * d_attn)
    small = pl.BlockSpec((1, ATTN_HEAD_DIM), lambda b, p: (0, 0))
    return pl.pallas_call(
        functools.partial(_attn_kernel, tq=tq, cast_rows=tuple(cast_rows)),
        grid=(batch, n_pairs),
        in_specs=[
            small, small, small, small,
            pl.BlockSpec((1, seq, pair), lambda b, p: (b, 0, p)),
            pl.BlockSpec((1, seq, pair), lambda b, p: (b, 0, n_pairs + p)),
            pl.BlockSpec((1, seq, pair), lambda b, p: (b, 0, 2 * n_pairs + p)),
            pl.BlockSpec((1, seq, pair), lambda b, p: (b, 0, 3 * n_pairs + p)),
            pl.BlockSpec((1, seq, vpair), lambda b, p: (b, 0, 2 * n_pairs + p)),
            pl.BlockSpec((2, 2 * tq, tq), lambda b, p: (p, 0, 0)),
            pl.BlockSpec((V_HEAD_DIM, 1), lambda b, p: (0, 0)),
        ] + cast_specs,
        out_specs=[pl.BlockSpec((1, seq, vpair), lambda b, p: (b, 0, p))] + cast_specs,
        out_shape=[jax.ShapeDtypeStruct((batch, seq, d_attn), BF16)]
                  + [jax.ShapeDtypeStruct(w.shape, BF16) for w in cast],
        scratch_shapes=[pltpu.VMEM((n_blocks, 2, VT_ROWS, tq), BF16),
                        pltpu.VMEM((n_blocks, 2, pair, 2 * tq), BF16),
                        pltpu.VMEM((2, 2, tq, 2 * tq), F32),
                        pltpu.VMEM((2, 1, 2 * tq), F32),
                        pltpu.VMEM((2, 2, VT_ROWS, 2 * tq), F32)],
        compiler_params=_params(("parallel", "parallel")),
        name="diff_attention",
    )(lq1, lk1, lq2, lk2, qkv3, qkv3, qkv3, qkv3, qkv3, band, subln_g, *cast)


def _block_diag_blocks(w, dst_ref):
    n_blocks, rows, cols = dst_ref.shape
    g, p, n = w.shape
    gl = g // n_blocks
    row_group = lax.broadcasted_iota(jnp.int32, (rows, cols), 0) // p
    col_group = lax.broadcasted_iota(jnp.int32, (rows, cols), 1) // n
    for j in range(n_blocks):
        stacked = w[j * gl:(j + 1) * gl].reshape(rows, n)
        tiled = jnp.concatenate([stacked] * gl, axis=1)
        dst_ref[j] = jnp.where(row_group == col_group, tiled, 0.0).astype(BF16)


def _ssm_params_kernel(a_re_ref, a_im_ref, log_dt_ref, b_re_ref, b_im_ref, c_re_ref, c_im_ref,
                       ab_re_ref, ab_im_ref, bd_re_ref, bd_im_ref, cd_re_ref, cd_im_ref):
    ar = a_re_ref[...]
    ai = a_im_ref[...]
    dt = jnp.exp(log_dt_ref[...])
    decay = jnp.exp(dt * ar)
    ab_re = decay * jnp.cos(dt * ai)
    ab_im = decay * jnp.sin(dt * ai)
    den = ar * ar + ai * ai
    nr = ab_re - 1.0
    ni = ab_im
    coef_re = (nr * ar + ni * ai) / den
    coef_im = (ni * ar - nr * ai) / den
    br = b_re_ref[...]
    bi = b_im_ref[...]
    ab_re_ref[...] = ab_re
    ab_im_ref[...] = ab_im
    _block_diag_blocks(coef_re * br - coef_im * bi, bd_re_ref)
    _block_diag_blocks(coef_re * bi + coef_im * br, bd_im_ref)
    _block_diag_blocks(c_re_ref[...], cd_re_ref)
    _block_diag_blocks(c_im_ref[...], cd_im_ref)


def _ssm_params(a_re, a_im, log_dt, b_re_t, b_im_t, c_re, c_im):
    g, _, n = a_re.shape
    p = b_re_t.shape[1]
    gl = g // SSM_BLOCKS
    dense = jax.ShapeDtypeStruct((SSM_BLOCKS, gl * p, gl * n), BF16)
    return pl.pallas_call(
        _ssm_params_kernel,
        out_shape=[jax.ShapeDtypeStruct((g, 1, n), F32), jax.ShapeDtypeStruct((g, 1, n), F32),
                   dense, dense, dense, dense],
        compiler_params=pltpu.CompilerParams(vmem_limit_bytes=V7X_VMEM_LIMIT_BYTES),
        name="ssm_params",
    )(a_re, a_im, log_dt, b_re_t, b_im_t, c_re, c_im)


SSM_COL_CHUNK = 512
SSM_BLOCKS = 4


def _ssm_fill_lhs(u_ref, lhs_ref, *, tb):
    n_seg = u_ref.shape[1]
    for seg in range(n_seg):
        for c in range(lhs_ref.shape[0]):
            lhs_ref[c, pl.ds(seg, tb, stride=n_seg), :] = u_ref[0, seg, :, c * V7X_LANES:(c + 1) * V7X_LANES]


def _ssm_drive_block(j, lhs_ref, bmat_re_ref, bmat_im_ref, bu_re_ref, bu_im_ref):
    per_block = lhs_ref.shape[0] // SSM_BLOCKS
    kst = bu_re_ref.shape[1] // SSM_BLOCKS
    uj = jnp.concatenate([lhs_ref[c] for c in range(j * per_block, (j + 1) * per_block)],
                         axis=-1).astype(BF16)
    bu_re_ref[:, j * kst:(j + 1) * kst] = jnp.dot(uj, bmat_re_ref[j], preferred_element_type=F32)
    bu_im_ref[:, j * kst:(j + 1) * kst] = jnp.dot(uj, bmat_im_ref[j], preferred_element_type=F32)


def _ssm_scan_block(j, ab_re_ref, ab_im_ref, st_re_ref, st_im_ref, bu_re_ref, bu_im_ref, *, tb,
                    x_re_ref=None, x_im_ref=None):
    kst = bu_re_ref.shape[1] // SSM_BLOCKS
    two = 2 * V7X_SUBLANES
    chunks = [slice(j * kst + c * SSM_COL_CHUNK, j * kst + (c + 1) * SSM_COL_CHUNK)
              for c in range(kst // SSM_COL_CHUNK)]
    coef = [(jnp.broadcast_to(ab_re_ref[:, cs], (V7X_SUBLANES, SSM_COL_CHUNK)),
             jnp.broadcast_to(ab_im_ref[:, cs], (V7X_SUBLANES, SSM_COL_CHUNK))) for cs in chunks]
    state = [(st_re_ref[:, cs], st_im_ref[:, cs]) for cs in chunks]
    for t2 in range(tb // 2):
        rows = slice(t2 * two, (t2 + 1) * two)
        for n, cs in enumerate(chunks):
            ar, ai = coef[n]
            xr, xi = state[n]
            bur = bu_re_ref[rows, cs]
            bui = bu_im_ref[rows, cs]
            xr1 = ar * xr - ai * xi + bur[:V7X_SUBLANES]
            xi1 = ar * xi + ai * xr + bui[:V7X_SUBLANES]
            xr2 = ar * xr1 - ai * xi1 + bur[V7X_SUBLANES:]
            xi2 = ar * xi1 + ai * xr1 + bui[V7X_SUBLANES:]
            if x_re_ref is not None:
                x_re_ref[rows, cs] = jnp.concatenate([xr1, xr2], axis=0).astype(BF16)
                x_im_ref[rows, cs] = jnp.concatenate([xi1, xi2], axis=0).astype(BF16)
            state[n] = (xr2, xi2)
    for (xr, xi), cs in zip(state, chunks):
        st_re_ref[:, cs] = xr
        st_im_ref[:, cs] = xi


def _ssm_ends_kernel(u_ref, bmat_re_ref, bmat_im_ref, ab_re_ref, ab_im_ref, end_re_ref, end_im_ref,
                     lhs_ref, bu_re_ref, bu_im_ref, st_re_ref, st_im_ref, *, tb):
    tblk = pl.program_id(1)

    @pl.when(tblk == 0)
    def _():
        st_re_ref[...] = jnp.zeros_like(st_re_ref)
        st_im_ref[...] = jnp.zeros_like(st_im_ref)

    _ssm_fill_lhs(u_ref, lhs_ref, tb=tb)
    drive = functools.partial(_ssm_drive_block, lhs_ref=lhs_ref, bmat_re_ref=bmat_re_ref,
                              bmat_im_ref=bmat_im_ref, bu_re_ref=bu_re_ref, bu_im_ref=bu_im_ref)
    drive(0)
    for j in range(SSM_BLOCKS):
        if j + 1 < SSM_BLOCKS:
            drive(j + 1)
        _ssm_scan_block(j, ab_re_ref, ab_im_ref, st_re_ref, st_im_ref, bu_re_ref, bu_im_ref, tb=tb)

    @pl.when(tblk == pl.num_programs(1) - 1)
    def _():
        end_re_ref[0] = st_re_ref[...]
        end_im_ref[0] = st_im_ref[...]


def _ssm_main_kernel(u_ref, end_re_ref, end_im_ref, bmat_re_ref, bmat_im_ref, ab_re_ref, ab_im_ref,
                     cmat_re_ref, cmat_im_ref, d_ref, wglu_ref, bglu_ref, o_ref,
                     lhs_ref, bu_re_ref, bu_im_ref, st_re_ref, st_im_ref, x_re_ref, x_im_ref, *, tb, seg_len):
    tblk = pl.program_id(1)
    n_seg = st_re_ref.shape[0]

    @pl.when(tblk == 0)
    def _():
        pr = ab_re_ref[...]
        pi = ab_im_ref[...]
        for _ in range(int(math.log2(seg_len))):
            pr, pi = pr * pr - pi * pi, 2.0 * pr * pi
        er = jnp.zeros_like(pr)
        ei = jnp.zeros_like(pi)
        st_re_ref[0:1, :] = er
        st_im_ref[0:1, :] = ei
        for j in range(1, n_seg):
            lr = end_re_ref[0, j - 1:j, :]
            li = end_im_ref[0, j - 1:j, :]
            er, ei = lr + pr * er - pi * ei, li + pr * ei + pi * er
            st_re_ref[j:j + 1, :] = er
            st_im_ref[j:j + 1, :] = ei

    _ssm_fill_lhs(u_ref, lhs_ref, tb=tb)
    drive = functools.partial(_ssm_drive_block, lhs_ref=lhs_ref, bmat_re_ref=bmat_re_ref,
                              bmat_im_ref=bmat_im_ref, bu_re_ref=bu_re_ref, bu_im_ref=bu_im_ref)
    n_slabs = lhs_ref.shape[0]
    kst = bu_re_ref.shape[1] // SSM_BLOCKS
    nt = (((1,), (1,)), ((), ()))
    ys = []
    drive(0)
    for j in range(SSM_BLOCKS):
        if j + 1 < SSM_BLOCKS:
            drive(j + 1)
        _ssm_scan_block(j, ab_re_ref, ab_im_ref, st_re_ref, st_im_ref, bu_re_ref, bu_im_ref, tb=tb,
                        x_re_ref=x_re_ref, x_im_ref=x_im_ref)
        xr = x_re_ref[:, j * kst:(j + 1) * kst]
        xi = x_im_ref[:, j * kst:(j + 1) * kst]
        ys.append(lax.dot_general(xr, cmat_re_ref[j], nt, preferred_element_type=F32)
                  - lax.dot_general(xi, cmat_im_ref[j], nt, preferred_element_type=F32))
    u_rows = jnp.concatenate([lhs_ref[c] for c in range(n_slabs)], axis=-1)
    y = jnp.concatenate(ys, axis=-1) + d_ref[...] * u_rows
    z = jax.nn.gelu(y)
    gate = jnp.dot(z.astype(BF16), wglu_ref[...], preferred_element_type=F32) + bglu_ref[...]
    s = z * jax.nn.sigmoid(gate)
    for c in range(n_slabs):
        lhs_ref[c] = s[:, c * V7X_LANES:(c + 1) * V7X_LANES]
    for seg in range(n_seg):
        for c in range(n_slabs):
            o_ref[0, seg, :, c * V7X_LANES:(c + 1) * V7X_LANES] = lhs_ref[c, pl.ds(seg, tb, stride=n_seg), :]


def _ssm(u_seg, bmat_re, bmat_im, ab_re, ab_im, cmat_re, cmat_im, d_skip, w_glu, b_glu, *, tb):
    batch, n_seg, seg_len, d_ssm = u_seg.shape
    n_cols = ab_re.shape[1]
    rows = tb * n_seg
    grid = (batch, seg_len // tb)
    u_spec = pl.BlockSpec((1, n_seg, tb, d_ssm), lambda b, t: (b, 0, t, 0))
    end_spec = pl.BlockSpec((1, n_seg, n_cols), lambda b, t: (b, 0, 0))

    def whole(a):
        return pl.BlockSpec(a.shape, lambda b, t: (0,) * a.ndim, pipeline_mode=pl.Buffered(1))

    scratch = [pltpu.VMEM((d_ssm // V7X_LANES, rows, V7X_LANES), F32), pltpu.VMEM((rows, n_cols), F32),
               pltpu.VMEM((rows, n_cols), F32), pltpu.VMEM((n_seg, n_cols), F32),
               pltpu.VMEM((n_seg, n_cols), F32)]
    end_re, end_im = pl.pallas_call(
        functools.partial(_ssm_ends_kernel, tb=tb),
        grid=grid,
        in_specs=[u_spec, whole(bmat_re), whole(bmat_im), whole(ab_re), whole(ab_im)],
        out_specs=[end_spec, end_spec],
        out_shape=[jax.ShapeDtypeStruct((batch, n_seg, n_cols), F32)] * 2,
        scratch_shapes=scratch,
        compiler_params=_params(("parallel", "arbitrary")),
        name="ssm_ends",
    )(u_seg, bmat_re, bmat_im, ab_re, ab_im)
    return pl.pallas_call(
        functools.partial(_ssm_main_kernel, tb=tb, seg_len=seg_len),
        grid=grid,
        in_specs=[u_spec, end_spec, end_spec, whole(bmat_re), whole(bmat_im), whole(ab_re), whole(ab_im),
                  whole(cmat_re), whole(cmat_im), whole(d_skip), whole(w_glu), whole(b_glu)],
        out_specs=u_spec,
        out_shape=jax.ShapeDtypeStruct(u_seg.shape, F32),
        scratch_shapes=scratch + [pltpu.VMEM((rows, n_cols), BF16), pltpu.VMEM((rows, n_cols), BF16)],
        compiler_params=_params(("parallel", "arbitrary")),
        name="ssm_main",
    )(u_seg, end_re, end_im, bmat_re, bmat_im, ab_re, ab_im, cmat_re, cmat_im, d_skip, w_glu, b_glu)


def _out_proj_kernel(a_ref, s_ref, wa_ref, ws_ref, xh_ref, xt_ref, g_ref, o_ref, *, n_head):
    m = (jnp.dot(a_ref[...], wa_ref[...], preferred_element_type=F32)
         + jnp.dot(s_ref[...].astype(BF16), ws_ref[...], preferred_element_type=F32))
    o_ref[...] = _row_tile(xh_ref, xt_ref, n_head) + m * _rms_scale(m) * g_ref[...]


def _out_proj(a, s, w, x_parts, g, *, tm):
    t = a.shape[0]
    d = w.shape[1]
    da = a.shape[1]
    ds = s.shape[1]
    assert da == ds and w.shape[0] == da + ds
    x_specs, n_head = _row_tile_specs(x_parts, tm)
    return pl.pallas_call(
        functools.partial(_out_proj_kernel, n_head=n_head),
        grid=(t // tm,),
        in_specs=[
            pl.BlockSpec((tm, da), lambda i: (i, 0)),
            pl.BlockSpec((tm, ds), lambda i: (i, 0)),
            pl.BlockSpec((da, d), lambda i: (0, 0)),
            pl.BlockSpec((ds, d), lambda i: (1, 0)),
        ] + x_specs + [
            pl.BlockSpec((1, d), lambda i: (0, 0)),
        ],
        out_specs=pl.BlockSpec((tm, d), lambda i: (i, 0)),
        out_shape=jax.ShapeDtypeStruct((t, d), F32),
        compiler_params=_params(("parallel",)),
        name="out_proj",
    )(a, s, w, w, *x_parts, g)


def _tiles(d_model, d_ff, d_ssm, n_states):
    tm, tf, tq, tb = 512, 512, 256, 64
    tm_first = 2 * tm
    f32, bf16 = 4, 2
    ffn_bytes = tm * d_model * (2 * f32 + 2 * f32 + f32 + bf16) + 2 * 3 * d_model * tf * bf16
    first_bytes = (tm_first * d_model * (f32 + f32 + f32 + bf16)
                   + 2 * 3 * d_model * (tf // 2) * (f32 + bf16))
    ssm_bytes = (V7X_SUBLANES * tb) * (d_ssm * f32 * (1 + 2 + 2) + 2 * n_states * (f32 + bf16))
    assert max(ffn_bytes, first_bytes, ssm_bytes) < V7X_VMEM_LIMIT_BYTES and d_ff % tf == 0
    return tm, tf, tm_first, tq, tb


def kernel(x, ffn1_pre_g, ffn1_w_gate, ffn1_w_up, ffn1_w_down, ffn1_post_g, mix_pre_g, w_in, lambda_q1, lambda_k1, lambda_q2, lambda_k2, subln_g, rel_bias, ssm_a_re, ssm_a_im, ssm_b_re, ssm_b_im, ssm_c_re, ssm_c_im, ssm_d, ssm_log_dt, w_glu, b_glu, w_out, mix_post_g, ffn2_pre_g, ffn2_w_gate, ffn2_w_up, ffn2_w_down, ffn2_post_g):
    batch, seq, d_model = x.shape
    depth = ffn1_pre_g.shape[0]
    assert depth == 1, "LAMBDA_INIT is specialised to a single layer"
    n_heads = rel_bias.shape[1]
    d_attn = n_heads * V_HEAD_DIM
    n_groups, n_state = ssm_a_re.shape[1:]
    d_ssm = n_groups * SSM_GROUP
    n_seg = V7X_SUBLANES
    seg_len = seq // n_seg
    tokens = batch * seq
    tm, tf, tm_first, tq, tb = _tiles(d_model, ffn1_w_gate.shape[2], d_ssm, n_groups * n_state)

    xt = x.reshape(tokens, d_model)
    row = lambda v: v.reshape(1, -1)
    l = 0

    x1_head, wg1, wu1, wd1, w_in_bf, w_glu_bf = _ffn_first(
        xt, row(ffn1_pre_g[l]), ffn1_w_gate[l], ffn1_w_up[l], ffn1_w_down[l], row(ffn1_post_g[l]),
        tm=tm_first, tf=tf // 2, cast=(w_in[l], w_glu[l]))
    x1_tail = _ffn(xt, row(ffn1_pre_g[l]), wg1, wu1, wd1, row(ffn1_post_g[l]), tm=tm, tf=tf,
                   rows_done=tm_first)
    x1 = (x1_head, x1_tail)

    qkv, u = _in_proj(x1, row(mix_pre_g[l]), w_in_bf, tm=tm, q_scale=ATTN_HEAD_DIM ** -0.5 * LOG2_E)

    band = _bias_band(rel_bias, tq=tq, scale=LOG2_E)
    a, wg2, wu2, wd2, wo = _attention(
        qkv, band, row(lambda_q1[l]), row(lambda_k1[l]), row(lambda_q2[l]), row(lambda_k2[l]),
        subln_g[l].reshape(-1, 1), batch=batch, seq=seq, n_heads=n_heads, tq=tq,
        cast=(ffn2_w_gate[l], ffn2_w_up[l], ffn2_w_down[l], w_out[l]))

    ab_re, ab_im, bmat_re, bmat_im, cmat_re, cmat_im = _ssm_params(
        ssm_a_re[l].reshape(n_groups, 1, n_state), ssm_a_im[l].reshape(n_groups, 1, n_state),
        ssm_log_dt[l].reshape(n_groups, 1, 1),
        jnp.swapaxes(ssm_b_re[l], 1, 2), jnp.swapaxes(ssm_b_im[l], 1, 2), ssm_c_re[l], ssm_c_im[l])
    u_seg = u.reshape(batch, n_seg, seg_len, d_ssm)
    s_seg = _ssm(u_seg, bmat_re, bmat_im, ab_re.reshape(1, -1), ab_im.reshape(1, -1), cmat_re, cmat_im,
                 row(ssm_d[l]), w_glu_bf, row(b_glu[l]), tb=tb)
    s = s_seg.reshape(tokens, d_ssm)

    x2 = _out_proj(a.reshape(tokens, d_attn), s, wo, x1, row(mix_post_g[l]), tm=tm)

    x3 = _ffn(x2, row(ffn2_pre_g[l]), wg2, wu2, wd2, row(ffn2_post_g[l]), tm=tm, tf=tf)
    return x3.reshape(batch, seq, d_model)
```

```python
import functools
import math

import jax
import jax.numpy as jnp
from jax import lax
from jax.experimental import pallas as pl
from jax.experimental.pallas import tpu as pltpu

V7X_LANES = 128
V7X_SUBLANES = 8
V7X_MXU_COLS = 256
V7X_VMEM_LIMIT_BYTES = 56 * 1024 * 1024

RMS_EPS = 1e-6
NEG_INF = -1e30
N_BUCKETS = 32
MAX_DISTANCE = 128
ATTN_HEAD_DIM = 64
V_HEAD_DIM = 128
SSM_GROUP = 16
LAMBDA_INIT = 0.8 - 0.6 * math.exp(-0.3 * 0)

F32 = jnp.float32
BF16 = jnp.bfloat16
EPILOGUE_ROWS = 128


def _params(semantics):
    return pltpu.CompilerParams(dimension_semantics=semantics,
                                vmem_limit_bytes=V7X_VMEM_LIMIT_BYTES)


def _rms_scale(v):
    return lax.rsqrt(jnp.mean(v * v, axis=-1, keepdims=True) + RMS_EPS)


def _ffn_prologue(x_ref, pre_g_ref, xn_ref, acc_ref):
    xf = x_ref[...]
    xn_ref[...] = (xf * _rms_scale(xf) * pre_g_ref[...]).astype(BF16)
    acc_ref[...] = jnp.zeros_like(acc_ref)


def _ffn_chunk(xn_ref, acc_ref, wg_ref, wu_ref, wd_ref):
    xn = xn_ref[...]
    n_split = max(1, wg_ref.shape[1] // V7X_MXU_COLS)
    half = wg_ref.shape[1] // n_split
    pre = []
    for c in range(n_split):
        cols = slice(c * half, (c + 1) * half)
        pre.append((jnp.dot(xn, wg_ref[:, cols], preferred_element_type=F32),
                    jnp.dot(xn, wu_ref[:, cols], preferred_element_type=F32)))
    out = acc_ref[...]
    for c, (gate, up) in enumerate(pre):
        hidden = (gate * jax.nn.sigmoid(gate) * up).astype(BF16)
        out = out + jnp.dot(hidden, wd_ref[c * half:(c + 1) * half, :], preferred_element_type=F32)
    acc_ref[...] = out


def _ffn_epilogue(x_ref, acc_ref, post_g_ref, o_ref):
    scale = _rms_scale(acc_ref[...])
    g_half = 0.5 * post_g_ref[...]
    for r in range(0, acc_ref.shape[0], EPILOGUE_ROWS):
        rows = slice(r, r + EPILOGUE_ROWS)
        o_ref[rows, :] = x_ref[rows, :] + acc_ref[rows, :] * scale[rows] * g_half


def _ffn_kernel(x_ref, pre_g_ref, wg_ref, wu_ref, wd_ref, *rest):
    tail_refs = rest[:-3]
    post_g_ref, o_ref, xn_ref = rest[-3:]
    k = pl.program_id(1)

    @pl.when(k == 0)
    def _():
        _ffn_prologue(x_ref, pre_g_ref, xn_ref, o_ref)
        if tail_refs:
            _ffn_chunk(xn_ref, o_ref, *tail_refs)

    _ffn_chunk(xn_ref, o_ref, wg_ref, wu_ref, wd_ref)

    @pl.when(k == pl.num_programs(1) - 1)
    def _():
        _ffn_epilogue(x_ref, o_ref, post_g_ref, o_ref)


def _ffn_first_kernel(x_ref, pre_g_ref, wg32_ref, wu32_ref, wd32_ref, post_g_ref, *rest, cast_steps):
    n_cast = len(cast_steps)
    cast_in = rest[:n_cast]
    o_ref, wg_ref, wu_ref, wd_ref = rest[n_cast:n_cast + 4]
    cast_out = rest[n_cast + 4:2 * n_cast + 4]
    xn_ref, acc_ref = rest[2 * n_cast + 4:]
    k = pl.program_id(0)

    @pl.when(k == 0)
    def _():
        _ffn_prologue(x_ref, pre_g_ref, xn_ref, acc_ref)

    for src, dst, steps in zip(cast_in, cast_out, cast_steps):
        @pl.when(k < steps)
        def _(src=src, dst=dst):
            dst[...] = src[...].astype(BF16)

    wg = wg32_ref[...].astype(BF16)
    wu = wu32_ref[...].astype(BF16)
    wd = wd32_ref[...].astype(BF16)
    wg_ref[...] = wg
    wu_ref[...] = wu
    wd_ref[...] = wd
    _ffn_chunk(xn_ref, acc_ref, wg, wu, wd)

    @pl.when(k == pl.num_programs(0) - 1)
    def _():
        _ffn_epilogue(x_ref, acc_ref, post_g_ref, o_ref)


def _ffn_first(x, pre_g, wg32, wu32, wd32, post_g, *, tm, tf, cast=()):
    t, d = x.shape
    f = wg32.shape[1]
    n_chunks = f // tf
    cast_steps = []
    for w in cast:
        steps = 1
        while steps * 2 <= n_chunks and w.shape[0] % (16 * steps * 2) == 0:
            steps *= 2
        cast_steps.append(steps)
    cast_specs = [pl.BlockSpec((w.shape[0] // steps, w.shape[1]),
                               lambda k, steps=steps: (jnp.minimum(k, steps - 1), 0))
                  for w, steps in zip(cast, cast_steps)]
    w_specs = [pl.BlockSpec((d, tf), lambda k: (0, k)), pl.BlockSpec((d, tf), lambda k: (0, k)),
               pl.BlockSpec((tf, d), lambda k: (k, 0))]
    return pl.pallas_call(
        functools.partial(_ffn_first_kernel, cast_steps=tuple(cast_steps)),
        grid=(n_chunks,),
        in_specs=[pl.BlockSpec((tm, d), lambda k: (0, 0), pipeline_mode=pl.Buffered(1)),
                  pl.BlockSpec((1, d), lambda k: (0, 0))]
                 + w_specs + [pl.BlockSpec((1, d), lambda k: (0, 0))] + cast_specs,
        out_specs=[pl.BlockSpec((tm, d), lambda k: (0, 0), pipeline_mode=pl.Buffered(1))] + w_specs + cast_specs,
        out_shape=[jax.ShapeDtypeStruct((tm, d), F32)]
                  + [jax.ShapeDtypeStruct(w.shape, BF16) for w in (wg32, wu32, wd32) + tuple(cast)],
        scratch_shapes=[pltpu.VMEM((tm, d), BF16), pltpu.VMEM((tm, d), F32)],
        compiler_params=_params(("arbitrary",)),
        name="ffn_first",
    )(x, pre_g, wg32, wu32, wd32, post_g, *cast)


def _ffn(x, pre_g, wg, wu, wd, post_g, *, tm, tf, rows_done=0):
    t, d = x.shape
    f = wg.shape[1]
    assert rows_done % tm == 0
    skip = rows_done // tm
    tail = f % tf
    tail_specs, tail_args = [], []
    if tail:
        assert f % tail == 0
        last = f // tail - 1
        once = pl.Buffered(1)
        tail_specs = [pl.BlockSpec((d, tail), lambda i, k: (0, last), pipeline_mode=once),
                      pl.BlockSpec((d, tail), lambda i, k: (0, last), pipeline_mode=once),
                      pl.BlockSpec((tail, d), lambda i, k: (last, 0), pipeline_mode=once)]
        tail_args = [wg, wu, wd]
    return pl.pallas_call(
        _ffn_kernel,
        grid=(t // tm - skip, f // tf),
        in_specs=[
            pl.BlockSpec((tm, d), lambda i, k: (i + skip, 0)),
            pl.BlockSpec((1, d), lambda i, k: (0, 0)),
            pl.BlockSpec((d, tf), lambda i, k: (0, k)),
            pl.BlockSpec((d, tf), lambda i, k: (0, k)),
            pl.BlockSpec((tf, d), lambda i, k: (k, 0)),
        ] + tail_specs + [pl.BlockSpec((1, d), lambda i, k: (0, 0))],
        out_specs=pl.BlockSpec((tm, d), lambda i, k: (i, 0)),
        out_shape=jax.ShapeDtypeStruct((t - rows_done, d), F32),
        scratch_shapes=[pltpu.VMEM((tm, d), BF16)],
        compiler_params=_params(("parallel", "arbitrary")),
        name="ffn",
    )(x, pre_g, wg, wu, wd, *tail_args, post_g)


def _row_tile_specs(parts, tm):
    head, tail = parts
    n_head = head.shape[0] // tm
    assert head.shape[0] % tm == 0 and tail.shape[0] % tm == 0 and head.shape[1] == tail.shape[1]
    d = head.shape[1]
    specs = [pl.BlockSpec((tm, d), lambda i: (jnp.minimum(i, n_head - 1), 0)),
             pl.BlockSpec((tm, d), lambda i: (jnp.maximum(i - n_head, 0), 0))]
    return specs, n_head


def _row_tile(head_ref, tail_ref, n_head):
    return jnp.where(pl.program_id(0) < n_head, head_ref[...], tail_ref[...])


def _in_proj_kernel(xh_ref, xt_ref, g_ref, w_ref, qkv_ref, u_ref, *, q_scale, n_head):
    xf = _row_tile(xh_ref, xt_ref, n_head)
    hn = (xf * _rms_scale(xf) * g_ref[...]).astype(BF16)
    tn = u_ref.shape[1]
    for n in range(4):
        acc = jnp.dot(hn, w_ref[:, n * tn:(n + 1) * tn], preferred_element_type=F32)
        if n == 0:
            qkv_ref[:, :tn] = (acc * q_scale).astype(BF16)
        elif n < 3:
            qkv_ref[:, n * tn:(n + 1) * tn] = acc.astype(BF16)
        else:
            u_ref[...] = acc


def _in_proj(x_parts, g, w, *, tm, q_scale):
    t = x_parts[0].shape[0] + x_parts[1].shape[0]
    d = x_parts[0].shape[1]
    tn = w.shape[1] // 4
    x_specs, n_head = _row_tile_specs(x_parts, tm)
    return pl.pallas_call(
        functools.partial(_in_proj_kernel, q_scale=q_scale, n_head=n_head),
        grid=(t // tm,),
        in_specs=x_specs + [
            pl.BlockSpec((1, d), lambda i: (0, 0)),
            pl.BlockSpec(w.shape, lambda i: (0, 0), pipeline_mode=pl.Buffered(1)),
        ],
        out_specs=[
            pl.BlockSpec((tm, 3 * tn), lambda i: (i, 0)),
            pl.BlockSpec((tm, tn), lambda i: (i, 0)),
        ],
        out_shape=[jax.ShapeDtypeStruct((t, 3 * tn), BF16),
                   jax.ShapeDtypeStruct((t, tn), F32)],
        compiler_params=_params(("parallel",)),
        name="in_proj",
    )(*x_parts, g, w)


def _bias_band_kernel(rb_ref, o_ref, *, tq, scale):
    h = pl.program_id(0)
    key = lax.broadcasted_iota(jnp.int32, (2 * tq, tq), 0)
    qry = lax.broadcasted_iota(jnp.int32, (2 * tq, tq), 1)
    dist = tq + qry - key
    n = jnp.maximum(dist, 0)
    max_exact = N_BUCKETS // 2
    nf = jnp.maximum(n, 1).astype(F32)
    large = max_exact + (jnp.log(nf / max_exact) / math.log(MAX_DISTANCE / max_exact)
                         * (N_BUCKETS - max_exact)).astype(jnp.int32)
    large = jnp.minimum(large, N_BUCKETS - 1)
    bucket = jnp.where(n < max_exact, n, large)
    val = jnp.zeros((2 * tq, tq), F32)
    for b in range(N_BUCKETS):
        val = jnp.where(bucket == b, rb_ref[b, h], val)
    o_ref[0] = jnp.where(dist >= 0, (val - rb_ref[N_BUCKETS - 1, h]) * scale, NEG_INF)


def _bias_band(rel_bias, *, tq, scale):
    assert tq >= MAX_DISTANCE
    n_heads = rel_bias.shape[1]
    return pl.pallas_call(
        functools.partial(_bias_band_kernel, tq=tq, scale=scale),
        grid=(n_heads,),
        in_specs=[pl.BlockSpec(memory_space=pltpu.SMEM)],
        out_specs=pl.BlockSpec((1, 2 * tq, tq), lambda h: (h, 0, 0)),
        out_shape=jax.ShapeDtypeStruct((n_heads, 2 * tq, tq), F32),
        compiler_params=_params(("arbitrary",)),
        name="bias_band",
    )(rel_bias)


VT_ROWS = V_HEAD_DIM + 16
LOG2_E = math.log2(math.e)


def _attn_kernel(lq1_ref, lk1_ref, lq2_ref, lk2_ref, q1_ref, q2_ref, k1_ref, k2_ref,
                 v_ref, band_ref, g_ref, *rest, tq, cast_rows):
    n_cast = len(cast_rows)
    cast_in = rest[:n_cast]
    o_ref = rest[n_cast]
    cast_out = rest[n_cast + 1:2 * n_cast + 1]
    vt_ref, qm_ref, s_ref, m_ref, acc_ref = rest[2 * n_cast + 1:]
    n_blocks = v_ref.shape[1] // tq
    q_refs = (q1_ref, q2_ref)
    k_refs = (k1_ref, k2_ref)

    ones_pad = (lax.broadcasted_iota(jnp.int32, (VT_ROWS - V_HEAD_DIM, tq), 0) == 0).astype(BF16)
    feat = lax.broadcasted_iota(jnp.int32, (2 * ATTN_HEAD_DIM, tq), 0)
    for c in range(n_blocks):
        rows = slice(c * tq, (c + 1) * tq)
        vt = v_ref[0, rows, :].astype(F32).T.astype(BF16)
        for hh in range(2):
            vt_ref[c, hh, :V_HEAD_DIM, :] = vt[hh * V_HEAD_DIM:(hh + 1) * V_HEAD_DIM]
            vt_ref[c, hh, V_HEAD_DIM:, :] = ones_pad
        for mi in range(2):
            qt = q_refs[mi][0, rows, :].astype(F32).T
            for hh in range(2):
                qm_ref[c, mi, :, hh * tq:(hh + 1) * tq] = jnp.where(
                    (feat >= ATTN_HEAD_DIM) == (hh == 1), qt, 0.0).astype(BF16)

    lam = (jnp.exp(jnp.sum(lq1_ref[...] * lk1_ref[...], axis=-1, keepdims=True))
           - jnp.exp(jnp.sum(lq2_ref[...] * lk2_ref[...], axis=-1, keepdims=True)) + LAMBDA_INIT)

    def score_phase(i, j, buf):
        rows = slice(j * tq, (j + 1) * tq)
        band_rows = slice(tq, 2 * tq) if j == i else slice(0, tq)
        for mi in range(2):
            s = jnp.dot(k_refs[mi][0, rows, :], qm_ref[i, mi], preferred_element_type=F32)
            if j >= i - 1:
                s = s + jnp.concatenate([band_ref[0, band_rows, :], band_ref[1, band_rows, :]], axis=1)
            s_ref[buf, mi] = s

    def value_phase(j, buf):
        alphas, ps = [], []
        for mi in range(2):
            s = s_ref[buf, mi]
            m_new = jnp.max(s, axis=0, keepdims=True)
            if j > 0:
                m_old = m_ref[mi]
                m_new = jnp.maximum(m_old, m_new)
                alphas.append(jnp.exp2(m_old - m_new))
            ps.append(jnp.exp2(s - m_new).astype(BF16))
            m_ref[mi] = m_new
        for hh in range(2):
            head = slice(hh * tq, (hh + 1) * tq)
            p2 = jnp.concatenate([ps[0][:, head], ps[1][:, head]], axis=1)
            pv = jnp.dot(vt_ref[j, hh], p2, preferred_element_type=F32)
            if j > 0:
                alpha2 = jnp.concatenate([alphas[0][:, head], alphas[1][:, head]], axis=1)
                pv = alpha2 * acc_ref[1 - buf, hh] + pv
            acc_ref[buf, hh] = pv

    def finish_block(i, buf):
        out_rows = slice(i * tq, (i + 1) * tq)
        den = V_HEAD_DIM
        for hh in range(2):
            a = (acc_ref[buf, hh, :den, :tq] / acc_ref[buf, hh, den:den + 1, :tq]
                 - lam * (acc_ref[buf, hh, :den, tq:] / acc_ref[buf, hh, den:den + 1, tq:]))
            scale = lax.rsqrt(jnp.mean(a * a, axis=0, keepdims=True) + RMS_EPS)
            a = a * scale * g_ref[...] * (1.0 - LAMBDA_INIT)
            o_ref[0, out_rows, hh * V_HEAD_DIM:(hh + 1) * V_HEAD_DIM] = a.T.astype(o_ref.dtype)

    pairs = [(i, j) for i in range(n_blocks) for j in range(i + 1)]
    score_phase(0, 0, 0)
    for n, (i, j) in enumerate(pairs):
        for src, dst, rows_per_it in zip(cast_in, cast_out, cast_rows):
            if n < src.shape[0] // rows_per_it:
                rows = slice(n * rows_per_it, (n + 1) * rows_per_it)
                dst[rows, :] = src[rows, :].astype(BF16)
        if n + 1 < len(pairs):
            score_phase(*pairs[n + 1], (n + 1) % 2)
        value_phase(j, n % 2)
        if j == i:
            finish_block(i, n % 2)


def _attention(qkv, band, lq1, lk1, lq2, lk2, subln_g, *, batch, seq, n_heads, tq, cast=()):
    d_attn = n_heads * V_HEAD_DIM
    pair = 2 * ATTN_HEAD_DIM
    vpair = 2 * V_HEAD_DIM
    n_pairs = d_attn // 2 // pair
    n_blocks = seq // tq
    n_steps = batch * n_pairs
    n_iters = n_blocks * (n_blocks + 1) // 2
    cast_rows = []
    for w in cast:
        block_rows = w.shape[0] // n_steps
        rows_per_it = 16
        while block_rows % rows_per_it or block_rows // rows_per_it > n_iters:
            rows_per_it += 16
        assert w.shape[0] % n_steps == 0 and rows_per_it <= block_rows
        cast_rows.append(rows_per_it)
    cast_specs = [pl.BlockSpec((w.shape[0] // n_steps, w.shape[1]), lambda b, p: (b * n_pairs + p, 0))
                  for w in cast]
    qkv3 = qkv.reshape(batch, seq, 3 * d_attn)
    small = pl.BlockSpec((1, ATTN_HEAD_DIM), lambda b, p: (0, 0))
    return pl.pallas_call(
        functools.partial(_attn_kernel, tq=tq, cast_rows=tuple(cast_rows)),
        grid=(batch, n_pairs),
        in_specs=[
            small, small, small, small,
            pl.BlockSpec((1, seq, pair), lambda b, p: (b, 0, p)),
            pl.BlockSpec((1, seq, pair), lambda b, p: (b, 0, n_pairs + p)),
            pl.BlockSpec((1, seq, pair), lambda b, p: (b, 0, 2 * n_pairs + p)),
            pl.BlockSpec((1, seq, pair), lambda b, p: (b, 0, 3 * n_pairs + p)),
            pl.BlockSpec((1, seq, vpair), lambda b, p: (b, 0, 2 * n_pairs + p)),
            pl.BlockSpec((2, 2 * tq, tq), lambda b, p: (p, 0, 0)),
            pl.BlockSpec((V_HEAD_DIM, 1), lambda b, p: (0, 0)),
        ] + cast_specs,
        out_specs=[pl.BlockSpec((1, seq, vpair), lambda b, p: (b, 0, p))] + cast_specs,
        out_shape=[jax.ShapeDtypeStruct((batch, seq, d_attn), BF16)]
                  + [jax.ShapeDtypeStruct(w.shape, BF16) for w in cast],
        scratch_shapes=[pltpu.VMEM((n_blocks, 2, VT_ROWS, tq), BF16),
                        pltpu.VMEM((n_blocks, 2, pair, 2 * tq), BF16),
                        pltpu.VMEM((2, 2, tq, 2 * tq), F32),
                        pltpu.VMEM((2, 1, 2 * tq), F32),
                        pltpu.VMEM((2, 2, VT_ROWS, 2 * tq), F32)],
        compiler_params=_params(("parallel", "parallel")),
        name="diff_attention",
    )(lq1, lk1, lq2, lk2, qkv3, qkv3, qkv3, qkv3, qkv3, band, subln_g, *cast)


def _block_diag_blocks(w, dst_ref):
    n_blocks, rows, cols = dst_ref.shape
    g, p, n = w.shape
    gl = g // n_blocks
    row_group = lax.broadcasted_iota(jnp.int32, (rows, cols), 0) // p
    col_group = lax.broadcasted_iota(jnp.int32, (rows, cols), 1) // n
    for j in range(n_blocks):
        stacked = w[j * gl:(j + 1) * gl].reshape(rows, n)
        tiled = jnp.concatenate([stacked] * gl, axis=1)
        dst_ref[j] = jnp.where(row_group == col_group, tiled, 0.0).astype(BF16)


def _ssm_params_kernel(a_re_ref, a_im_ref, log_dt_ref, b_re_ref, b_im_ref, c_re_ref, c_im_ref,
                       ab_re_ref, ab_im_ref, bd_re_ref, bd_im_ref, cd_re_ref, cd_im_ref):
    ar = a_re_ref[...]
    ai = a_im_ref[...]
    dt = jnp.exp(log_dt_ref[...])
    decay = jnp.exp(dt * ar)
    ab_re = decay * jnp.cos(dt * ai)
    ab_im = decay * jnp.sin(dt * ai)
    den = ar * ar + ai * ai
    nr = ab_re - 1.0
    ni = ab_im
    coef_re = (nr * ar + ni * ai) / den
    coef_im = (ni * ar - nr * ai) / den
    br = b_re_ref[...]
    bi = b_im_ref[...]
    ab_re_ref[...] = ab_re
    ab_im_ref[...] = ab_im
    _block_diag_blocks(coef_re * br - coef_im * bi, bd_re_ref)
    _block_diag_blocks(coef_re * bi + coef_im * br, bd_im_ref)
    _block_diag_blocks(c_re_ref[...], cd_re_ref)
    _block_diag_blocks(c_im_ref[...], cd_im_ref)


def _ssm_params(a_re, a_im, log_dt, b_re_t, b_im_t, c_re, c_im):
    g, _, n = a_re.shape
    p = b_re_t.shape[1]
    gl = g // SSM_BLOCKS
    dense = jax.ShapeDtypeStruct((SSM_BLOCKS, gl * p, gl * n), BF16)
    return pl.pallas_call(
        _ssm_params_kernel,
        out_shape=[jax.ShapeDtypeStruct((g, 1, n), F32), jax.ShapeDtypeStruct((g, 1, n), F32),
                   dense, dense, dense, dense],
        compiler_params=pltpu.CompilerParams(vmem_limit_bytes=V7X_VMEM_LIMIT_BYTES),
        name="ssm_params",
    )(a_re, a_im, log_dt, b_re_t, b_im_t, c_re, c_im)


SSM_COL_CHUNK = 512
SSM_BLOCKS = 4


def _ssm_fill_lhs(u_ref, lhs_ref, *, tb):
    n_seg = u_ref.shape[1]
    for seg in range(n_seg):
        for c in range(lhs_ref.shape[0]):
            lhs_ref[c, pl.ds(seg, tb, stride=n_seg), :] = u_ref[0, seg, :, c * V7X_LANES:(c + 1) * V7X_LANES]


def _ssm_drive_block(j, lhs_ref, bmat_re_ref, bmat_im_ref, bu_re_ref, bu_im_ref):
    per_block = lhs_ref.shape[0] // SSM_BLOCKS
    kst = bu_re_ref.shape[1] // SSM_BLOCKS
    uj = jnp.concatenate([lhs_ref[c] for c in range(j * per_block, (j + 1) * per_block)],
                         axis=-1).astype(BF16)
    bu_re_ref[:, j * kst:(j + 1) * kst] = jnp.dot(uj, bmat_re_ref[j], preferred_element_type=F32)
    bu_im_ref[:, j * kst:(j + 1) * kst] = jnp.dot(uj, bmat_im_ref[j], preferred_element_type=F32)


def _ssm_scan_block(j, ab_re_ref, ab_im_ref, st_re_ref, st_im_ref, bu_re_ref, bu_im_ref, *, tb,
                    x_re_ref=None, x_im_ref=None):
    kst = bu_re_ref.shape[1] // SSM_BLOCKS
    two = 2 * V7X_SUBLANES
    chunks = [slice(j * kst + c * SSM_COL_CHUNK, j * kst + (c + 1) * SSM_COL_CHUNK)
              for c in range(kst // SSM_COL_CHUNK)]
    coef = [(jnp.broadcast_to(ab_re_ref[:, cs], (V7X_SUBLANES, SSM_COL_CHUNK)),
             jnp.broadcast_to(ab_im_ref[:, cs], (V7X_SUBLANES, SSM_COL_CHUNK))) for cs in chunks]
    state = [(st_re_ref[:, cs], st_im_ref[:, cs]) for cs in chunks]
    for t2 in range(tb // 2):
        rows = slice(t2 * two, (t2 + 1) * two)
        for n, cs in enumerate(chunks):
            ar, ai = coef[n]
            xr, xi = state[n]
            bur = bu_re_ref[rows, cs]
            bui = bu_im_ref[rows, cs]
            xr1 = ar * xr - ai * xi + bur[:V7X_SUBLANES]
            xi1 = ar * xi + ai * xr + bui[:V7X_SUBLANES]
            xr2 = ar * xr1 - ai * xi1 + bur[V7X_SUBLANES:]
            xi2 = ar * xi1 + ai * xr1 + bui[V7X_SUBLANES:]
            if x_re_ref is not None:
                x_re_ref[rows, cs] = jnp.concatenate([xr1, xr2], axis=0).astype(BF16)
                x_im_ref[rows, cs] = jnp.concatenate([xi1, xi2], axis=0).astype(BF16)
            state[n] = (xr2, xi2)
    for (xr, xi), cs in zip(state, chunks):
        st_re_ref[:, cs] = xr
        st_im_ref[:, cs] = xi


def _ssm_ends_kernel(u_ref, bmat_re_ref, bmat_im_ref, ab_re_ref, ab_im_ref, end_re_ref, end_im_ref,
                     lhs_ref, bu_re_ref, bu_im_ref, st_re_ref, st_im_ref, *, tb):
    tblk = pl.program_id(1)

    @pl.when(tblk == 0)
    def _():
        st_re_ref[...] = jnp.zeros_like(st_re_ref)
        st_im_ref[...] = jnp.zeros_like(st_im_ref)

    _ssm_fill_lhs(u_ref, lhs_ref, tb=tb)
    drive = functools.partial(_ssm_drive_block, lhs_ref=lhs_ref, bmat_re_ref=bmat_re_ref,
                              bmat_im_ref=bmat_im_ref, bu_re_ref=bu_re_ref, bu_im_ref=bu_im_ref)
    drive(0)
    for j in range(SSM_BLOCKS):
        if j + 1 < SSM_BLOCKS:
            drive(j + 1)
        _ssm_scan_block(j, ab_re_ref, ab_im_ref, st_re_ref, st_im_ref, bu_re_ref, bu_im_ref, tb=tb)

    @pl.when(tblk == pl.num_programs(1) - 1)
    def _():
        end_re_ref[0] = st_re_ref[...]
        end_im_ref[0] = st_im_ref[...]


def _ssm_main_kernel(u_ref, end_re_ref, end_im_ref, bmat_re_ref, bmat_im_ref, ab_re_ref, ab_im_ref,
                     cmat_re_ref, cmat_im_ref, d_ref, wglu_ref, bglu_ref, o_ref,
                     lhs_ref, bu_re_ref, bu_im_ref, st_re_ref, st_im_ref, x_re_ref, x_im_ref, *, tb, seg_len):
    tblk = pl.program_id(1)
    n_seg = st_re_ref.shape[0]

    @pl.when(tblk == 0)
    def _():
        pr = ab_re_ref[...]
        pi = ab_im_ref[...]
        for _ in range(int(math.log2(seg_len))):
            pr, pi = pr * pr - pi * pi, 2.0 * pr * pi
        er = jnp.zeros_like(pr)
        ei = jnp.zeros_like(pi)
        st_re_ref[0:1, :] = er
        st_im_ref[0:1, :] = ei
        for j in range(1, n_seg):
            lr = end_re_ref[0, j - 1:j, :]
            li = end_im_ref[0, j - 1:j, :]
            er, ei = lr + pr * er - pi * ei, li + pr * ei + pi * er
            st_re_ref[j:j + 1, :] = er
            st_im_ref[j:j + 1, :] = ei

    _ssm_fill_lhs(u_ref, lhs_ref, tb=tb)
    drive = functools.partial(_ssm_drive_block, lhs_ref=lhs_ref, bmat_re_ref=bmat_re_ref,
                              bmat_im_ref=bmat_im_ref, bu_re_ref=bu_re_ref, bu_im_ref=bu_im_ref)
    n_slabs = lhs_ref.shape[0]
    kst = bu_re_ref.shape[1] // SSM_BLOCKS
    nt = (((1,), (1,)), ((), ()))
    ys = []
    drive(0)
    for j in range(SSM_BLOCKS):
        if j + 1 < SSM_BLOCKS:
            drive(j + 1)
        _ssm_scan_block(j, ab_re_ref, ab_im_ref, st_re_ref, st_im_ref, bu_re_ref, bu_im_ref, tb=tb,
                        x_re_ref=x_re_ref, x_im_ref=x_im_ref)
        xr = x_re_ref[:, j * kst:(j + 1) * kst]
        xi = x_im_ref[:, j * kst:(j + 1) * kst]
        ys.append(lax.dot_general(xr, cmat_re_ref[j], nt, preferred_element_type=F32)
                  - lax.dot_general(xi, cmat_im_ref[j], nt, preferred_element_type=F32))
    u_rows = jnp.concatenate([lhs_ref[c] for c in range(n_slabs)], axis=-1)
    y = jnp.concatenate(ys, axis=-1) + d_ref[...] * u_rows
    z = jax.nn.gelu(y)
    gate = jnp.dot(z.astype(BF16), wglu_ref[...], preferred_element_type=F32) + bglu_ref[...]
    s = z * jax.nn.sigmoid(gate)
    for c in range(n_slabs):
        lhs_ref[c] = s[:, c * V7X_LANES:(c + 1) * V7X_LANES]
    for seg in range(n_seg):
        for c in range(n_slabs):
            o_ref[0, seg, :, c * V7X_LANES:(c + 1) * V7X_LANES] = lhs_ref[c, pl.ds(seg, tb, stride=n_seg), :]


def _ssm(u_seg, bmat_re, bmat_im, ab_re, ab_im, cmat_re, cmat_im, d_skip, w_glu, b_glu, *, tb):
    batch, n_seg, seg_len, d_ssm = u_seg.shape
    n_cols = ab_re.shape[1]
    rows = tb * n_seg
    grid = (batch, seg_len // tb)
    u_spec = pl.BlockSpec((1, n_seg, tb, d_ssm), lambda b, t: (b, 0, t, 0))
    end_spec = pl.BlockSpec((1, n_seg, n_cols), lambda b, t: (b, 0, 0))

    def whole(a):
        return pl.BlockSpec(a.shape, lambda b, t: (0,) * a.ndim, pipeline_mode=pl.Buffered(1))

    scratch = [pltpu.VMEM((d_ssm // V7X_LANES, rows, V7X_LANES), F32), pltpu.VMEM((rows, n_cols), F32),
               pltpu.VMEM((rows, n_cols), F32), pltpu.VMEM((n_seg, n_cols), F32),
               pltpu.VMEM((n_seg, n_cols), F32)]
    end_re, end_im = pl.pallas_call(
        functools.partial(_ssm_ends_kernel, tb=tb),
        grid=grid,
        in_specs=[u_spec, whole(bmat_re), whole(bmat_im), whole(ab_re), whole(ab_im)],
        out_specs=[end_spec, end_spec],
        out_shape=[jax.ShapeDtypeStruct((batch, n_seg, n_cols), F32)] * 2,
        scratch_shapes=scratch,
        compiler_params=_params(("parallel", "arbitrary")),
        name="ssm_ends",
    )(u_seg, bmat_re, bmat_im, ab_re, ab_im)
    return pl.pallas_call(
        functools.partial(_ssm_main_kernel, tb=tb, seg_len=seg_len),
        grid=grid,
        in_specs=[u_spec, end_spec, end_spec, whole(bmat_re), whole(bmat_im), whole(ab_re), whole(ab_im),
                  whole(cmat_re), whole(cmat_im), whole(d_skip), whole(w_glu), whole(b_glu)],
        out_specs=u_spec,
        out_shape=jax.ShapeDtypeStruct(u_seg.shape, F32),
        scratch_shapes=scratch + [pltpu.VMEM((rows, n_cols), BF16), pltpu.VMEM((rows, n_cols), BF16)],
        compiler_params=_params(("parallel", "arbitrary")),
        name="ssm_main",
    )(u_seg, end_re, end_im, bmat_re, bmat_im, ab_re, ab_im, cmat_re, cmat_im, d_skip, w_glu, b_glu)


def _out_proj_kernel(a_ref, s_ref, wa_ref, ws_ref, xh_ref, xt_ref, g_ref, o_ref, *, n_head):
    m = (jnp.dot(a_ref[...], wa_ref[...], preferred_element_type=F32)
         + jnp.dot(s_ref[...].astype(BF16), ws_ref[...], preferred_element_type=F32))
    o_ref[...] = _row_tile(xh_ref, xt_ref, n_head) + m * _rms_scale(m) * g_ref[...]


def _out_proj(a, s, w, x_parts, g, *, tm):
    t = a.shape[0]
    d = w.shape[1]
    da = a.shape[1]
    ds = s.shape[1]
    assert da == ds and w.shape[0] == da + ds
    x_specs, n_head = _row_tile_specs(x_parts, tm)
    return pl.pallas_call(
        functools.partial(_out_proj_kernel, n_head=n_head),
        grid=(t // tm,),
        in_specs=[
            pl.BlockSpec((tm, da), lambda i: (i, 0)),
            pl.BlockSpec((tm, ds), lambda i: (i, 0)),
            pl.BlockSpec((da, d), lambda i: (0, 0)),
            pl.BlockSpec((ds, d), lambda i: (1, 0)),
        ] + x_specs + [
            pl.BlockSpec((1, d), lambda i: (0, 0)),
        ],
        out_specs=pl.BlockSpec((tm, d), lambda i: (i, 0)),
        out_shape=jax.ShapeDtypeStruct((t, d), F32),
        compiler_params=_params(("parallel",)),
        name="out_proj",
    )(a, s, w, w, *x_parts, g)


def _tiles(d_model, d_ff, d_ssm, n_states):
    tm, tf, tf_first, tq, tb = 512, 1024, 256, 256, 64
    tm_first = 2 * tm
    f32, bf16 = 4, 2
    ffn_bytes = (tm * d_model * (2 * f32 + 2 * f32 + bf16)
                 + 3 * d_model * (2 * tf + d_ff % tf) * bf16)
    first_bytes = (tm_first * d_model * (f32 + f32 + f32 + bf16)
                   + 2 * 3 * d_model * tf_first * (f32 + bf16))
    ssm_bytes = (V7X_SUBLANES * tb) * (d_ssm * f32 * (1 + 2 + 2) + 2 * n_states * (f32 + bf16))
    assert max(ffn_bytes, first_bytes, ssm_bytes) < V7X_VMEM_LIMIT_BYTES
    assert d_ff % tf % V7X_MXU_COLS == 0 and d_ff % tf_first == 0
    return tm, tf, tm_first, tf_first, tq, tb


def kernel(x, ffn1_pre_g, ffn1_w_gate, ffn1_w_up, ffn1_w_down, ffn1_post_g, mix_pre_g, w_in, lambda_q1, lambda_k1, lambda_q2, lambda_k2, subln_g, rel_bias, ssm_a_re, ssm_a_im, ssm_b_re, ssm_b_im, ssm_c_re, ssm_c_im, ssm_d, ssm_log_dt, w_glu, b_glu, w_out, mix_post_g, ffn2_pre_g, ffn2_w_gate, ffn2_w_up, ffn2_w_down, ffn2_post_g):
    batch, seq, d_model = x.shape
    depth = ffn1_pre_g.shape[0]
    assert depth == 1, "LAMBDA_INIT is specialised to a single layer"
    n_heads = rel_bias.shape[1]
    d_attn = n_heads * V_HEAD_DIM
    n_groups, n_state = ssm_a_re.shape[1:]
    d_ssm = n_groups * SSM_GROUP
    n_seg = V7X_SUBLANES
    seg_len = seq // n_seg
    tokens = batch * seq
    tm, tf, tm_first, tf_first, tq, tb = _tiles(d_model, ffn1_w_gate.shape[2], d_ssm,
                                                n_groups * n_state)

    xt = x.reshape(tokens, d_model)
    row = lambda v: v.reshape(1, -1)
    l = 0

    x1_head, wg1, wu1, wd1, w_in_bf, w_glu_bf = _ffn_first(
        xt, row(ffn1_pre_g[l]), ffn1_w_gate[l], ffn1_w_up[l], ffn1_w_down[l], row(ffn1_post_g[l]),
        tm=tm_first, tf=tf_first, cast=(w_in[l], w_glu[l]))
    x1_tail = _ffn(xt, row(ffn1_pre_g[l]), wg1, wu1, wd1, row(ffn1_post_g[l]), tm=tm, tf=tf,
                   rows_done=tm_first)
    x1 = (x1_head, x1_tail)

    qkv, u = _in_proj(x1, row(mix_pre_g[l]), w_in_bf, tm=tm, q_scale=ATTN_HEAD_DIM ** -0.5 * LOG2_E)

    band = _bias_band(rel_bias, tq=tq, scale=LOG2_E)
    a, wg2, wu2, wd2, wo = _attention(
        qkv, band, row(lambda_q1[l]), row(lambda_k1[l]), row(lambda_q2[l]), row(lambda_k2[l]),
        subln_g[l].reshape(-1, 1), batch=batch, seq=seq, n_heads=n_heads, tq=tq,
        cast=(ffn2_w_gate[l], ffn2_w_up[l], ffn2_w_down[l], w_out[l]))

    ab_re, ab_im, bmat_re, bmat_im, cmat_re, cmat_im = _ssm_params(
        ssm_a_re[l].reshape(n_groups, 1, n_state), ssm_a_im[l].reshape(n_groups, 1, n_state),
        ssm_log_dt[l].reshape(n_groups, 1, 1),
        jnp.swapaxes(ssm_b_re[l], 1, 2), jnp.swapaxes(ssm_b_im[l], 1, 2), ssm_c_re[l], ssm_c_im[l])
    u_seg = u.reshape(batch, n_seg, seg_len, d_ssm)
    s_seg = _ssm(u_seg, bmat_re, bmat_im, ab_re.reshape(1, -1), ab_im.reshape(1, -1), cmat_re, cmat_im,
                 row(ssm_d[l]), w_glu_bf, row(b_glu[l]), tb=tb)
    s = s_seg.reshape(tokens, d_ssm)

    x2 = _out_proj(a.reshape(tokens, d_attn), s, wo, x1, row(mix_post_g[l]), tm=tm)

    x3 = _ffn(x2, row(ffn2_pre_g[l]), wg2, wu2, wd2, row(ffn2_post_g[l]), tm=tm, tf=tf)
    return x3.reshape(batch, seq, d_model)
```

```python
import functools
import math

import jax
import jax.numpy as jnp
from jax import lax
from jax.experimental import pallas as pl
from jax.experimental.pallas import tpu as pltpu

V7X_LANES = 128
V7X_SUBLANES = 8
V7X_MXU_COLS = 256
V7X_VMEM_LIMIT_BYTES = 56 * 1024 * 1024

RMS_EPS = 1e-6
NEG_INF = -1e30
N_BUCKETS = 32
MAX_DISTANCE = 128
ATTN_HEAD_DIM = 64
V_HEAD_DIM = 128
SSM_GROUP = 16
LAMBDA_INIT = 0.8 - 0.6 * math.exp(-0.3 * 0)

F32 = jnp.float32
BF16 = jnp.bfloat16
EPILOGUE_ROWS = 128


def _params(semantics):
    return pltpu.CompilerParams(dimension_semantics=semantics,
                                vmem_limit_bytes=V7X_VMEM_LIMIT_BYTES)


def _rms_scale(v):
    return lax.rsqrt(jnp.mean(v * v, axis=-1, keepdims=True) + RMS_EPS)


def _ffn_prologue(x_ref, pre_g_ref, xn_ref, acc_ref):
    xf = x_ref[...]
    xn_ref[...] = (xf * _rms_scale(xf) * pre_g_ref[...]).astype(BF16)
    acc_ref[...] = jnp.zeros_like(acc_ref)


def _ffn_chunk(xn_ref, acc_ref, wg_ref, wu_ref, wd_ref):
    xn = xn_ref[...]
    n_split = max(1, wg_ref.shape[1] // V7X_MXU_COLS)
    half = wg_ref.shape[1] // n_split
    pre = []
    for c in range(n_split):
        cols = slice(c * half, (c + 1) * half)
        pre.append((jnp.dot(xn, wg_ref[:, cols], preferred_element_type=F32),
                    jnp.dot(xn, wu_ref[:, cols], preferred_element_type=F32)))
    out = acc_ref[...]
    for c, (gate, up) in enumerate(pre):
        hidden = (gate * jax.nn.sigmoid(gate) * up).astype(BF16)
        out = out + jnp.dot(hidden, wd_ref[c * half:(c + 1) * half, :], preferred_element_type=F32)
    acc_ref[...] = out


def _ffn_epilogue(x_ref, acc_ref, post_g_ref, o_ref):
    scale = _rms_scale(acc_ref[...])
    g_half = 0.5 * post_g_ref[...]
    for r in range(0, acc_ref.shape[0], EPILOGUE_ROWS):
        rows = slice(r, r + EPILOGUE_ROWS)
        o_ref[rows, :] = x_ref[rows, :] + acc_ref[rows, :] * scale[rows] * g_half


def _ffn_kernel(x_ref, pre_g_ref, wg_ref, wu_ref, wd_ref, *rest):
    tail_refs = rest[:-3]
    post_g_ref, o_ref, xn_ref = rest[-3:]
    k = pl.program_id(1)
    last = pl.num_programs(1) - 1

    @pl.when(k == 0)
    def _():
        _ffn_prologue(x_ref, pre_g_ref, xn_ref, o_ref)
        if tail_refs:
            _ffn_chunk(xn_ref, o_ref, *tail_refs)
        _ffn_chunk(xn_ref, o_ref, wg_ref, wu_ref, wd_ref)

    @pl.when(jnp.logical_and(k > 0, k < last))
    def _():
        _ffn_chunk(xn_ref, o_ref, wg_ref, wu_ref, wd_ref)

    @pl.when(k == last)
    def _():
        _ffn_chunk(xn_ref, o_ref, wg_ref, wu_ref, wd_ref)
        _ffn_epilogue(x_ref, o_ref, post_g_ref, o_ref)


def _ffn_first_kernel(x_ref, pre_g_ref, wg32_ref, wu32_ref, wd32_ref, post_g_ref, *rest, cast_steps):
    n_cast = len(cast_steps)
    cast_in = rest[:n_cast]
    o_ref, wg_ref, wu_ref, wd_ref = rest[n_cast:n_cast + 4]
    cast_out = rest[n_cast + 4:2 * n_cast + 4]
    xn_ref, acc_ref = rest[2 * n_cast + 4:]
    k = pl.program_id(0)

    @pl.when(k == 0)
    def _():
        _ffn_prologue(x_ref, pre_g_ref, xn_ref, acc_ref)

    for src, dst, steps in zip(cast_in, cast_out, cast_steps):
        @pl.when(k < steps)
        def _(src=src, dst=dst):
            dst[...] = src[...].astype(BF16)

    wg = wg32_ref[...].astype(BF16)
    wu = wu32_ref[...].astype(BF16)
    wd = wd32_ref[...].astype(BF16)
    wg_ref[...] = wg
    wu_ref[...] = wu
    wd_ref[...] = wd
    _ffn_chunk(xn_ref, acc_ref, wg, wu, wd)

    @pl.when(k == pl.num_programs(0) - 1)
    def _():
        _ffn_epilogue(x_ref, acc_ref, post_g_ref, o_ref)


def _ffn_first(x, pre_g, wg32, wu32, wd32, post_g, *, tm, tf, cast=()):
    t, d = x.shape
    f = wg32.shape[1]
    n_chunks = f // tf
    cast_steps = []
    for w in cast:
        steps = 1
        while steps * 2 <= n_chunks and w.shape[0] % (16 * steps * 2) == 0:
            steps *= 2
        cast_steps.append(steps)
    cast_specs = [pl.BlockSpec((w.shape[0] // steps, w.shape[1]),
                               lambda k, steps=steps: (jnp.minimum(k, steps - 1), 0))
                  for w, steps in zip(cast, cast_steps)]
    w_specs = [pl.BlockSpec((d, tf), lambda k: (0, k)), pl.BlockSpec((d, tf), lambda k: (0, k)),
               pl.BlockSpec((tf, d), lambda k: (k, 0))]
    return pl.pallas_call(
        functools.partial(_ffn_first_kernel, cast_steps=tuple(cast_steps)),
        grid=(n_chunks,),
        in_specs=[pl.BlockSpec((tm, d), lambda k: (0, 0), pipeline_mode=pl.Buffered(1)),
                  pl.BlockSpec((1, d), lambda k: (0, 0))]
                 + w_specs + [pl.BlockSpec((1, d), lambda k: (0, 0))] + cast_specs,
        out_specs=[pl.BlockSpec((tm, d), lambda k: (0, 0), pipeline_mode=pl.Buffered(1))] + w_specs + cast_specs,
        out_shape=[jax.ShapeDtypeStruct((tm, d), F32)]
                  + [jax.ShapeDtypeStruct(w.shape, BF16) for w in (wg32, wu32, wd32) + tuple(cast)],
        scratch_shapes=[pltpu.VMEM((tm, d), BF16), pltpu.VMEM((tm, d), F32)],
        compiler_params=_params(("arbitrary",)),
        name="ffn_first",
    )(x, pre_g, wg32, wu32, wd32, post_g, *cast)


def _ffn(x, pre_g, wg, wu, wd, post_g, *, tm, tf, rows_done=0):
    t, d = x.shape
    f = wg.shape[1]
    assert rows_done % tm == 0 and f // tf >= 2
    skip = rows_done // tm
    tail = f % tf
    tail_specs, tail_args = [], []
    if tail:
        assert f % tail == 0
        last = f // tail - 1
        once = pl.Buffered(1)
        tail_specs = [pl.BlockSpec((d, tail), lambda i, k: (0, last), pipeline_mode=once),
                      pl.BlockSpec((d, tail), lambda i, k: (0, last), pipeline_mode=once),
                      pl.BlockSpec((tail, d), lambda i, k: (last, 0), pipeline_mode=once)]
        tail_args = [wg, wu, wd]
    return pl.pallas_call(
        _ffn_kernel,
        grid=(t // tm - skip, f // tf),
        in_specs=[
            pl.BlockSpec((tm, d), lambda i, k: (i + skip, 0)),
            pl.BlockSpec((1, d), lambda i, k: (0, 0)),
            pl.BlockSpec((d, tf), lambda i, k: (0, k)),
            pl.BlockSpec((d, tf), lambda i, k: (0, k)),
            pl.BlockSpec((tf, d), lambda i, k: (k, 0)),
        ] + tail_specs + [pl.BlockSpec((1, d), lambda i, k: (0, 0))],
        out_specs=pl.BlockSpec((tm, d), lambda i, k: (i, 0)),
        out_shape=jax.ShapeDtypeStruct((t - rows_done, d), F32),
        scratch_shapes=[pltpu.VMEM((tm, d), BF16)],
        compiler_params=_params(("parallel", "arbitrary")),
        name="ffn",
    )(x, pre_g, wg, wu, wd, *tail_args, post_g)


def _row_tile_specs(parts, tm):
    head, tail = parts
    n_head = head.shape[0] // tm
    assert head.shape[0] % tm == 0 and tail.shape[0] % tm == 0 and head.shape[1] == tail.shape[1]
    d = head.shape[1]
    specs = [pl.BlockSpec((tm, d), lambda i: (jnp.minimum(i, n_head - 1), 0)),
             pl.BlockSpec((tm, d), lambda i: (jnp.maximum(i - n_head, 0), 0))]
    return specs, n_head


def _row_tile(head_ref, tail_ref, n_head):
    return jnp.where(pl.program_id(0) < n_head, head_ref[...], tail_ref[...])


def _in_proj_kernel(xh_ref, xt_ref, g_ref, w_ref, qkv_ref, u_ref, *, q_scale, n_head):
    xf = _row_tile(xh_ref, xt_ref, n_head)
    hn = (xf * _rms_scale(xf) * g_ref[...]).astype(BF16)
    tn = u_ref.shape[1]
    for n in range(4):
        acc = jnp.dot(hn, w_ref[:, n * tn:(n + 1) * tn], preferred_element_type=F32)
        if n == 0:
            qkv_ref[:, :tn] = (acc * q_scale).astype(BF16)
        elif n < 3:
            qkv_ref[:, n * tn:(n + 1) * tn] = acc.astype(BF16)
        else:
            u_ref[...] = acc


def _in_proj(x_parts, g, w, *, tm, q_scale):
    t = x_parts[0].shape[0] + x_parts[1].shape[0]
    d = x_parts[0].shape[1]
    tn = w.shape[1] // 4
    x_specs, n_head = _row_tile_specs(x_parts, tm)
    return pl.pallas_call(
        functools.partial(_in_proj_kernel, q_scale=q_scale, n_head=n_head),
        grid=(t // tm,),
        in_specs=x_specs + [
            pl.BlockSpec((1, d), lambda i: (0, 0)),
            pl.BlockSpec(w.shape, lambda i: (0, 0), pipeline_mode=pl.Buffered(1)),
        ],
        out_specs=[
            pl.BlockSpec((tm, 3 * tn), lambda i: (i, 0)),
            pl.BlockSpec((tm, tn), lambda i: (i, 0)),
        ],
        out_shape=[jax.ShapeDtypeStruct((t, 3 * tn), BF16),
                   jax.ShapeDtypeStruct((t, tn), F32)],
        compiler_params=_params(("parallel",)),
        name="in_proj",
    )(*x_parts, g, w)


def _bias_band_kernel(rb_ref, o_ref, *, tq, scale):
    h = pl.program_id(0)
    key = lax.broadcasted_iota(jnp.int32, (2 * tq, tq), 0)
    qry = lax.broadcasted_iota(jnp.int32, (2 * tq, tq), 1)
    dist = tq + qry - key
    n = jnp.maximum(dist, 0)
    max_exact = N_BUCKETS // 2
    nf = jnp.maximum(n, 1).astype(F32)
    large = max_exact + (jnp.log(nf / max_exact) / math.log(MAX_DISTANCE / max_exact)
                         * (N_BUCKETS - max_exact)).astype(jnp.int32)
    large = jnp.minimum(large, N_BUCKETS - 1)
    bucket = jnp.where(n < max_exact, n, large)
    val = jnp.zeros((2 * tq, tq), F32)
    for b in range(N_BUCKETS):
        val = jnp.where(bucket == b, rb_ref[b, h], val)
    o_ref[0] = jnp.where(dist >= 0, (val - rb_ref[N_BUCKETS - 1, h]) * scale, NEG_INF)


def _bias_band(rel_bias, *, tq, scale):
    assert tq >= MAX_DISTANCE
    n_heads = rel_bias.shape[1]
    return pl.pallas_call(
        functools.partial(_bias_band_kernel, tq=tq, scale=scale),
        grid=(n_heads,),
        in_specs=[pl.BlockSpec(memory_space=pltpu.SMEM)],
        out_specs=pl.BlockSpec((1, 2 * tq, tq), lambda h: (h, 0, 0)),
        out_shape=jax.ShapeDtypeStruct((n_heads, 2 * tq, tq), F32),
        compiler_params=_params(("arbitrary",)),
        name="bias_band",
    )(rel_bias)


VT_ROWS = V_HEAD_DIM + 16
LOG2_E = math.log2(math.e)


def _attn_kernel(lq1_ref, lk1_ref, lq2_ref, lk2_ref, q1_ref, q2_ref, k1_ref, k2_ref,
                 v_ref, band_ref, g_ref, *rest, tq, cast_rows):
    n_cast = len(cast_rows)
    cast_in = rest[:n_cast]
    o_ref = rest[n_cast]
    cast_out = rest[n_cast + 1:2 * n_cast + 1]
    vt_ref, qm_ref, s_ref, m_ref, acc_ref = rest[2 * n_cast + 1:]
    n_blocks = v_ref.shape[1] // tq
    q_refs = (q1_ref, q2_ref)
    k_refs = (k1_ref, k2_ref)

    ones_pad = (lax.broadcasted_iota(jnp.int32, (VT_ROWS - V_HEAD_DIM, tq), 0) == 0).astype(BF16)
    feat = lax.broadcasted_iota(jnp.int32, (2 * ATTN_HEAD_DIM, tq), 0)
    for c in range(n_blocks):
        rows = slice(c * tq, (c + 1) * tq)
        vt = v_ref[0, rows, :].astype(F32).T.astype(BF16)
        for hh in range(2):
            vt_ref[c, hh, :V_HEAD_DIM, :] = vt[hh * V_HEAD_DIM:(hh + 1) * V_HEAD_DIM]
            vt_ref[c, hh, V_HEAD_DIM:, :] = ones_pad
        for mi in range(2):
            qt = q_refs[mi][0, rows, :].astype(F32).T
            for hh in range(2):
                qm_ref[c, mi, :, hh * tq:(hh + 1) * tq] = jnp.where(
                    (feat >= ATTN_HEAD_DIM) == (hh == 1), qt, 0.0).astype(BF16)

    lam = (jnp.exp(jnp.sum(lq1_ref[...] * lk1_ref[...], axis=-1, keepdims=True))
           - jnp.exp(jnp.sum(lq2_ref[...] * lk2_ref[...], axis=-1, keepdims=True)) + LAMBDA_INIT)

    def score_phase(i, j, buf):
        rows = slice(j * tq, (j + 1) * tq)
        band_rows = slice(tq, 2 * tq) if j == i else slice(0, tq)
        for mi in range(2):
            s = jnp.dot(k_refs[mi][0, rows, :], qm_ref[i, mi], preferred_element_type=F32)
            if j >= i - 1:
                s = s + jnp.concatenate([band_ref[0, band_rows, :], band_ref[1, band_rows, :]], axis=1)
            s_ref[buf, mi] = s

    def value_phase(j, buf):
        alphas, ps = [], []
        for mi in range(2):
            s = s_ref[buf, mi]
            m_new = jnp.max(s, axis=0, keepdims=True)
            if j > 0:
                m_old = m_ref[mi]
                m_new = jnp.maximum(m_old, m_new)
                alphas.append(jnp.exp2(m_old - m_new))
            ps.append(jnp.exp2(s - m_new).astype(BF16))
            m_ref[mi] = m_new
        for hh in range(2):
            head = slice(hh * tq, (hh + 1) * tq)
            p2 = jnp.concatenate([ps[0][:, head], ps[1][:, head]], axis=1)
            pv = jnp.dot(vt_ref[j, hh], p2, preferred_element_type=F32)
            if j > 0:
                alpha2 = jnp.concatenate([alphas[0][:, head], alphas[1][:, head]], axis=1)
                pv = alpha2 * acc_ref[1 - buf, hh] + pv
            acc_ref[buf, hh] = pv

    def finish_block(i, buf):
        out_rows = slice(i * tq, (i + 1) * tq)
        den = V_HEAD_DIM
        for hh in range(2):
            a = (acc_ref[buf, hh, :den, :tq] / acc_ref[buf, hh, den:den + 1, :tq]
                 - lam * (acc_ref[buf, hh, :den, tq:] / acc_ref[buf, hh, den:den + 1, tq:]))
            scale = lax.rsqrt(jnp.mean(a * a, axis=0, keepdims=True) + RMS_EPS)
            a = a * scale * g_ref[...] * (1.0 - LAMBDA_INIT)
            o_ref[0, out_rows, hh * V_HEAD_DIM:(hh + 1) * V_HEAD_DIM] = a.T.astype(o_ref.dtype)

    pairs = [(i, j) for i in range(n_blocks) for j in range(i + 1)]
    score_phase(0, 0, 0)
    for n, (i, j) in enumerate(pairs):
        for src, dst, rows_per_it in zip(cast_in, cast_out, cast_rows):
            if n < src.shape[0] // rows_per_it:
                rows = slice(n * rows_per_it, (n + 1) * rows_per_it)
                dst[rows, :] = src[rows, :].astype(BF16)
        if n + 1 < len(pairs):
            score_phase(*pairs[n + 1], (n + 1) % 2)
        value_phase(j, n % 2)
        if j == i:
            finish_block(i, n % 2)


def _attention(qkv, band, lq1, lk1, lq2, lk2, subln_g, *, batch, seq, n_heads, tq, cast=()):
    d_attn = n_heads * V_HEAD_DIM
    pair = 2 * ATTN_HEAD_DIM
    vpair = 2 * V_HEAD_DIM
    n_pairs = d_attn // 2 // pair
    n_blocks = seq // tq
    n_steps = batch * n_pairs
    n_iters = n_blocks * (n_blocks + 1) // 2
    cast_rows = []
    for w in cast:
        block_rows = w.shape[0] // n_steps
        rows_per_it = 16
        while block_rows % rows_per_it or block_rows // rows_per_it > n_iters:
            rows_per_it += 16
        assert w.shape[0] % n_steps == 0 and rows_per_it <= block_rows
        cast_rows.append(rows_per_it)
    cast_specs = [pl.BlockSpec((w.shape[0] // n_steps, w.shape[1]), lambda b, p: (b * n_pairs + p, 0))
                  for w in cast]
    qkv3 = qkv.reshape(batch, seq, 3 * d_attn)
    small = pl.BlockSpec((1, ATTN_HEAD_DIM), lambda b, p: (0, 0))
    return pl.pallas_call(
        functools.partial(_attn_kernel, tq=tq, cast_rows=tuple(cast_rows)),
        grid=(batch, n_pairs),
        in_specs=[
            small, small, small, small,
            pl.BlockSpec((1, seq, pair), lambda b, p: (b, 0, p)),
            pl.BlockSpec((1, seq, pair), lambda b, p: (b, 0, n_pairs + p)),
            pl.BlockSpec((1, seq, pair), lambda b, p: (b, 0, 2 * n_pairs + p)),
            pl.BlockSpec((1, seq, pair), lambda b, p: (b, 0, 3 * n_pairs + p)),
            pl.BlockSpec((1, seq, vpair), lambda b, p: (b, 0, 2 * n_pairs + p)),
            pl.BlockSpec((2, 2 * tq, tq), lambda b, p: (p, 0, 0)),
            pl.BlockSpec((V_HEAD_DIM, 1), lambda b, p: (0, 0)),
        ] + cast_specs,
        out_specs=[pl.BlockSpec((1, seq, vpair), lambda b, p: (b, 0, p))] + cast_specs,
        out_shape=[jax.ShapeDtypeStruct((batch, seq, d_attn), BF16)]
                  + [jax.ShapeDtypeStruct(w.shape, BF16) for w in cast],
        scratch_shapes=[pltpu.VMEM((n_blocks, 2, VT_ROWS, tq), BF16),
                        pltpu.VMEM((n_blocks, 2, pair, 2 * tq), BF16),
                        pltpu.VMEM((2, 2, tq, 2 * tq), F32),
                        pltpu.VMEM((2, 1, 2 * tq), F32),
                        pltpu.VMEM((2, 2, VT_ROWS, 2 * tq), F32)],
        compiler_params=_params(("parallel", "parallel")),
        name="diff_attention",
    )(lq1, lk1, lq2, lk2, qkv3, qkv3, qkv3, qkv3, qkv3, band, subln_g, *cast)


def _block_diag_blocks(w, dst_ref):
    n_blocks, rows, cols = dst_ref.shape
    g, p, n = w.shape
    gl = g // n_blocks
    row_group = lax.broadcasted_iota(jnp.int32, (rows, cols), 0) // p
    col_group = lax.broadcasted_iota(jnp.int32, (rows, cols), 1) // n
    for j in range(n_blocks):
        stacked = w[j * gl:(j + 1) * gl].reshape(rows, n)
        tiled = jnp.concatenate([stacked] * gl, axis=1)
        dst_ref[j] = jnp.where(row_group == col_group, tiled, 0.0).astype(BF16)


def _ssm_params_kernel(a_re_ref, a_im_ref, log_dt_ref, b_re_ref, b_im_ref, c_re_ref, c_im_ref,
                       ab_re_ref, ab_im_ref, bd_re_ref, bd_im_ref, cd_re_ref, cd_im_ref):
    ar = a_re_ref[...]
    ai = a_im_ref[...]
    dt = jnp.exp(log_dt_ref[...])
    decay = jnp.exp(dt * ar)
    ab_re = decay * jnp.cos(dt * ai)
    ab_im = decay * jnp.sin(dt * ai)
    den = ar * ar + ai * ai
    nr = ab_re - 1.0
    ni = ab_im
    coef_re = (nr * ar + ni * ai) / den
    coef_im = (ni * ar - nr * ai) / den
    br = b_re_ref[...]
    bi = b_im_ref[...]
    ab_re_ref[...] = ab_re
    ab_im_ref[...] = ab_im
    _block_diag_blocks(coef_re * br - coef_im * bi, bd_re_ref)
    _block_diag_blocks(coef_re * bi + coef_im * br, bd_im_ref)
    _block_diag_blocks(c_re_ref[...], cd_re_ref)
    _block_diag_blocks(c_im_ref[...], cd_im_ref)


def _ssm_params(a_re, a_im, log_dt, b_re_t, b_im_t, c_re, c_im):
    g, _, n = a_re.shape
    p = b_re_t.shape[1]
    gl = g // SSM_BLOCKS
    dense = jax.ShapeDtypeStruct((SSM_BLOCKS, gl * p, gl * n), BF16)
    return pl.pallas_call(
        _ssm_params_kernel,
        out_shape=[jax.ShapeDtypeStruct((g, 1, n), F32), jax.ShapeDtypeStruct((g, 1, n), F32),
                   dense, dense, dense, dense],
        compiler_params=pltpu.CompilerParams(vmem_limit_bytes=V7X_VMEM_LIMIT_BYTES),
        name="ssm_params",
    )(a_re, a_im, log_dt, b_re_t, b_im_t, c_re, c_im)


SSM_COL_CHUNK = 512
SSM_BLOCKS = 4


def _ssm_fill_lhs(u_ref, lhs_ref, *, tb):
    n_seg = u_ref.shape[1]
    for seg in range(n_seg):
        for c in range(lhs_ref.shape[0]):
            lhs_ref[c, pl.ds(seg, tb, stride=n_seg), :] = u_ref[0, seg, :, c * V7X_LANES:(c + 1) * V7X_LANES]


def _ssm_drive_block(j, lhs_ref, bmat_re_ref, bmat_im_ref, bu_re_ref, bu_im_ref):
    per_block = lhs_ref.shape[0] // SSM_BLOCKS
    kst = bu_re_ref.shape[1] // SSM_BLOCKS
    uj = jnp.concatenate([lhs_ref[c] for c in range(j * per_block, (j + 1) * per_block)],
                         axis=-1).astype(BF16)
    bu_re_ref[:, j * kst:(j + 1) * kst] = jnp.dot(uj, bmat_re_ref[j], preferred_element_type=F32)
    bu_im_ref[:, j * kst:(j + 1) * kst] = jnp.dot(uj, bmat_im_ref[j], preferred_element_type=F32)


def _ssm_scan_block(j, ab_re_ref, ab_im_ref, st_re_ref, st_im_ref, bu_re_ref, bu_im_ref, *, tb,
                    x_re_ref=None, x_im_ref=None):
    kst = bu_re_ref.shape[1] // SSM_BLOCKS
    two = 2 * V7X_SUBLANES
    chunks = [slice(j * kst + c * SSM_COL_CHUNK, j * kst + (c + 1) * SSM_COL_CHUNK)
              for c in range(kst // SSM_COL_CHUNK)]
    coef = [(jnp.broadcast_to(ab_re_ref[:, cs], (V7X_SUBLANES, SSM_COL_CHUNK)),
             jnp.broadcast_to(ab_im_ref[:, cs], (V7X_SUBLANES, SSM_COL_CHUNK))) for cs in chunks]
    state = [(st_re_ref[:, cs], st_im_ref[:, cs]) for cs in chunks]
    for t2 in range(tb // 2):
        rows = slice(t2 * two, (t2 + 1) * two)
        for n, cs in enumerate(chunks):
            ar, ai = coef[n]
            xr, xi = state[n]
            bur = bu_re_ref[rows, cs]
            bui = bu_im_ref[rows, cs]
            xr1 = ar * xr - ai * xi + bur[:V7X_SUBLANES]
            xi1 = ar * xi + ai * xr + bui[:V7X_SUBLANES]
            xr2 = ar * xr1 - ai * xi1 + bur[V7X_SUBLANES:]
            xi2 = ar * xi1 + ai * xr1 + bui[V7X_SUBLANES:]
            if x_re_ref is not None:
                x_re_ref[rows, cs] = jnp.concatenate([xr1, xr2], axis=0).astype(BF16)
                x_im_ref[rows, cs] = jnp.concatenate([xi1, xi2], axis=0).astype(BF16)
            state[n] = (xr2, xi2)
    for (xr, xi), cs in zip(state, chunks):
        st_re_ref[:, cs] = xr
        st_im_ref[:, cs] = xi


def _ssm_ends_kernel(u_ref, bmat_re_ref, bmat_im_ref, ab_re_ref, ab_im_ref, end_re_ref, end_im_ref,
                     lhs_ref, bu_re_ref, bu_im_ref, st_re_ref, st_im_ref, *, tb):
    tblk = pl.program_id(1)

    @pl.when(tblk == 0)
    def _():
        st_re_ref[...] = jnp.zeros_like(st_re_ref)
        st_im_ref[...] = jnp.zeros_like(st_im_ref)

    _ssm_fill_lhs(u_ref, lhs_ref, tb=tb)
    drive = functools.partial(_ssm_drive_block, lhs_ref=lhs_ref, bmat_re_ref=bmat_re_ref,
                              bmat_im_ref=bmat_im_ref, bu_re_ref=bu_re_ref, bu_im_ref=bu_im_ref)
    drive(0)
    for j in range(SSM_BLOCKS):
        if j + 1 < SSM_BLOCKS:
            drive(j + 1)
        _ssm_scan_block(j, ab_re_ref, ab_im_ref, st_re_ref, st_im_ref, bu_re_ref, bu_im_ref, tb=tb)

    @pl.when(tblk == pl.num_programs(1) - 1)
    def _():
        end_re_ref[0] = st_re_ref[...]
        end_im_ref[0] = st_im_ref[...]


def _ssm_main_kernel(u_ref, end_re_ref, end_im_ref, bmat_re_ref, bmat_im_ref, ab_re_ref, ab_im_ref,
                     cmat_re_ref, cmat_im_ref, d_ref, wglu_ref, bglu_ref, o_ref,
                     lhs_ref, bu_re_ref, bu_im_ref, st_re_ref, st_im_ref, x_re_ref, x_im_ref, *, tb, seg_len):
    tblk = pl.program_id(1)
    n_seg = st_re_ref.shape[0]

    @pl.when(tblk == 0)
    def _():
        pr = ab_re_ref[...]
        pi = ab_im_ref[...]
        for _ in range(int(math.log2(seg_len))):
            pr, pi = pr * pr - pi * pi, 2.0 * pr * pi
        er = jnp.zeros_like(pr)
        ei = jnp.zeros_like(pi)
        st_re_ref[0:1, :] = er
        st_im_ref[0:1, :] = ei
        for j in range(1, n_seg):
            lr = end_re_ref[0, j - 1:j, :]
            li = end_im_ref[0, j - 1:j, :]
            er, ei = lr + pr * er - pi * ei, li + pr * ei + pi * er
            st_re_ref[j:j + 1, :] = er
            st_im_ref[j:j + 1, :] = ei

    _ssm_fill_lhs(u_ref, lhs_ref, tb=tb)
    drive = functools.partial(_ssm_drive_block, lhs_ref=lhs_ref, bmat_re_ref=bmat_re_ref,
                              bmat_im_ref=bmat_im_ref, bu_re_ref=bu_re_ref, bu_im_ref=bu_im_ref)
    n_slabs = lhs_ref.shape[0]
    kst = bu_re_ref.shape[1] // SSM_BLOCKS
    nt = (((1,), (1,)), ((), ()))
    ys = []
    drive(0)
    for j in range(SSM_BLOCKS):
        if j + 1 < SSM_BLOCKS:
            drive(j + 1)
        _ssm_scan_block(j, ab_re_ref, ab_im_ref, st_re_ref, st_im_ref, bu_re_ref, bu_im_ref, tb=tb,
                        x_re_ref=x_re_ref, x_im_ref=x_im_ref)
        xr = x_re_ref[:, j * kst:(j + 1) * kst]
        xi = x_im_ref[:, j * kst:(j + 1) * kst]
        ys.append(lax.dot_general(xr, cmat_re_ref[j], nt, preferred_element_type=F32)
                  - lax.dot_general(xi, cmat_im_ref[j], nt, preferred_element_type=F32))
    u_rows = jnp.concatenate([lhs_ref[c] for c in range(n_slabs)], axis=-1)
    y = jnp.concatenate(ys, axis=-1) + d_ref[...] * u_rows
    z = jax.nn.gelu(y)
    gate = jnp.dot(z.astype(BF16), wglu_ref[...], preferred_element_type=F32) + bglu_ref[...]
    s = z * jax.nn.sigmoid(gate)
    for c in range(n_slabs):
        lhs_ref[c] = s[:, c * V7X_LANES:(c + 1) * V7X_LANES]
    for seg in range(n_seg):
        for c in range(n_slabs):
            o_ref[0, seg, :, c * V7X_LANES:(c + 1) * V7X_LANES] = lhs_ref[c, pl.ds(seg, tb, stride=n_seg), :]


def _ssm(u_seg, bmat_re, bmat_im, ab_re, ab_im, cmat_re, cmat_im, d_skip, w_glu, b_glu, *, tb):
    batch, n_seg, seg_len, d_ssm = u_seg.shape
    n_cols = ab_re.shape[1]
    rows = tb * n_seg
    grid = (batch, seg_len // tb)
    u_spec = pl.BlockSpec((1, n_seg, tb, d_ssm), lambda b, t: (b, 0, t, 0))
    end_spec = pl.BlockSpec((1, n_seg, n_cols), lambda b, t: (b, 0, 0))

    def whole(a):
        return pl.BlockSpec(a.shape, lambda b, t: (0,) * a.ndim, pipeline_mode=pl.Buffered(1))

    scratch = [pltpu.VMEM((d_ssm // V7X_LANES, rows, V7X_LANES), F32), pltpu.VMEM((rows, n_cols), F32),
               pltpu.VMEM((rows, n_cols), F32), pltpu.VMEM((n_seg, n_cols), F32),
               pltpu.VMEM((n_seg, n_cols), F32)]
    end_re, end_im = pl.pallas_call(
        functools.partial(_ssm_ends_kernel, tb=tb),
        grid=grid,
        in_specs=[u_spec, whole(bmat_re), whole(bmat_im), whole(ab_re), whole(ab_im)],
        out_specs=[end_spec, end_spec],
        out_shape=[jax.ShapeDtypeStruct((batch, n_seg, n_cols), F32)] * 2,
        scratch_shapes=scratch,
        compiler_params=_params(("parallel", "arbitrary")),
        name="ssm_ends",
    )(u_seg, bmat_re, bmat_im, ab_re, ab_im)
    return pl.pallas_call(
        functools.partial(_ssm_main_kernel, tb=tb, seg_len=seg_len),
        grid=grid,
        in_specs=[u_spec, end_spec, end_spec, whole(bmat_re), whole(bmat_im), whole(ab_re), whole(ab_im),
                  whole(cmat_re), whole(cmat_im), whole(d_skip), whole(w_glu), whole(b_glu)],
        out_specs=u_spec,
        out_shape=jax.ShapeDtypeStruct(u_seg.shape, F32),
        scratch_shapes=scratch + [pltpu.VMEM((rows, n_cols), BF16), pltpu.VMEM((rows, n_cols), BF16)],
        compiler_params=_params(("parallel", "arbitrary")),
        name="ssm_main",
    )(u_seg, end_re, end_im, bmat_re, bmat_im, ab_re, ab_im, cmat_re, cmat_im, d_skip, w_glu, b_glu)


def _out_proj_kernel(a_ref, s_ref, wa_ref, ws_ref, xh_ref, xt_ref, g_ref, o_ref, *, n_head):
    m = (jnp.dot(a_ref[...], wa_ref[...], preferred_element_type=F32)
         + jnp.dot(s_ref[...].astype(BF16), ws_ref[...], preferred_element_type=F32))
    o_ref[...] = _row_tile(xh_ref, xt_ref, n_head) + m * _rms_scale(m) * g_ref[...]


def _out_proj(a, s, w, x_parts, g, *, tm):
    t = a.shape[0]
    d = w.shape[1]
    da = a.shape[1]
    ds = s.shape[1]
    assert da == ds and w.shape[0] == da + ds
    x_specs, n_head = _row_tile_specs(x_parts, tm)
    return pl.pallas_call(
        functools.partial(_out_proj_kernel, n_head=n_head),
        grid=(t // tm,),
        in_specs=[
            pl.BlockSpec((tm, da), lambda i: (i, 0)),
            pl.BlockSpec((tm, ds), lambda i: (i, 0)),
            pl.BlockSpec((da, d), lambda i: (0, 0)),
            pl.BlockSpec((ds, d), lambda i: (1, 0)),
        ] + x_specs + [
            pl.BlockSpec((1, d), lambda i: (0, 0)),
        ],
        out_specs=pl.BlockSpec((tm, d), lambda i: (i, 0)),
        out_shape=jax.ShapeDtypeStruct((t, d), F32),
        compiler_params=_params(("parallel",)),
        name="out_proj",
    )(a, s, w, w, *x_parts, g)


def _tiles(d_model, d_ff, d_ssm, n_states):
    tm, tf, tf_first, tq, tb = 512, 1024, 256, 256, 64
    tm_first = 2 * tm
    f32, bf16 = 4, 2
    ffn_bytes = (tm * d_model * (2 * f32 + 2 * f32 + bf16)
                 + 3 * d_model * (2 * tf + d_ff % tf) * bf16)
    first_bytes = (tm_first * d_model * (f32 + f32 + f32 + bf16)
                   + 2 * 3 * d_model * tf_first * (f32 + bf16))
    ssm_bytes = (V7X_SUBLANES * tb) * (d_ssm * f32 * (1 + 2 + 2) + 2 * n_states * (f32 + bf16))
    assert max(ffn_bytes, first_bytes, ssm_bytes) < V7X_VMEM_LIMIT_BYTES
    assert d_ff % tf % V7X_MXU_COLS == 0 and d_ff % tf_first == 0
    return tm, tf, tm_first, tf_first, tq, tb


def kernel(x, ffn1_pre_g, ffn1_w_gate, ffn1_w_up, ffn1_w_down, ffn1_post_g, mix_pre_g, w_in, lambda_q1, lambda_k1, lambda_q2, lambda_k2, subln_g, rel_bias, ssm_a_re, ssm_a_im, ssm_b_re, ssm_b_im, ssm_c_re, ssm_c_im, ssm_d, ssm_log_dt, w_glu, b_glu, w_out, mix_post_g, ffn2_pre_g, ffn2_w_gate, ffn2_w_up, ffn2_w_down, ffn2_post_g):
    batch, seq, d_model = x.shape
    depth = ffn1_pre_g.shape[0]
    assert depth == 1, "LAMBDA_INIT is specialised to a single layer"
    n_heads = rel_bias.shape[1]
    d_attn = n_heads * V_HEAD_DIM
    n_groups, n_state = ssm_a_re.shape[1:]
    d_ssm = n_groups * SSM_GROUP
    n_seg = V7X_SUBLANES
    seg_len = seq // n_seg
    tokens = batch * seq
    tm, tf, tm_first, tf_first, tq, tb = _tiles(d_model, ffn1_w_gate.shape[2], d_ssm,
                                                n_groups * n_state)

    xt = x.reshape(tokens, d_model)
    row = lambda v: v.reshape(1, -1)
    l = 0

    x1_head, wg1, wu1, wd1, w_in_bf, w_glu_bf = _ffn_first(
        xt, row(ffn1_pre_g[l]), ffn1_w_gate[l], ffn1_w_up[l], ffn1_w_down[l], row(ffn1_post_g[l]),
        tm=tm_first, tf=tf_first, cast=(w_in[l], w_glu[l]))
    x1_tail = _ffn(xt, row(ffn1_pre_g[l]), wg1, wu1, wd1, row(ffn1_post_g[l]), tm=tm, tf=tf,
                   rows_done=tm_first)
    x1 = (x1_head, x1_tail)

    qkv, u = _in_proj(x1, row(mix_pre_g[l]), w_in_bf, tm=tm, q_scale=ATTN_HEAD_DIM ** -0.5 * LOG2_E)

    band = _bias_band(rel_bias, tq=tq, scale=LOG2_E)
    a, wg2, wu2, wd2, wo = _attention(
        qkv, band, row(lambda_q1[l]), row(lambda_k1[l]), row(lambda_q2[l]), row(lambda_k2[l]),
        subln_g[l].reshape(-1, 1), batch=batch, seq=seq, n_heads=n_heads, tq=tq,
        cast=(ffn2_w_gate[l], ffn2_w_up[l], ffn2_w_down[l], w_out[l]))

    ab_re, ab_im, bmat_re, bmat_im, cmat_re, cmat_im = _ssm_params(
        ssm_a_re[l].reshape(n_groups, 1, n_state), ssm_a_im[l].reshape(n_groups, 1, n_state),
        ssm_log_dt[l].reshape(n_groups, 1, 1),
        jnp.swapaxes(ssm_b_re[l], 1, 2), jnp.swapaxes(ssm_b_im[l], 1, 2), ssm_c_re[l], ssm_c_im[l])
    u_seg = u.reshape(batch, n_seg, seg_len, d_ssm)
    s_seg = _ssm(u_seg, bmat_re, bmat_im, ab_re.reshape(1, -1), ab_im.reshape(1, -1), cmat_re, cmat_im,
                 row(ssm_d[l]), w_glu_bf, row(b_glu[l]), tb=tb)
    s = s_seg.reshape(tokens, d_ssm)

    x2 = _out_proj(a.reshape(tokens, d_attn), s, wo, x1, row(mix_post_g[l]), tm=tm)

    x3 = _ffn(x2, row(ffn2_pre_g[l]), wg2, wu2, wd2, row(ffn2_post_g[l]), tm=tm, tf=tf)
    return x3.reshape(batch, seq, d_model)
```
